```python
import math
import jax
import jax.numpy as jnp
from jax import lax
import numpy as np

D_MODEL = 4096
BATCH = 4
SEQ = 2048
DEPTH = 1
DEC_BATCH = 128
DEC_SEQ = 8
PAST_LEN = 2048
PAGE_SIZE = 128

MIX_WIDTH = D_MODEL
HEAD_DIM = 128
HGRN_WIDTH = MIX_WIDTH // 2
NSA_WIDTH = MIX_WIDTH - HGRN_WIDTH
HGRN_HEADS = HGRN_WIDTH // HEAD_DIM
HGRN_DK = HEAD_DIM
HGRN_DV = HEAD_DIM
HGRN_CHUNK = 64
NSA_HEADS = NSA_WIDTH // HEAD_DIM
NSA_KV_HEADS = 4
NSA_GROUP = NSA_HEADS // NSA_KV_HEADS
KV_WIDTH = NSA_KV_HEADS * HEAD_DIM
CMP_BLOCK = 32
CMP_STRIDE = 16
CMP_HIDDEN = 256
SEL_BLOCK = 64
SEL_TOPK = 16
SEL_LOCAL = 2
SEL_FORCE = 1e3
SEL_INVALID = -1e9
WINDOW = 512
SEL_QBLOCK = 32
WIN_QBLOCK = 128
N_BUCKETS = 32
MAX_DISTANCE = 128
PLE_DIM = 256
FFN_DIM = 256 * ((8 * D_MODEL // 3 + 255) // 256)
N_KV_SLOTS = 4
IN_COLS = 4 * HGRN_WIDTH + NSA_WIDTH + 6 * KV_WIDTH + 3 * NSA_HEADS
SCALE = HEAD_DIM ** -0.5
RMS_EPS = 1e-6
NEG_INF = -1e30

kernel_name = 'hymba_hgrn2_nsa_macaron_step'


def rmsnorm(x, gain):
    xf = x.astype(jnp.float32)
    y = xf * lax.rsqrt(jnp.mean(xf * xf, axis=-1, keepdims=True) + RMS_EPS)
    return (y * gain.astype(jnp.float32)).astype(x.dtype)


def swiglu(x, w_gate, w_up, w_down):
    return (jax.nn.silu(x @ w_gate) * (x @ w_up)) @ w_down


def t5_bucket(dist):
    dist = jnp.maximum(dist, 0)
    max_exact = N_BUCKETS // 2
    log_ratio = jnp.log(jnp.maximum(dist, 1).astype(jnp.float32) / max_exact) / math.log(MAX_DISTANCE / max_exact)
    large = jnp.minimum(max_exact + (log_ratio * (N_BUCKETS - max_exact)).astype(jnp.int32), N_BUCKETS - 1)
    return jnp.where(dist < max_exact, dist, large)


def head_bias(table, dist):
    b = jnp.moveaxis(table[t5_bucket(dist)], -1, 0)
    return b.reshape(NSA_KV_HEADS, NSA_GROUP, *dist.shape).astype(jnp.float32)


def masked_softmax(logits, mask):
    logits = jnp.where(mask, logits.astype(jnp.float32), NEG_INF)
    m = jnp.max(logits, axis=-1, keepdims=True)
    e = jnp.where(mask, jnp.exp(logits - m), 0.0)
    return e / jnp.maximum(jnp.sum(e, axis=-1, keepdims=True), 1e-30)


def hgrn2_recurrence(q, log_f, k, v, s0):
    B, T, H, DK = q.shape
    DV = v.shape[-1]
    C = math.gcd(T, HGRN_CHUNK)
    n = T // C

    def chunks(a):
        return a.reshape(B, n, C, H, a.shape[-1]).transpose(1, 0, 3, 2, 4)

    causal = jnp.tril(jnp.ones((C, C), dtype=bool))[:, :, None]

    def step(s, inp):
        qc, gc, kc, vc = inp
        a = jnp.cumsum(gc, axis=2)
        a_last = a[:, :, -1:, :]
        inter = jnp.einsum('bhtk,bhkv->bhtv', qc * jnp.exp(a), s)
        diff = a[:, :, :, None, :] - a[:, :, None, :, :]
        decay = jnp.where(causal, jnp.exp(jnp.where(causal, diff, 0.0)), 0.0)
        attn = jnp.einsum('bhtk,bhtsk,bhsk->bhts', qc, decay, kc)
        intra = jnp.einsum('bhts,bhsv->bhtv', attn, vc)
        s_new = jnp.exp(a_last)[:, :, 0, :, None] * s + jnp.einsum('bhsk,bhsv->bhkv', kc * jnp.exp(a_last - a), vc)
        return s_new, inter + intra

    s_fin, o = lax.scan(step, s0, (chunks(q), chunks(log_f), chunks(k), chunks(v)))
    o = o.transpose(1, 0, 3, 2, 4).reshape(B, T, H, DV)
    return o, s_fin


def compress_blocks(rows, pos, w1, w2):
    n_rows = rows.shape[0]
    n_cmp = (n_rows - CMP_BLOCK) // CMP_STRIDE + 1
    idx = jnp.arange(n_cmp)[:, None] * CMP_STRIDE + jnp.arange(CMP_BLOCK)[None, :]
    blk = rows[idx] + pos[None, :, None, :]
    flat = blk.transpose(0, 2, 1, 3).reshape(n_cmp, NSA_KV_HEADS, CMP_BLOCK * HEAD_DIM)
    return jax.nn.silu(flat @ w1) @ w2


def nsa_sequence(q, k_cmp, v_cmp, k_sel, v_sel, k_win, v_win, gates, nsa_p):
    pos_k, w1_k, w2_k, pos_v, w1_v, w2_v, k_norm_cmp, table = nsa_p
    T = q.shape[0]
    L = k_cmp.shape[0]
    qpos = (L - T) + jnp.arange(T)
    qg = q.reshape(T, NSA_KV_HEADS, NSA_GROUP, HEAD_DIM)

    kc = rmsnorm(compress_blocks(k_cmp, pos_k, w1_k, w2_k), k_norm_cmp)
    vc = compress_blocks(v_cmp, pos_v, w1_v, w2_v)
    n_cmp = kc.shape[0]
    c_end = jnp.arange(n_cmp) * CMP_STRIDE + CMP_BLOCK - 1
    dist_c = qpos[:, None] - c_end[None, :]
    s_c = jnp.einsum('tkgd,nkd->kgtn', qg, kc).astype(jnp.float32) * SCALE + head_bias(table, dist_c)
    p_c = masked_softmax(s_c, dist_c >= 0)
    o_cmp = jnp.einsum('kgtn,nkd->tkgd', p_c.astype(vc.dtype), vc)

    n_sel = -(-L // SEL_BLOCK)
    ci = jnp.arange(n_cmp)[:, None] * CMP_STRIDE
    sj = jnp.arange(n_sel)[None, :] * SEL_BLOCK
    overlap = ((ci < sj + SEL_BLOCK) & (ci + CMP_BLOCK > sj)).astype(jnp.float32)
    importance = jnp.einsum('ktn,ns->kts', jnp.sum(p_c, axis=1), overlap)
    j = jnp.arange(n_sel)
    lag = (qpos // SEL_BLOCK)[:, None] - j[None, :]
    forced = (j[None, :] == 0) | ((lag >= 0) & (lag < SEL_LOCAL))
    score = jnp.where(lag >= 0, importance + jnp.where(forced, SEL_FORCE, 0.0), SEL_INVALID)
    k_top = min(SEL_TOPK, n_sel)
    _, blk = lax.top_k(score, k_top)
    pad = n_sel * SEL_BLOCK - L
    k_sel_p = jnp.pad(k_sel, ((0, pad), (0, 0), (0, 0)))
    v_sel_p = jnp.pad(v_sel, ((0, pad), (0, 0), (0, 0)))
    qb_sel = math.gcd(T, SEL_QBLOCK)
    n_qb = T // qb_sel
    table_kg = table.reshape(N_BUCKETS, NSA_KV_HEADS, NSA_GROUP)
    kv_ar = jnp.arange(NSA_KV_HEADS)
    g_ar = jnp.arange(NSA_GROUP)

    def sel_block(args):
        qb, bb, pb = args
        tok = (bb[..., None] * SEL_BLOCK + jnp.arange(SEL_BLOCK)).reshape(NSA_KV_HEADS, qb_sel, k_top * SEL_BLOCK)
        kg = k_sel_p[tok, kv_ar[:, None, None]]
        vg = v_sel_p[tok, kv_ar[:, None, None]]
        dist = pb[None, :, None] - tok
        bias = table_kg[t5_bucket(dist)[:, None], kv_ar[:, None, None, None], g_ar[None, :, None, None]]
        s = jnp.einsum('qkgd,kqsd->kgqs', qb, kg).astype(jnp.float32) * SCALE + bias.astype(jnp.float32)
        p = masked_softmax(s, (dist >= 0)[:, None])
        return jnp.einsum('kgqs,kqsd->qkgd', p.astype(vg.dtype), vg)

    o_sel = lax.map(sel_block, (qg.reshape(n_qb, qb_sel, NSA_KV_HEADS, NSA_GROUP, HEAD_DIM),
                                blk.reshape(NSA_KV_HEADS, n_qb, qb_sel, k_top).transpose(1, 0, 2, 3),
                                qpos.reshape(n_qb, qb_sel)))
    o_sel = o_sel.reshape(T, NSA_KV_HEADS, NSA_GROUP, HEAD_DIM)

    l_win = k_win.shape[0]
    off = l_win - T
    qb_win = math.gcd(T, WIN_QBLOCK)
    n_qw = T // qb_win
    kw_len = WINDOW + qb_win
    k_win_p = jnp.pad(k_win, ((WINDOW, 0), (0, 0), (0, 0)))
    v_win_p = jnp.pad(v_win, ((WINDOW, 0), (0, 0), (0, 0)))

    def win_block(args):
        qb, i0 = args
        kb = lax.dynamic_slice_in_dim(k_win_p, i0 + off, kw_len, axis=0)
        vb = lax.dynamic_slice_in_dim(v_win_p, i0 + off, kw_len, axis=0)
        kidx = i0 + off - WINDOW + jnp.arange(kw_len)
        qidx = i0 + off + jnp.arange(qb_win)
        dist = qidx[:, None] - kidx[None, :]
        mask = (dist >= 0) & (dist < WINDOW) & (kidx >= 0)[None, :]
        s = jnp.einsum('qkgd,skd->kgqs', qb, kb).astype(jnp.float32) * SCALE + head_bias(table, dist)
        p = masked_softmax(s, mask)
        return jnp.einsum('kgqs,skd->qkgd', p.astype(vb.dtype), vb)

    o_win = lax.map(win_block, (qg.reshape(n_qw, qb_win, NSA_KV_HEADS, NSA_GROUP, HEAD_DIM),
                                jnp.arange(n_qw) * qb_win))
    o_win = o_win.reshape(T, NSA_KV_HEADS, NSA_GROUP, HEAD_DIM)

    g = gates.reshape(T, NSA_KV_HEADS, NSA_GROUP, 3)
    o = g[..., 0:1] * o_cmp + g[..., 1:2] * o_sel + g[..., 2:3] * o_win
    return o.reshape(T, NSA_WIDTH)


def setup_inputs(seed: int = 0) -> dict:
    key = jax.random.key(seed)
    ks = jax.random.split(key, 32)
    f32 = jnp.float32
    n_pages = PAST_LEN // PAGE_SIZE
    n_pool = (DEC_BATCH * n_pages * 5) // 4
    win_keep = min(WINDOW, PAST_LEN)

    def nrm(k, shape, scale=1.0):
        return scale * jax.random.normal(k, shape, f32)

    def gain(k, shape):
        return 1.0 + 0.01 * jax.random.normal(k, shape, f32)

    page_table = jax.random.permutation(ks[5], n_pool)[: DEC_BATCH * n_pages].reshape(DEC_BATCH, n_pages).astype(jnp.int32)
    return {
        'x_prompt': nrm(ks[0], (BATCH, SEQ, D_MODEL)),
        'x_sample': nrm(ks[1], (DEC_BATCH, DEC_SEQ, D_MODEL)),
        'cache_kv': nrm(ks[2], (DEPTH, n_pool, PAGE_SIZE, N_KV_SLOTS, NSA_KV_HEADS, HEAD_DIM)),
        'state_win_kv': nrm(ks[3], (DEPTH, DEC_BATCH, win_keep, 2, NSA_KV_HEADS, HEAD_DIM)),
        'state_hgrn': nrm(ks[4], (DEPTH, DEC_BATCH, HGRN_HEADS, HGRN_DK, HGRN_DV), 0.5),
        'page_table': page_table,
        'p_prompt': nrm(ks[6], (DEPTH, BATCH, SEQ, PLE_DIM)),
        'p_sample': nrm(ks[7], (DEPTH, DEC_BATCH, DEC_SEQ, PLE_DIM)),
        'ffn1_norm': gain(ks[8], (DEPTH, D_MODEL)),
        'ffn1_w_gate': nrm(ks[9], (DEPTH, D_MODEL, FFN_DIM), D_MODEL ** -0.5),
        'ffn1_w_up': nrm(ks[10], (DEPTH, D_MODEL, FFN_DIM), D_MODEL ** -0.5),
        'ffn1_w_down': nrm(ks[11], (DEPTH, FFN_DIM, D_MODEL), FFN_DIM ** -0.5),
        'mix_norm': gain(ks[12], (DEPTH, D_MODEL)),
        'w_in': nrm(ks[13], (DEPTH, D_MODEL, IN_COLS), D_MODEL ** -0.5),
        'w_out': nrm(ks[14], (DEPTH, MIX_WIDTH, D_MODEL), MIX_WIDTH ** -0.5),
        'hgrn_lb': nrm(ks[15], (DEPTH + 1, HGRN_WIDTH), 0.1),
        'hgrn_out_norm': gain(ks[16], (DEPTH, HGRN_DV)),
        'nsa_q_norm': gain(ks[17], (DEPTH, HEAD_DIM)),
        'nsa_k_norm': gain(ks[18], (DEPTH, 3, HEAD_DIM)),
        'cmp_pos': nrm(ks[19], (DEPTH, 2, CMP_BLOCK, HEAD_DIM), 0.1),
        'cmp_w1': nrm(ks[20], (DEPTH, 2, CMP_BLOCK * HEAD_DIM, CMP_HIDDEN), (CMP_BLOCK * HEAD_DIM) ** -0.5),
        'cmp_w2': nrm(ks[21], (DEPTH, 2, CMP_HIDDEN, HEAD_DIM), CMP_HIDDEN ** -0.5),
        'rel_bias_table': nrm(ks[22], (N_BUCKETS, NSA_HEADS), 0.5),
        'ffn2_norm': gain(ks[23], (DEPTH, D_MODEL)),
        'ffn2_w_gate': nrm(ks[24], (DEPTH, D_MODEL, FFN_DIM), D_MODEL ** -0.5),
        'ffn2_w_up': nrm(ks[25], (DEPTH, D_MODEL, FFN_DIM), D_MODEL ** -0.5),
        'ffn2_w_down': nrm(ks[26], (DEPTH, FFN_DIM, D_MODEL), FFN_DIM ** -0.5),
        'ple_norm': gain(ks[27], (DEPTH, D_MODEL)),
        'ple_w_gate': nrm(ks[28], (DEPTH, D_MODEL, D_MODEL), D_MODEL ** -0.5),
        'ple_w_proj': nrm(ks[29], (DEPTH, PLE_DIM, D_MODEL), PLE_DIM ** -0.5),
        'ple_post_norm': gain(ks[30], (DEPTH, D_MODEL)),
    }


def reference(x_prompt, x_sample, cache_kv, state_win_kv, state_hgrn, page_table, p_prompt, p_sample,
              ffn1_norm, ffn1_w_gate, ffn1_w_up, ffn1_w_down, mix_norm, w_in, w_out, hgrn_lb, hgrn_out_norm,
              nsa_q_norm, nsa_k_norm, cmp_pos, cmp_w1, cmp_w2, rel_bias_table,
              ffn2_norm, ffn2_w_gate, ffn2_w_up, ffn2_w_down, ple_norm, ple_w_gate, ple_w_proj, ple_post_norm):
    lower_bounds = jnp.cumsum(jax.nn.softmax(hgrn_lb.astype(jnp.float32), axis=0), axis=0)
    split_points = np.cumsum([HGRN_WIDTH] * 4 + [NSA_WIDTH] + [KV_WIDTH] * 6 + [3 * NSA_HEADS])[:-1].tolist()
    win_keep = state_win_kv.shape[2]

    def nsa_params(i):
        return (cmp_pos[i, 0], cmp_w1[i, 0], cmp_w2[i, 0], cmp_pos[i, 1], cmp_w1[i, 1], cmp_w2[i, 1],
                nsa_k_norm[i, 0], rel_bias_table)

    def nsa_prompt(i, q, kc, vc, ks, vs, kw, vw, g):
        nsa_p = nsa_params(i)

        def body(a):
            return nsa_sequence(*a, nsa_p)

        return lax.map(body, (q, kc, vc, ks, vs, kw, vw, g))

    def nsa_sample(i, q, kc, vc, ks, vs, kw, vw, g):
        nsa_p = nsa_params(i)
        pool = cache_kv[i]

        def body(a):
            q_b, kc_b, vc_b, ks_b, vs_b, kw_b, vw_b, g_b, pt_b, wb_b = a
            past = pool[pt_b].reshape(-1, N_KV_SLOTS, NSA_KV_HEADS, HEAD_DIM)
            return nsa_sequence(q_b,
                                jnp.concatenate([past[:, 0], kc_b], axis=0),
                                jnp.concatenate([past[:, 1], vc_b], axis=0),
                                jnp.concatenate([past[:, 2], ks_b], axis=0),
                                jnp.concatenate([past[:, 3], vs_b], axis=0),
                                jnp.concatenate([wb_b[:, 0], kw_b], axis=0),
                                jnp.concatenate([wb_b[:, 1], vw_b], axis=0),
                                g_b, nsa_p)

        return lax.map(body, (q, kc, vc, ks, vs, kw, vw, g, page_table, state_win_kv[i]))

    def run_layer(i, x, pemb, s0, nsa_group):
        x = x + 0.5 * swiglu(rmsnorm(x, ffn1_norm[i]), ffn1_w_gate[i], ffn1_w_up[i], ffn1_w_down[i])
        h = rmsnorm(x, mix_norm[i])
        B, T = h.shape[0], h.shape[1]
        z = h @ w_in[i]
        hq, hf, hi, hg, nq, kc, vc, ks, vs, kw, vw, ng = jnp.split(z, split_points, axis=-1)
        hshape = (B, T, HGRN_HEADS, HGRN_DK)
        lb = lower_bounds[i].reshape(HGRN_HEADS, HGRN_DK)
        f_gate = lb + (1.0 - lb) * jax.nn.sigmoid(hf.astype(jnp.float32).reshape(hshape))
        o_h, s_fin = hgrn2_recurrence(hq.astype(jnp.float32).reshape(hshape), jnp.log(f_gate), 1.0 - f_gate,
                                      hi.astype(jnp.float32).reshape(B, T, HGRN_HEADS, HGRN_DV),
                                      s0.astype(jnp.float32))
        o_h = rmsnorm(o_h, hgrn_out_norm[i]) * jax.nn.silu(hg.astype(jnp.float32).reshape(B, T, HGRN_HEADS, HGRN_DV))
        o_h = o_h.reshape(B, T, HGRN_WIDTH).astype(x.dtype)
        kv_shape = (B, T, NSA_KV_HEADS, HEAD_DIM)
        q_n = rmsnorm(nq.reshape(B, T, NSA_HEADS, HEAD_DIM), nsa_q_norm[i])
        k_cmp = kc.reshape(kv_shape)
        v_cmp = vc.reshape(kv_shape)
        k_sel = rmsnorm(ks.reshape(kv_shape), nsa_k_norm[i, 1])
        v_sel = vs.reshape(kv_shape)
        k_win = rmsnorm(kw.reshape(kv_shape), nsa_k_norm[i, 2])
        v_win = vw.reshape(kv_shape)
        gates = jax.nn.sigmoid(ng.astype(jnp.float32)).reshape(B, T, NSA_HEADS, 3).astype(x.dtype)
        o_n = nsa_group(i, q_n, k_cmp, v_cmp, k_sel, v_sel, k_win, v_win, gates)
        x = x + jnp.concatenate([o_h, o_n], axis=-1) @ w_out[i]
        x = x + 0.5 * swiglu(rmsnorm(x, ffn2_norm[i]), ffn2_w_gate[i], ffn2_w_up[i], ffn2_w_down[i])
        gate = jax.nn.sigmoid(rmsnorm(x, ple_norm[i]) @ ple_w_gate[i])
        x = x + gate * rmsnorm(pemb @ ple_w_proj[i], ple_post_norm[i])
        kv_rows = jnp.stack([k_cmp, v_cmp, k_sel, v_sel], axis=2)
        win_rows = jnp.stack([k_win, v_win], axis=2)
        return x, kv_rows, win_rows, s_fin.astype(s0.dtype)

    y_p, y_s = x_prompt, x_sample
    pk, pw, ph, sk, sw, sh = [], [], [], [], [], []
    for i in range(DEPTH):
        s0_prompt = jnp.zeros((x_prompt.shape[0], HGRN_HEADS, HGRN_DK, HGRN_DV), state_hgrn.dtype)
        y_p, kv_p, win_p, h_p = run_layer(i, y_p, p_prompt[i], s0_prompt, nsa_prompt)
        y_s, kv_s, win_s, h_s = run_layer(i, y_s, p_sample[i], state_hgrn[i], nsa_sample)
        pk.append(kv_p)
        pw.append(win_p[:, -min(WINDOW, win_p.shape[1]):])
        ph.append(h_p)
        sk.append(kv_s)
        sw.append(jnp.concatenate([state_win_kv[i], win_s], axis=1)[:, -win_keep:])
        sh.append(h_s)
    return (y_p, y_s, jnp.stack(pk), jnp.stack(pw), jnp.stack(ph), jnp.stack(sk), jnp.stack(sw), jnp.stack(sh))
```

```python
import functools
import math

import numpy as np
import jax
import jax.numpy as jnp
from jax import lax
from jax.experimental import pallas as pl
from jax.experimental.pallas import tpu as pltpu

F32 = jnp.float32
BF16 = jnp.bfloat16

LANE = 128
SUBLANE = 8
VMEM_LIMIT_BYTES = 56 * 1024 * 1024

HEAD_DIM = 128
RMS_EPS = 1e-6
NEG_INF = -1e30


def _cparams(*sem):
    return pltpu.CompilerParams(dimension_semantics=sem, vmem_limit_bytes=VMEM_LIMIT_BYTES)


def _tile(n, pref):
    if n <= pref:
        return n
    t = pref
    while t >= SUBLANE:
        if n % t == 0:
            return t
        t -= SUBLANE
    return n


def _rms(x, gain):
    ms = jnp.mean(x * x, axis=-1, keepdims=True)
    return x * lax.rsqrt(ms + RMS_EPS) * gain


def _norm_rows_to(x_ref, gain_ref, dst_ref):
    rows = x_ref.shape[0]
    rc = 32 if rows % 32 == 0 else rows

    def body(i, carry):
        r = pl.multiple_of(i * rc, rc)
        dst_ref[pl.ds(r, rc), :] = _rms(x_ref[pl.ds(r, rc), :], gain_ref[...]).astype(dst_ref.dtype)
        return carry

    lax.fori_loop(0, rows // rc, body, 0)


def _ffn_kernel(x_ref, gain_ref, wg_ref, wu_ref, wd_ref, o_ref, hn_ref):
    @pl.when(pl.program_id(1) == 0)
    def _():
        _norm_rows_to(x_ref, gain_ref, hn_ref)
        o_ref[...] = x_ref[...]

    h = hn_ref[...]
    g = jnp.dot(h, wg_ref[...], preferred_element_type=F32)
    u = jnp.dot(h, wu_ref[...], preferred_element_type=F32)
    a = (g * jax.nn.sigmoid(g) * (0.5 * u)).astype(BF16)
    o_ref[...] += jnp.dot(a, wd_ref[...], preferred_element_type=F32)


def ffn_residual(x, gain, wg, wu, wd, tm_pref=512, tf_pref=256):
    n, d = x.shape
    f = wg.shape[1]
    tm, tf = _tile(n, tm_pref), _tile(f, tf_pref)
    return pl.pallas_call(
        _ffn_kernel,
        grid=(n // tm, f // tf),
        in_specs=[
            pl.BlockSpec((tm, d), lambda i, j: (i, 0)),
            pl.BlockSpec((1, d), lambda i, j: (0, 0)),
            pl.BlockSpec((d, tf), lambda i, j: (0, j)),
            pl.BlockSpec((d, tf), lambda i, j: (0, j)),
            pl.BlockSpec((tf, d), lambda i, j: (j, 0)),
        ],
        out_specs=pl.BlockSpec((tm, d), lambda i, j: (i, 0)),
        out_shape=jax.ShapeDtypeStruct((n, d), F32),
        scratch_shapes=[pltpu.VMEM((tm, d), BF16)],
        compiler_params=_cparams("parallel", "arbitrary"),
        name="ffn_residual",
    )(x, gain.reshape(1, d), wg, wu, wd)


def _proj_kernel(epilogue, n_out, x_ref, gain_ref, w_ref, hg_ref, *rest):
    outs, hn_ref = rest[:n_out], rest[n_out]
    j = pl.program_id(1)

    @pl.when(j == 0)
    def _():
        _norm_rows_to(x_ref, gain_ref, hn_ref)

    acc = jnp.dot(hn_ref[...], w_ref[...], preferred_element_type=F32)
    epilogue(j, acc, hg_ref, outs)


def _epi_plain(j, acc, hg_ref, outs):
    outs[0][...] = acc


def _epi_sigmoid(j, acc, hg_ref, outs):
    outs[0][...] = jax.nn.sigmoid(acc)


def _epi_headnorm(normed_tiles, scale, out_dtypes, j, acc, hg_ref, outs):
    flag = functools.reduce(jnp.logical_or, [j == t for t in normed_tiles])
    gain = hg_ref[0]
    for h in range(acc.shape[1] // HEAD_DIM):
        sl = acc[:, h * HEAD_DIM:(h + 1) * HEAD_DIM]
        val = jnp.where(flag, _rms(sl, gain), sl)
        if scale != 1.0:
            val = val * scale
        for o_ref, dt in zip(outs, out_dtypes):
            o_ref[:, h * HEAD_DIM:(h + 1) * HEAD_DIM] = val.astype(dt)


def norm_project(x, gain, w, epilogue, out_dtypes, head_gain=None, tm_pref=512, tn_pref=512):
    n, d = x.shape
    ncols = w.shape[1]
    tm, tn = _tile(n, tm_pref), _tile(ncols, tn_pref)
    if head_gain is None:
        head_gain = jnp.ones((ncols // tn, 1, HEAD_DIM), F32)
    n_out = len(out_dtypes)
    out = pl.pallas_call(
        functools.partial(_proj_kernel, epilogue, n_out),
        grid=(n // tm, ncols // tn),
        in_specs=[
            pl.BlockSpec((tm, d), lambda i, j: (i, 0)),
            pl.BlockSpec((1, d), lambda i, j: (0, 0)),
            pl.BlockSpec((d, tn), lambda i, j: (0, j)),
            pl.BlockSpec((1, 1, HEAD_DIM), lambda i, j: (j, 0, 0)),
        ],
        out_specs=[pl.BlockSpec((tm, tn), lambda i, j: (i, j)) for _ in out_dtypes],
        out_shape=[jax.ShapeDtypeStruct((n, ncols), dt) for dt in out_dtypes],
        scratch_shapes=[pltpu.VMEM((tm, d), BF16)],
        compiler_params=_cparams("parallel", "arbitrary"),
        name="norm_project",
    )(x, gain.reshape(1, d), w, head_gain)
    return out


def _outproj_kernel(x_ref, a_ref, b_ref, wa_ref, wb_ref, o_ref):
    acc = jnp.dot(a_ref[...].astype(BF16), wa_ref[...], preferred_element_type=F32)
    acc += jnp.dot(b_ref[...].astype(BF16), wb_ref[...], preferred_element_type=F32)
    o_ref[...] = x_ref[...] + acc


def out_project_residual(x, a, b, wa, wb, tm_pref=512, tn_pref=1024):
    n, d = x.shape
    ka, kb = a.shape[1], b.shape[1]
    tm, tn = _tile(n, tm_pref), _tile(d, tn_pref)
    return pl.pallas_call(
        _outproj_kernel,
        grid=(n // tm, d // tn),
        in_specs=[
            pl.BlockSpec((tm, tn), lambda i, j: (i, j)),
            pl.BlockSpec((tm, ka), lambda i, j: (i, 0)),
            pl.BlockSpec((tm, kb), lambda i, j: (i, 0)),
            pl.BlockSpec((ka, tn), lambda i, j: (0, j)),
            pl.BlockSpec((kb, tn), lambda i, j: (0, j)),
        ],
        out_specs=pl.BlockSpec((tm, tn), lambda i, j: (i, j)),
        out_shape=jax.ShapeDtypeStruct((n, d), F32),
        compiler_params=_cparams("parallel", "arbitrary"),
        name="out_project_residual",
    )(x, a, b, wa, wb)


def _ple_kernel(tn, x_ref, gain_ref, wg_ref, p_ref, wp_ref, pg_ref, o_ref, hn_ref, pe_ref):
    j = pl.program_id(1)

    @pl.when(j == 0)
    def _():
        _norm_rows_to(x_ref, gain_ref, hn_ref)
        pe_ref[...] = jnp.dot(p_ref[...].astype(BF16), wp_ref[...], preferred_element_type=F32)
        _norm_rows_to(pe_ref, pg_ref, pe_ref)

    c = pl.multiple_of(j * tn, LANE)
    gate = jax.nn.sigmoid(jnp.dot(hn_ref[...], wg_ref[...], preferred_element_type=F32))
    o_ref[...] = x_ref[:, pl.ds(c, tn)] + gate * pe_ref[:, pl.ds(c, tn)]


def ple_residual(x, gain, wg, p, wp, post_gain, tm_pref=512, tn_pref=512):
    n, d = x.shape
    pd = p.shape[1]
    tm, tn = _tile(n, tm_pref), _tile(d, tn_pref)
    return pl.pallas_call(
        functools.partial(_ple_kernel, tn),
        grid=(n // tm, d // tn),
        in_specs=[
            pl.BlockSpec((tm, d), lambda i, j: (i, 0)),
            pl.BlockSpec((1, d), lambda i, j: (0, 0)),
            pl.BlockSpec((d, tn), lambda i, j: (0, j)),
            pl.BlockSpec((tm, pd), lambda i, j: (i, 0)),
            pl.BlockSpec((pd, d), lambda i, j: (0, 0)),
            pl.BlockSpec((1, d), lambda i, j: (0, 0)),
        ],
        out_specs=pl.BlockSpec((tm, tn), lambda i, j: (i, j)),
        out_shape=jax.ShapeDtypeStruct((n, d), F32),
        scratch_shapes=[pltpu.VMEM((tm, d), BF16), pltpu.VMEM((tm, d), F32)],
        compiler_params=_cparams("parallel", "arbitrary"),
        name="ple_residual",
    )(x, gain.reshape(1, d), wg, p, wp, post_gain.reshape(1, d))


HGRN_DIAG = SUBLANE


def _split3_dot(lhs_bf16, x):
    hi = x.astype(BF16)
    r1 = x - hi.astype(F32)
    mid = r1.astype(BF16)
    lo = (r1 - mid.astype(F32)).astype(BF16)
    acc = jnp.dot(lhs_bf16, hi, preferred_element_type=F32)
    acc += jnp.dot(lhs_bf16, mid, preferred_element_type=F32)
    acc += jnp.dot(lhs_bf16, lo, preferred_element_type=F32)
    return acc


def _bcast_row_in_blocks(a, bs, row):
    c, k = a.shape
    a3 = a.reshape(c // bs, bs, k)
    return jnp.broadcast_to(a3[:, row:row + 1, :], (c // bs, bs, k)).reshape(c, k)


def _nt_dot(a, b):
    return lax.dot_general(a, b, (((1,), (1,)), ((), ())), preferred_element_type=F32)


def _hgrn_kernel(layer, has_s0, q_ref, f_ref, i_ref, g_ref, lb_ref, og_ref, *rest):
    if has_s0:
        s0_ref, o_ref, sfin_ref, st_ref = rest
    else:
        o_ref, sfin_ref, st_ref = rest
    ci = pl.program_id(2)
    c = q_ref.shape[0]

    @pl.when(ci == 0)
    def _():
        if has_s0:
            st_ref[...] = s0_ref[...].T
        else:
            st_ref[...] = jnp.zeros_like(st_ref)

    lbr = lb_ref[...]
    e = jnp.exp(lbr - jnp.max(lbr, axis=0, keepdims=True))
    lb = jnp.sum(e[:layer + 1], axis=0, keepdims=True) / jnp.sum(e, axis=0, keepdims=True)

    q = q_ref[...]
    fg = lb + (1.0 - lb) * jax.nn.sigmoid(f_ref[...])
    logf = jnp.log(fg)
    kk = 1.0 - fg
    v = i_ref[...]

    ti = lax.broadcasted_iota(jnp.int32, (c, c), 0)
    si = lax.broadcasted_iota(jnp.int32, (c, c), 1)
    tril = (si <= ti)
    a = _split3_dot(tril.astype(BF16), logf)
    a_last = a[c - 1:c, :]

    kk_b = kk.astype(BF16)
    d = HGRN_DIAG
    xs = [(q * jnp.exp(jnp.minimum(a - _bcast_row_in_blocks(a, d, j), 0.0))).astype(BF16) for j in range(d)]
    res = _nt_dot(jnp.concatenate(xs, axis=0), kk_b)
    attn = jnp.zeros((c, c), F32)
    for j in range(d):
        attn += jnp.where((si % d) == j, res[j * c:(j + 1) * c], 0.0)
    attn = jnp.where(((si // d) == (ti // d)) & tril, attn, 0.0)
    bs = 2 * d
    while bs <= c:
        half = bs // 2
        bnd = _bcast_row_in_blocks(a, bs, half - 1)
        qe = (q * jnp.exp(jnp.minimum(a - bnd, 0.0))).astype(BF16)
        ke = (kk * jnp.exp(jnp.minimum(bnd - a, 0.0))).astype(BF16)
        m = ((si // bs) == (ti // bs)) & ((ti % bs) >= half) & ((si % bs) < half)
        attn += jnp.where(m, _nt_dot(qe, ke), 0.0)
        bs *= 2

    st = st_ref[...]
    v_b = v.astype(BF16)
    o = jnp.dot(attn.astype(BF16), v_b, preferred_element_type=F32)
    o += _nt_dot((q * jnp.exp(a)).astype(BF16), st.astype(BF16))
    kd = (kk * jnp.exp(a_last - a)).astype(BF16)
    st_new = st * jnp.exp(a_last) + jnp.dot(v_b.T, kd, preferred_element_type=F32)
    st_ref[...] = st_new

    gr = g_ref[...]
    o_ref[...] = (_rms(o, og_ref[...]) * (gr * jax.nn.sigmoid(gr))).astype(o_ref.dtype)

    @pl.when(ci == pl.num_programs(2) - 1)
    def _():
        sfin_ref[...] = st_new.T


def hgrn2_mix(zh, hgrn_lb, out_gain, layer, batch, seq, heads, s0=None, chunk_pref=128):
    n = batch * seq
    dk = HEAD_DIM
    c = _tile(seq, chunk_pref)
    assert c % HGRN_DIAG == 0 and (c // HGRN_DIAG) & (c // HGRN_DIAG - 1) == 0
    nc = seq // c
    nl = hgrn_lb.shape[0]

    def zspec(sec):
        return pl.BlockSpec((c, dk), lambda b, h, ci: (b * nc + ci, sec * heads + h))

    in_specs = [zspec(0), zspec(1), zspec(2), zspec(3),
                pl.BlockSpec((nl, dk), lambda b, h, ci: (0, h)),
                pl.BlockSpec((1, dk), lambda b, h, ci: (0, 0))]
    args = [zh, zh, zh, zh, hgrn_lb, out_gain.reshape(1, dk)]
    if s0 is not None:
        in_specs.append(pl.BlockSpec((None, None, dk, dk), lambda b, h, ci: (b, h, 0, 0)))
        args.append(s0)
    return pl.pallas_call(
        functools.partial(_hgrn_kernel, layer, s0 is not None),
        grid=(batch, heads, nc),
        in_specs=in_specs,
        out_specs=[pl.BlockSpec((c, dk), lambda b, h, ci: (b * nc + ci, h)),
                   pl.BlockSpec((None, None, dk, dk), lambda b, h, ci: (b, h, 0, 0))],
        out_shape=[jax.ShapeDtypeStruct((n, heads * dk), BF16 if c % 16 == 0 else F32),
                   jax.ShapeDtypeStruct((batch, heads, dk, dk), F32)],
        scratch_shapes=[pltpu.VMEM((dk, dk), F32)],
        compiler_params=_cparams("parallel", "parallel", "arbitrary"),
        name="hgrn2_mix",
    )(*args)


NSA_KV_HEADS = 4
NSA_GROUP = 4
NSA_HEADS = NSA_KV_HEADS * NSA_GROUP
KV_LANES = NSA_KV_HEADS * HEAD_DIM
PAGE = 128
CMP_BLOCK = 32
CMP_STRIDE = 16
CMP_PER_PAGE = PAGE // CMP_STRIDE
N_CMP_PAD = 128
SEL_BLOCK = 64
SEL_TOPK = 16
SEL_LOCAL = 2
SEL_FORCE = 1e3
SEL_INVALID = -1e9
WINDOW = 512
N_BUCKETS = 32
MAX_DISTANCE = 128
MASKED = -1e29


def _bias_kernel(q0, k0, kstride, ncols_valid, window, table_ref, o_ref):
    h = pl.program_id(0)
    rt, cols = o_ref.shape
    r = lax.broadcasted_iota(jnp.int32, (rt, cols), 0) + pl.program_id(1) * rt
    c = lax.broadcasted_iota(jnp.int32, (rt, cols), 1)
    dist = (q0 + r) - (k0 + c * kstride)
    dpos = jnp.maximum(dist, 0)
    max_exact = N_BUCKETS // 2
    log_ratio = jnp.log(jnp.maximum(dpos, 1).astype(F32) / max_exact) / math.log(MAX_DISTANCE / max_exact)
    large = jnp.minimum(max_exact + (log_ratio * (N_BUCKETS - max_exact)).astype(jnp.int32), N_BUCKETS - 1)
    bucket = jnp.where(dpos < max_exact, dpos, large)
    acc = jnp.zeros((rt, cols), F32)
    for b in range(N_BUCKETS):
        acc = jnp.where(bucket == b, table_ref[b, h], acc)
    valid = (dist >= 0) & (c < ncols_valid)
    if window is not None:
        valid = valid & (dist < window)
    o_ref[...] = jnp.where(valid, acc, NEG_INF)


def rel_bias(table, rows, cols, q0, k0, kstride, ncols_valid, window=None):
    rt = _tile(rows, 256)
    return pl.pallas_call(
        functools.partial(_bias_kernel, q0, k0, kstride, ncols_valid, window),
        grid=(NSA_HEADS, rows // rt),
        in_specs=[pl.BlockSpec(memory_space=pltpu.SMEM)],
        out_specs=pl.BlockSpec((None, rt, cols), lambda h, i: (h, i, 0)),
        out_shape=jax.ShapeDtypeStruct((NSA_HEADS, rows, cols), F32),
        compiler_params=_cparams("parallel", "parallel"),
        name="rel_bias",
    )(table)


def _compress_kernel(pt_ref, *refs):
    n_in = 2 * NSA_KV_HEADS
    page_refs = refs[:n_in]
    (wck_ref, wcv_ref, posk_ref, posv_ref, w1k_ref, w1v_ref, w2k_ref, w2v_ref, kn_ref,
     kc_ref, vc_ref, xs_ref) = refs[n_in:]
    p = pl.program_id(1)
    m_rows = NSA_KV_HEADS * N_CMP_PAD
    for slot in range(2):
        for kvh in range(NSA_KV_HEADS):
            page_ref = page_refs[slot * NSA_KV_HEADS + kvh]
            row0 = pl.multiple_of(kvh * N_CMP_PAD + p * CMP_PER_PAGE, CMP_PER_PAGE)
            for r in range(CMP_STRIDE):
                xs_ref[slot, pl.ds(row0, CMP_PER_PAGE), r * HEAD_DIM:(r + 1) * HEAD_DIM] = (
                    page_ref[pl.ds(r, CMP_PER_PAGE, stride=CMP_STRIDE), :])

    @pl.when(p == pl.num_programs(1) - 1)
    def _():
        hid = w2k_ref.shape[0]
        rowid = lax.broadcasted_iota(jnp.int32, (m_rows, HEAD_DIM), 0) % N_CMP_PAD
        for slot, (wc, pos, w1, w2, out_ref) in enumerate((
                (wck_ref, posk_ref, w1k_ref, w2k_ref, kc_ref), (wcv_ref, posv_ref, w1v_ref, w2v_ref, vc_ref))):
            pq = jnp.dot(xs_ref[slot].astype(BF16), wc[...], preferred_element_type=F32)
            nxt = pltpu.roll(pq[:, hid:], m_rows - 1, 0)
            posb = jnp.dot(jnp.broadcast_to(pos[...], (SUBLANE, pos.shape[1])).astype(BF16), w1[...],
                           preferred_element_type=F32)[0:1]
            hcur = pq[:, :hid] + nxt + posb
            act = (hcur * jax.nn.sigmoid(hcur)).astype(BF16)
            out = jnp.dot(act, w2[...], preferred_element_type=F32)
            if slot == 0:
                out = _rms(out, kn_ref[...])
            out = jnp.where(rowid == N_CMP_PAD - 1, 0.0, out)
            out_ref[...] = out.reshape(NSA_KV_HEADS, N_CMP_PAD, HEAD_DIM).astype(out_ref.dtype)


def compress_cache(pages, page_table, lane_block, w1, w2, pos, k_norm_cmp):
    bsz, n_pages = page_table.shape
    assert n_pages * CMP_PER_PAGE == N_CMP_PAD
    hid = w2.shape[1]
    wcat = [w1[s].reshape(2, CMP_STRIDE, HEAD_DIM, hid).transpose(1, 2, 0, 3).reshape(CMP_STRIDE * HEAD_DIM, 2 * hid).astype(BF16)
            for s in range(2)]
    posf = [pos[s].reshape(1, CMP_BLOCK * HEAD_DIM) for s in range(2)]
    w1b = [w1[s].astype(BF16) for s in range(2)]
    w2b = [w2[s].astype(BF16) for s in range(2)]
    full = lambda a: pl.BlockSpec(a.shape, lambda b, p, pt: (0,) * a.ndim)
    n_in = 2 * NSA_KV_HEADS

    def page_map(lane_blk, b, p, pt):
        return (pt[b, p], 0, lane_blk)

    ins = [wcat[0], wcat[1], posf[0], posf[1], w1b[0], w1b[1], w2b[0], w2b[1], k_norm_cmp.reshape(1, HEAD_DIM)]
    out_spec = pl.BlockSpec((None, NSA_KV_HEADS, N_CMP_PAD, HEAD_DIM), lambda b, p, pt: (b, 0, 0, 0))
    out_sds = jax.ShapeDtypeStruct((bsz, NSA_KV_HEADS, N_CMP_PAD, HEAD_DIM), BF16)
    return pl.pallas_call(
        _compress_kernel,
        grid_spec=pltpu.PrefetchScalarGridSpec(
            num_scalar_prefetch=1,
            grid=(bsz, n_pages),
            in_specs=[pl.BlockSpec((None, PAGE, HEAD_DIM), functools.partial(page_map, lane_block * n_in + j))
                      for j in range(n_in)] + [full(a) for a in ins],
            out_specs=[out_spec, out_spec],
            scratch_shapes=[pltpu.VMEM((2, NSA_KV_HEADS * N_CMP_PAD, CMP_STRIDE * HEAD_DIM), F32)],
        ),
        out_shape=[out_sds, out_sds],
        compiler_params=_cparams("parallel", "arbitrary"),
        name="compress_cache",
    )(page_table, *([pages] * n_in), *ins)


def _heads_to_rows(q_ref):
    return jnp.concatenate([q_ref[:, g * HEAD_DIM:(g + 1) * HEAD_DIM] for g in range(NSA_GROUP)], axis=0).astype(BF16)


def _dot3_lhs(x, rhs_bf16):
    hi = x.astype(BF16)
    r1 = x - hi.astype(F32)
    mid = r1.astype(BF16)
    lo = (r1 - mid.astype(F32)).astype(BF16)
    acc = jnp.dot(hi, rhs_bf16, preferred_element_type=F32)
    acc += jnp.dot(mid, rhs_bf16, preferred_element_type=F32)
    acc += jnp.dot(lo, rhs_bf16, preferred_element_type=F32)
    return acc


def _cmp_topk_kernel(q0, n_sel, q_ref, kc_ref, vc_ref, bias_ref, ovl_ref, exp_ref, ocmp_ref, mask_ref):
    qi = pl.program_id(2)
    tq = q_ref.shape[0]
    q4 = _heads_to_rows(q_ref)
    bias = bias_ref[...].reshape(NSA_GROUP * tq, N_CMP_PAD)
    s = _nt_dot(q4, kc_ref[...]) + bias
    m = jnp.max(s, axis=-1, keepdims=True)
    e = jnp.where(bias > MASKED, jnp.exp(s - m), 0.0)
    p = e / jnp.maximum(jnp.sum(e, axis=-1, keepdims=True), 1e-30)
    o = jnp.dot(p.astype(BF16), vc_ref[...], preferred_element_type=F32)
    psum = jnp.zeros((tq, N_CMP_PAD), F32)
    for g in range(NSA_GROUP):
        ocmp_ref[:, g * HEAD_DIM:(g + 1) * HEAD_DIM] = o[g * tq:(g + 1) * tq]
        psum += p[g * tq:(g + 1) * tq]
    imp = _dot3_lhs(psum, ovl_ref[...])
    lane = lax.broadcasted_iota(jnp.int32, (tq, LANE), 1)
    qpos = q0 + qi * tq + lax.broadcasted_iota(jnp.int32, (tq, LANE), 0)
    lag = qpos // SEL_BLOCK - lane
    forced = (lane == 0) | ((lag >= 0) & (lag < SEL_LOCAL))
    score = jnp.where(lag >= 0, imp + jnp.where(forced, SEL_FORCE, 0.0), SEL_INVALID)
    cnt = jnp.zeros((tq, LANE), jnp.int32)
    for j in range(n_sel):
        col = score[:, j:j + 1]
        beats = jnp.where(col > score, 1, jnp.where(col == score, jnp.where(lane > j, 1, 0), 0))
        cnt += beats
    sel = jnp.where(cnt < min(SEL_TOPK, n_sel), jnp.where(lane < n_sel, 1.0, 0.0), 0.0)
    lp = mask_ref.shape[1]
    ex = jnp.dot(sel.astype(BF16), exp_ref[...], preferred_element_type=F32)
    kpos = lax.broadcasted_iota(jnp.int32, (tq, lp), 1)
    qp = q0 + qi * tq + lax.broadcasted_iota(jnp.int32, (tq, lp), 0)
    keep = jnp.where(kpos <= qp, ex, 0.0) > 0.5
    mask_ref[...] = jnp.where(keep, 0.0, NEG_INF).astype(mask_ref.dtype)


def cmp_attention_topk(q, kc, vc, bias_c, bsz, seq, q0, key_len, tq_pref=256):
    n = bsz * seq
    tq = _tile(seq, tq_pref)
    nq = seq // tq
    n_sel = -(-key_len // SEL_BLOCK)
    lp = -(-(n_sel * SEL_BLOCK) // LANE) * LANE
    ci = np.arange(N_CMP_PAD)[:, None] * CMP_STRIDE
    sj = np.arange(LANE)[None, :] * SEL_BLOCK
    overlap = jnp.asarray(((ci < sj + SEL_BLOCK) & (ci + CMP_BLOCK > sj)).astype(np.float32), BF16)
    expand = jnp.asarray((np.arange(lp)[None, :] // SEL_BLOCK == np.arange(LANE)[:, None]).astype(np.float32), BF16)
    mask_dtype = BF16 if tq % 16 == 0 else F32
    return pl.pallas_call(
        functools.partial(_cmp_topk_kernel, q0, n_sel),
        grid=(bsz, NSA_KV_HEADS, nq),
        in_specs=[
            pl.BlockSpec((tq, NSA_GROUP * HEAD_DIM), lambda b, k, i: (b * nq + i, k)),
            pl.BlockSpec((None, None, N_CMP_PAD, HEAD_DIM), lambda b, k, i: (b, k, 0, 0)),
            pl.BlockSpec((None, None, N_CMP_PAD, HEAD_DIM), lambda b, k, i: (b, k, 0, 0)),
            pl.BlockSpec((NSA_GROUP, tq, N_CMP_PAD), lambda b, k, i: (k, i, 0)),
            pl.BlockSpec((N_CMP_PAD, LANE), lambda b, k, i: (0, 0)),
            pl.BlockSpec((LANE, lp), lambda b, k, i: (0, 0)),
        ],
        out_specs=[
            pl.BlockSpec((tq, NSA_GROUP * HEAD_DIM), lambda b, k, i: (b * nq + i, k)),
            pl.BlockSpec((None, None, tq, lp), lambda b, k, i: (b, k, i, 0)),
        ],
        out_shape=[jax.ShapeDtypeStruct((n, NSA_HEADS * HEAD_DIM), F32),
                   jax.ShapeDtypeStruct((bsz, NSA_KV_HEADS, seq, lp), mask_dtype)],
        compiler_params=_cparams("parallel", "parallel", "parallel"),
        name="cmp_attention_topk",
    )(q, kc, vc, bias_c, overlap, expand)


def _softmax_step(q4, k, v, bias, carry):
    m, l, acc = carry
    s = _nt_dot(q4, k) + bias
    m_new = jnp.maximum(m, jnp.max(s, axis=-1, keepdims=True))
    alpha = jnp.exp(m - m_new)
    pr = jnp.where(bias > MASKED, jnp.exp(s - m_new), 0.0)
    l = alpha * l + jnp.sum(pr, axis=-1, keepdims=True)
    acc = alpha * acc + jnp.dot(pr.astype(BF16), v, preferred_element_type=F32)
    return m_new, l, acc


def _nsa_prompt_kernel(q_ref, ks_ref, vs_ref, kw_ref, vw_ref, bias_ref, mask_ref, ocmp_ref, gate_ref, o_ref):
    qi = pl.program_id(2)
    tq = q_ref.shape[0]
    rows = NSA_GROUP * tq
    q4 = _heads_to_rows(q_ref)
    n_win = WINDOW // tq

    def bias_tile(delta):
        off = pl.multiple_of(delta * tq, tq)
        return bias_ref[:, pl.ds(off, tq), :].reshape(rows, tq)

    def kv_chunk(k_ref, v_ref, kc):
        off = pl.multiple_of(kc * tq, tq)
        return k_ref[pl.ds(off, tq), :], v_ref[pl.ds(off, tq), :]

    init = (jnp.full((rows, 1), NEG_INF, F32), jnp.zeros((rows, 1), F32), jnp.zeros((rows, HEAD_DIM), F32))

    def sel_body(kc, carry):
        off = pl.multiple_of(kc * tq, tq)
        mb = mask_ref[:, pl.ds(off, tq)].astype(F32)
        bias = bias_tile(jnp.minimum(qi - kc, 2)) + jnp.concatenate([mb] * NSA_GROUP, axis=0)
        k, v = kv_chunk(ks_ref, vs_ref, kc)
        return _softmax_step(q4, k, v, bias, carry)

    _, l, acc = lax.fori_loop(0, qi + 1, sel_body, init)
    o_sel = acc / l

    def win_body(kc, carry):
        k, v = kv_chunk(kw_ref, vw_ref, kc)
        return _softmax_step(q4, k, v, bias_tile(qi - kc), carry)

    _, l, acc = lax.fori_loop(jnp.maximum(qi - n_win, 0), qi + 1, win_body, init)
    o_win = acc / l

    gates = gate_ref[...]
    for g in range(NSA_GROUP):
        sl = slice(g * HEAD_DIM, (g + 1) * HEAD_DIM)
        rs = slice(g * tq, (g + 1) * tq)
        o = (gates[:, 3 * g:3 * g + 1] * ocmp_ref[:, sl] + gates[:, 3 * g + 1:3 * g + 2] * o_sel[rs]
             + gates[:, 3 * g + 2:3 * g + 3] * o_win[rs])
        o_ref[:, sl] = o.astype(o_ref.dtype)


def nsa_prompt_attention(q, kvb, winb, bias5, mask, o_cmp, gates, bsz, seq):
    n = bsz * seq
    tq = PAGE
    nq = seq // tq
    grp = NSA_GROUP * HEAD_DIM
    seq_col = lambda col0: pl.BlockSpec((seq, HEAD_DIM), lambda b, k, i: (b, col0 + k))
    return pl.pallas_call(
        _nsa_prompt_kernel,
        grid=(bsz, NSA_KV_HEADS, nq),
        in_specs=[
            pl.BlockSpec((tq, grp), lambda b, k, i: (b * nq + i, k)),
            seq_col(2 * NSA_KV_HEADS), seq_col(3 * NSA_KV_HEADS),
            seq_col(0), seq_col(NSA_KV_HEADS),
            pl.BlockSpec((NSA_GROUP, bias5.shape[1], tq), lambda b, k, i: (k, 0, 0)),
            pl.BlockSpec((None, None, tq, seq), lambda b, k, i: (b, k, i, 0)),
            pl.BlockSpec((tq, grp), lambda b, k, i: (b * nq + i, k)),
            pl.BlockSpec((tq, LANE), lambda b, k, i: (b * nq + i, k)),
        ],
        out_specs=pl.BlockSpec((tq, grp), lambda b, k, i: (b * nq + i, k)),
        out_shape=jax.ShapeDtypeStruct((n, NSA_HEADS * HEAD_DIM), BF16),
        compiler_params=_cparams("parallel", "parallel", "arbitrary"),
        name="nsa_prompt_attention",
    )(q, kvb, kvb, winb, winb, bias5, mask, o_cmp, gates)


def _nsa_sample_kernel(n_past, pt_ref, q_ref, page_ref, knew_ref, wst_ref, wnew_ref, bsel_ref, bwin_ref,
                       mask_ref, ocmp_ref, gate_ref, o_ref, qbd_ref, m_ref, l_ref, acc_ref):
    p = pl.program_id(1)
    t = q_ref.shape[0]
    rows = NSA_HEADS * t

    def reset():
        m_ref[...] = jnp.full(m_ref.shape, NEG_INF, F32)
        l_ref[...] = jnp.zeros(l_ref.shape, F32)
        acc_ref[...] = jnp.zeros(acc_ref.shape, F32)

    @pl.when(p == 0)
    def _():
        qbd_ref[...] = jnp.zeros(qbd_ref.shape, F32)
        for h in range(NSA_HEADS):
            kvh = h // NSA_GROUP
            qbd_ref[h * t:(h + 1) * t, kvh * HEAD_DIM:(kvh + 1) * HEAD_DIM] = q_ref[:, h * HEAD_DIM:(h + 1) * HEAD_DIM]
        reset()

    def expand_mask(c0):
        pieces = []
        for kvh in range(NSA_KV_HEADS):
            pieces += [mask_ref[kvh, :, pl.ds(c0, PAGE)]] * NSA_GROUP
        return jnp.concatenate(pieces, axis=0)

    def update(kv, bias):
        carry = _softmax_step(qbd_ref[...].astype(BF16), kv[:, :KV_LANES].astype(BF16),
                              kv[:, KV_LANES:].astype(BF16), bias, (m_ref[...], l_ref[...], acc_ref[...]))
        m_ref[...], l_ref[...], acc_ref[...] = carry

    def pad_rows(x):
        return jnp.concatenate([x, jnp.zeros((PAGE - x.shape[0], x.shape[1]), F32)], axis=0)

    c0 = pl.multiple_of(p * PAGE, PAGE)
    update(page_ref[...], bsel_ref[:, pl.ds(c0, PAGE)] + expand_mask(c0))

    @pl.when(p == pl.num_programs(1) - 1)
    def _():
        update(pad_rows(knew_ref[...]), bsel_ref[:, n_past:n_past + PAGE] + expand_mask(n_past))
        o_sel = acc_ref[...] / l_ref[...]
        reset()
        n_st = wst_ref.shape[0]
        for c in range(n_st // PAGE):
            update(wst_ref[c * PAGE:(c + 1) * PAGE, :], bwin_ref[:, c * PAGE:(c + 1) * PAGE])
        update(pad_rows(wnew_ref[...]), bwin_ref[:, n_st:n_st + PAGE])
        o_win = acc_ref[...] / l_ref[...]
        gates = gate_ref[...]
        for h in range(NSA_HEADS):
            kvh, g = divmod(h, NSA_GROUP)
            rs = slice(h * t, (h + 1) * t)
            ls = slice(kvh * HEAD_DIM, (kvh + 1) * HEAD_DIM)
            gc = kvh * LANE + 3 * g
            o_ref[:, h * HEAD_DIM:(h + 1) * HEAD_DIM] = (
                gates[:, gc:gc + 1] * ocmp_ref[:, h * HEAD_DIM:(h + 1) * HEAD_DIM]
                + gates[:, gc + 1:gc + 2] * o_sel[rs, ls] + gates[:, gc + 2:gc + 3] * o_win[rs, ls])


def nsa_sample_attention(q, pages, page_table, kv_new, win_state, win_new, bias_sel, bias_win, mask, o_cmp, gates, seq):
    bsz, n_pages = page_table.shape
    n = bsz * seq
    rows = NSA_HEADS * seq
    width = NSA_HEADS * HEAD_DIM
    n_st = win_state.shape[1]
    cst = lambda a: pl.BlockSpec(a.shape, lambda b, p, pt: (0,) * a.ndim)
    per_seq = lambda w, j: pl.BlockSpec((seq, w), lambda b, p, pt: (b, j))
    return pl.pallas_call(
        functools.partial(_nsa_sample_kernel, n_pages * PAGE),
        grid_spec=pltpu.PrefetchScalarGridSpec(
            num_scalar_prefetch=1,
            grid=(bsz, n_pages),
            in_specs=[
                per_seq(width, 0),
                pl.BlockSpec((None, PAGE, 2 * KV_LANES), lambda b, p, pt: (pt[b, p], 0, 1)),
                per_seq(2 * KV_LANES, 1),
                pl.BlockSpec((None, n_st, 2 * KV_LANES), lambda b, p, pt: (b, 0, 0)),
                per_seq(2 * KV_LANES, 0),
                cst(bias_sel), cst(bias_win),
                pl.BlockSpec((None, NSA_KV_HEADS, seq, mask.shape[3]), lambda b, p, pt: (b, 0, 0, 0)),
                per_seq(width, 0),
                per_seq(NSA_KV_HEADS * LANE, 0),
            ],
            out_specs=per_seq(width, 0),
            scratch_shapes=[pltpu.VMEM((rows, KV_LANES), F32), pltpu.VMEM((rows, 1), F32),
                            pltpu.VMEM((rows, 1), F32), pltpu.VMEM((rows, KV_LANES), F32)],
        ),
        out_shape=jax.ShapeDtypeStruct((n, width), F32),
        compiler_params=_cparams("parallel", "arbitrary"),
        name="nsa_sample_attention",
    )(page_table, q, pages, kv_new, win_state, win_new, bias_sel, bias_win, mask, o_cmp, gates)


HGRN_HEADS = 16
HGRN_WIDTH = HGRN_HEADS * HEAD_DIM
NSA_WIDTH = NSA_HEADS * HEAD_DIM
SCALE = HEAD_DIM ** -0.5


def _layer_weights(i, ffn1_w_gate, ffn1_w_up, ffn1_w_down, w_in, w_out, nsa_q_norm, nsa_k_norm,
                   ffn2_w_gate, ffn2_w_up, ffn2_w_down, ple_w_gate, ple_w_proj):
    c0 = 4 * HGRN_WIDTH
    c1 = c0 + NSA_WIDTH
    c2 = c1 + 4 * KV_LANES
    c3 = c2 + 2 * KV_LANES
    wi = w_in[i]
    wgate = wi[:, c3:].reshape(-1, NSA_KV_HEADS, 3 * NSA_GROUP)
    wgate = jnp.pad(wgate, ((0, 0), (0, 0), (0, LANE - 3 * NSA_GROUP))).reshape(-1, NSA_KV_HEADS * LANE)
    ones = jnp.ones((1, 1, HEAD_DIM), F32)
    return dict(
        ffn1=(ffn1_w_gate[i].astype(BF16), ffn1_w_up[i].astype(BF16), ffn1_w_down[i].astype(BF16)),
        ffn2=(ffn2_w_gate[i].astype(BF16), ffn2_w_up[i].astype(BF16), ffn2_w_down[i].astype(BF16)),
        w_h=wi[:, :c0].astype(BF16), w_q=wi[:, c0:c1].astype(BF16), w_kv=wi[:, c1:c2].astype(BF16),
        w_win=wi[:, c2:c3].astype(BF16), w_gate=wgate.astype(BF16),
        g_q=jnp.broadcast_to(nsa_q_norm[i].reshape(1, 1, HEAD_DIM), (NSA_WIDTH // 512, 1, HEAD_DIM)),
        g_kv=jnp.concatenate([ones, ones, nsa_k_norm[i, 1].reshape(1, 1, HEAD_DIM), ones], axis=0),
        g_win=jnp.concatenate([nsa_k_norm[i, 2].reshape(1, 1, HEAD_DIM), ones], axis=0),
        wo_h=w_out[i, :HGRN_WIDTH].astype(BF16), wo_n=w_out[i, HGRN_WIDTH:].astype(BF16),
        ple_gate=ple_w_gate[i].astype(BF16), ple_proj=ple_w_proj[i].astype(BF16),
    )


def _run_layer(i, w, x, pemb, bsz, seq, s0, nsa_fn, norms, hgrn_lb):
    ffn1_norm, mix_norm, hgrn_out_norm, ffn2_norm, ple_norm, ple_post_norm = norms
    small = seq % 16 != 0
    x1 = ffn_residual(x, ffn1_norm[i], *w["ffn1"])
    (zh,) = norm_project(x1, mix_norm[i], w["w_h"], _epi_plain, (F32,))
    o_h, s_fin = hgrn2_mix(zh, hgrn_lb, hgrn_out_norm[i], i, bsz, seq, HGRN_HEADS, s0=s0)
    qdt = F32 if small else BF16
    (q,) = norm_project(x1, mix_norm[i], w["w_q"], functools.partial(_epi_headnorm, (0, 1, 2, 3), SCALE, (qdt,)),
                        (qdt,), head_gain=w["g_q"])
    kv, kvb = norm_project(x1, mix_norm[i], w["w_kv"], functools.partial(_epi_headnorm, (2,), 1.0, (F32, BF16)),
                           (F32, BF16), head_gain=w["g_kv"])
    win, winb = norm_project(x1, mix_norm[i], w["w_win"], functools.partial(_epi_headnorm, (0,), 1.0, (F32, BF16)),
                             (F32, BF16), head_gain=w["g_win"])
    (gates,) = norm_project(x1, mix_norm[i], w["w_gate"], _epi_sigmoid, (F32,), tn_pref=LANE)
    o_n = nsa_fn(q, kv, kvb, win, winb, gates)
    x2 = out_project_residual(x1, o_h, o_n, w["wo_h"], w["wo_n"])
    x3 = ffn_residual(x2, ffn2_norm[i], *w["ffn2"])
    y = ple_residual(x3, ple_norm[i], w["ple_gate"], pemb, w["ple_proj"], ple_post_norm[i])
    return y, kv, win, s_fin


def kernel(x_prompt, x_sample, cache_kv, state_win_kv, state_hgrn, page_table, p_prompt, p_sample, ffn1_norm, ffn1_w_gate, ffn1_w_up, ffn1_w_down, mix_norm, w_in, w_out, hgrn_lb, hgrn_out_norm, nsa_q_norm, nsa_k_norm, cmp_pos, cmp_w1, cmp_w2, rel_bias_table, ffn2_norm, ffn2_w_gate, ffn2_w_up, ffn2_w_down, ple_norm, ple_w_gate, ple_w_proj, ple_post_norm):
    depth = cache_kv.shape[0]
    bp, tp, d = x_prompt.shape
    bs, ts, _ = x_sample.shape
    n_pool = cache_kv.shape[1]
    n_pages = page_table.shape[1]
    past = n_pages * PAGE
    win_keep = state_win_kv.shape[2]
    assert tp % PAGE == 0 and tp >= WINDOW and win_keep == WINDOW
    norms = (ffn1_norm, mix_norm, hgrn_out_norm, ffn2_norm, ple_norm, ple_post_norm)
    table = rel_bias_table.astype(F32)

    n_cmp = (tp - CMP_BLOCK) // CMP_STRIDE + 1
    bias_pc = rel_bias(table, tp, N_CMP_PAD, 0, CMP_BLOCK - 1, CMP_STRIDE, n_cmp)
    bias_p5 = rel_bias(table, WINDOW + PAGE, PAGE, 0, 0, 1, PAGE, window=WINDOW)
    n_cmp_s = (past + ts - CMP_BLOCK) // CMP_STRIDE + 1
    assert n_cmp_s <= N_CMP_PAD - 1 and (n_cmp_s - 1) * CMP_STRIDE + CMP_BLOCK <= past
    sel_cols = past + PAGE
    bias_sc = rel_bias(table, ts, N_CMP_PAD, past, CMP_BLOCK - 1, CMP_STRIDE, n_cmp_s)
    bias_ss = rel_bias(table, ts, sel_cols, past, 0, 1, past + ts).reshape(NSA_HEADS * ts, sel_cols)
    bias_sw = rel_bias(table, ts, win_keep + PAGE, win_keep, 0, 1, win_keep + ts, window=WINDOW)
    bias_sw = bias_sw.reshape(NSA_HEADS * ts, win_keep + PAGE)

    xp = x_prompt.reshape(bp * tp, d)
    xs = x_sample.reshape(bs * ts, d)
    outs = [[] for _ in range(6)]
    for i in range(depth):
        w = _layer_weights(i, ffn1_w_gate, ffn1_w_up, ffn1_w_down, w_in, w_out, nsa_q_norm, nsa_k_norm,
                           ffn2_w_gate, ffn2_w_up, ffn2_w_down, ple_w_gate, ple_w_proj)
        cmp_args = (cmp_w1[i], cmp_w2[i], cmp_pos[i], nsa_k_norm[i, 0])

        def nsa_prompt(q, kv, kvb, win, winb, gates):
            pages = kv.reshape(bp * tp // PAGE, PAGE, 4 * KV_LANES)
            pt = jnp.arange(bp * tp // PAGE, dtype=jnp.int32).reshape(bp, tp // PAGE)
            kc, vc = compress_cache(pages, pt, 0, *cmp_args)
            o_cmp, mask = cmp_attention_topk(q, kc, vc, bias_pc, bp, tp, 0, tp)
            return nsa_prompt_attention(q, kvb, winb, bias_p5, mask, o_cmp, gates, bp, tp)

        def nsa_sample(q, kv, kvb, win, winb, gates):
            pages = cache_kv[i].reshape(n_pool, PAGE, 4 * KV_LANES)
            kc, vc = compress_cache(pages, page_table, 0, *cmp_args)
            o_cmp, mask = cmp_attention_topk(q, kc, vc, bias_sc, bs, ts, past, past + ts)
            wst = state_win_kv[i].reshape(bs, win_keep, 2 * KV_LANES)
            return nsa_sample_attention(q, pages, page_table, kv, wst, win, bias_ss, bias_sw, mask, o_cmp, gates, ts)

        xp, kv_p, win_p, h_p = _run_layer(i, w, xp, p_prompt[i].reshape(bp * tp, -1), bp, tp, None, nsa_prompt, norms, hgrn_lb)
        xs, kv_s, win_s, h_s = _run_layer(i, w, xs, p_sample[i].reshape(bs * ts, -1), bs, ts, state_hgrn[i], nsa_sample, norms, hgrn_lb)
        outs[0].append(kv_p.reshape(bp, tp, 4, NSA_KV_HEADS, HEAD_DIM))
        outs[1].append(win_p.reshape(bp, tp, 2, NSA_KV_HEADS, HEAD_DIM)[:, -WINDOW:])
        outs[2].append(h_p)
        outs[3].append(kv_s.reshape(bs, ts, 4, NSA_KV_HEADS, HEAD_DIM))
        win_new = win_s.reshape(bs, ts, 2, NSA_KV_HEADS, HEAD_DIM)
        outs[4].append(jnp.concatenate([state_win_kv[i], win_new], axis=1)[:, -win_keep:])
        outs[5].append(h_s.astype(state_hgrn.dtype))
    return (xp.reshape(bp, tp, d), xs.reshape(bs, ts, d)) + tuple(jnp.stack(o) for o in outs)
```

```python
import functools
import math

import numpy as np
import jax
import jax.numpy as jnp
from jax import lax
from jax.experimental import pallas as pl
from jax.experimental.pallas import tpu as pltpu

F32 = jnp.float32
BF16 = jnp.bfloat16

LANE = 128
SUBLANE = 8
VMEM_LIMIT_BYTES = 56 * 1024 * 1024

HEAD_DIM = 128
RMS_EPS = 1e-6
NEG_INF = -1e30


def _cparams(*sem):
    return pltpu.CompilerParams(dimension_semantics=sem, vmem_limit_bytes=VMEM_LIMIT_BYTES)


def _tile(n, pref):
    if n <= pref:
        return n
    t = pref
    while t >= SUBLANE:
        if n % t == 0:
            return t
        t -= SUBLANE
    return n


def _rms(x, gain):
    ms = jnp.mean(x * x, axis=-1, keepdims=True)
    return x * lax.rsqrt(ms + RMS_EPS) * gain


def _norm_rows_to(x_ref, gain_ref, dst_ref):
    rows = x_ref.shape[0]
    rc = 32 if rows % 32 == 0 else rows

    def body(i, carry):
        r = pl.multiple_of(i * rc, rc)
        dst_ref[pl.ds(r, rc), :] = _rms(x_ref[pl.ds(r, rc), :], gain_ref[...]).astype(dst_ref.dtype)
        return carry

    lax.fori_loop(0, rows // rc, body, 0)


def _ffn_kernel(x_ref, gain_ref, wg_ref, wu_ref, wd_ref, o_ref, hn_ref):
    @pl.when(pl.program_id(1) == 0)
    def _():
        _norm_rows_to(x_ref, gain_ref, hn_ref)
        o_ref[...] = x_ref[...]

    h = hn_ref[...]
    g = jnp.dot(h, wg_ref[...], preferred_element_type=F32)
    u = jnp.dot(h, wu_ref[...], preferred_element_type=F32)
    a = (g * jax.nn.sigmoid(g) * (0.5 * u)).astype(BF16)
    o_ref[...] += jnp.dot(a, wd_ref[...], preferred_element_type=F32)


def ffn_residual(x, gain, wg, wu, wd, tm_pref=512, tf_pref=256):
    n, d = x.shape
    f = wg.shape[1]
    tm, tf = _tile(n, tm_pref), _tile(f, tf_pref)
    return pl.pallas_call(
        _ffn_kernel,
        grid=(n // tm, f // tf),
        in_specs=[
            pl.BlockSpec((tm, d), lambda i, j: (i, 0)),
            pl.BlockSpec((1, d), lambda i, j: (0, 0)),
            pl.BlockSpec((d, tf), lambda i, j: (0, j)),
            pl.BlockSpec((d, tf), lambda i, j: (0, j)),
            pl.BlockSpec((tf, d), lambda i, j: (j, 0)),
        ],
        out_specs=pl.BlockSpec((tm, d), lambda i, j: (i, 0)),
        out_shape=jax.ShapeDtypeStruct((n, d), F32),
        scratch_shapes=[pltpu.VMEM((tm, d), BF16)],
        compiler_params=_cparams("parallel", "arbitrary"),
        name="ffn_residual",
    )(x, gain.reshape(1, d), wg, wu, wd)


def _proj_kernel(epilogue, n_out, x_ref, gain_ref, w_ref, hg_ref, *rest):
    outs, hn_ref = rest[:n_out], rest[n_out]
    j = pl.program_id(1)

    @pl.when(j == 0)
    def _():
        _norm_rows_to(x_ref, gain_ref, hn_ref)

    acc = jnp.dot(hn_ref[...], w_ref[...], preferred_element_type=F32)
    epilogue(j, acc, hg_ref, outs)


def _epi_plain(j, acc, hg_ref, outs):
    outs[0][...] = acc


def _epi_sigmoid(j, acc, hg_ref, outs):
    outs[0][...] = jax.nn.sigmoid(acc)


def _epi_headnorm(normed_tiles, scale, out_dtypes, j, acc, hg_ref, outs):
    flag = functools.reduce(jnp.logical_or, [j == t for t in normed_tiles])
    gain = hg_ref[0]
    for h in range(acc.shape[1] // HEAD_DIM):
        sl = acc[:, h * HEAD_DIM:(h + 1) * HEAD_DIM]
        val = jnp.where(flag, _rms(sl, gain), sl)
        if scale != 1.0:
            val = val * scale
        for o_ref, dt in zip(outs, out_dtypes):
            o_ref[:, h * HEAD_DIM:(h + 1) * HEAD_DIM] = val.astype(dt)


def norm_project(x, gain, w, epilogue, out_dtypes, head_gain=None, tm_pref=512, tn_pref=512):
    n, d = x.shape
    ncols = w.shape[1]
    tm, tn = _tile(n, tm_pref), _tile(ncols, tn_pref)
    if head_gain is None:
        head_gain = jnp.ones((ncols // tn, 1, HEAD_DIM), F32)
    n_out = len(out_dtypes)
    out = pl.pallas_call(
        functools.partial(_proj_kernel, epilogue, n_out),
        grid=(n // tm, ncols // tn),
        in_specs=[
            pl.BlockSpec((tm, d), lambda i, j: (i, 0)),
            pl.BlockSpec((1, d), lambda i, j: (0, 0)),
            pl.BlockSpec((d, tn), lambda i, j: (0, j)),
            pl.BlockSpec((1, 1, HEAD_DIM), lambda i, j: (j, 0, 0)),
        ],
        out_specs=[pl.BlockSpec((tm, tn), lambda i, j: (i, j)) for _ in out_dtypes],
        out_shape=[jax.ShapeDtypeStruct((n, ncols), dt) for dt in out_dtypes],
        scratch_shapes=[pltpu.VMEM((tm, d), BF16)],
        compiler_params=_cparams("parallel", "arbitrary"),
        name="norm_project",
    )(x, gain.reshape(1, d), w, head_gain)
    return out


def _outproj_kernel(x_ref, a_ref, b_ref, wa_ref, wb_ref, o_ref):
    acc = jnp.dot(a_ref[...].astype(BF16), wa_ref[...], preferred_element_type=F32)
    acc += jnp.dot(b_ref[...].astype(BF16), wb_ref[...], preferred_element_type=F32)
    o_ref[...] = x_ref[...] + acc


def out_project_residual(x, a, b, wa, wb, tm_pref=512, tn_pref=1024):
    n, d = x.shape
    ka, kb = a.shape[1], b.shape[1]
    tm, tn = _tile(n, tm_pref), _tile(d, tn_pref)
    return pl.pallas_call(
        _outproj_kernel,
        grid=(n // tm, d // tn),
        in_specs=[
            pl.BlockSpec((tm, tn), lambda i, j: (i, j)),
            pl.BlockSpec((tm, ka), lambda i, j: (i, 0)),
            pl.BlockSpec((tm, kb), lambda i, j: (i, 0)),
            pl.BlockSpec((ka, tn), lambda i, j: (0, j)),
            pl.BlockSpec((kb, tn), lambda i, j: (0, j)),
        ],
        out_specs=pl.BlockSpec((tm, tn), lambda i, j: (i, j)),
        out_shape=jax.ShapeDtypeStruct((n, d), F32),
        compiler_params=_cparams("parallel", "arbitrary"),
        name="out_project_residual",
    )(x, a, b, wa, wb)


def _ple_kernel(tn, x_ref, gain_ref, wg_ref, p_ref, wp_ref, pg_ref, o_ref, hn_ref, pe_ref):
    j = pl.program_id(1)

    @pl.when(j == 0)
    def _():
        _norm_rows_to(x_ref, gain_ref, hn_ref)
        pe_ref[...] = jnp.dot(p_ref[...].astype(BF16), wp_ref[...], preferred_element_type=F32)
        _norm_rows_to(pe_ref, pg_ref, pe_ref)

    c = pl.multiple_of(j * tn, LANE)
    gate = jax.nn.sigmoid(jnp.dot(hn_ref[...], wg_ref[...], preferred_element_type=F32))
    o_ref[...] = x_ref[:, pl.ds(c, tn)] + gate * pe_ref[:, pl.ds(c, tn)]


def ple_residual(x, gain, wg, p, wp, post_gain, tm_pref=512, tn_pref=512):
    n, d = x.shape
    pd = p.shape[1]
    tm, tn = _tile(n, tm_pref), _tile(d, tn_pref)
    return pl.pallas_call(
        functools.partial(_ple_kernel, tn),
        grid=(n // tm, d // tn),
        in_specs=[
            pl.BlockSpec((tm, d), lambda i, j: (i, 0)),
            pl.BlockSpec((1, d), lambda i, j: (0, 0)),
            pl.BlockSpec((d, tn), lambda i, j: (0, j)),
            pl.BlockSpec((tm, pd), lambda i, j: (i, 0)),
            pl.BlockSpec((pd, d), lambda i, j: (0, 0)),
            pl.BlockSpec((1, d), lambda i, j: (0, 0)),
        ],
        out_specs=pl.BlockSpec((tm, tn), lambda i, j: (i, j)),
        out_shape=jax.ShapeDtypeStruct((n, d), F32),
        scratch_shapes=[pltpu.VMEM((tm, d), BF16), pltpu.VMEM((tm, d), F32)],
        compiler_params=_cparams("parallel", "arbitrary"),
        name="ple_residual",
    )(x, gain.reshape(1, d), wg, p, wp, post_gain.reshape(1, d))


HGRN_DIAG = SUBLANE


def _split3_dot(lhs_bf16, x):
    hi = x.astype(BF16)
    r1 = x - hi.astype(F32)
    mid = r1.astype(BF16)
    lo = (r1 - mid.astype(F32)).astype(BF16)
    acc = jnp.dot(lhs_bf16, hi, preferred_element_type=F32)
    acc += jnp.dot(lhs_bf16, mid, preferred_element_type=F32)
    acc += jnp.dot(lhs_bf16, lo, preferred_element_type=F32)
    return acc


def _bcast_row_in_blocks(a, bs, row):
    c, k = a.shape
    a3 = a.reshape(c // bs, bs, k)
    return jnp.broadcast_to(a3[:, row:row + 1, :], (c // bs, bs, k)).reshape(c, k)


def _nt_dot(a, b):
    return lax.dot_general(a, b, (((1,), (1,)), ((), ())), preferred_element_type=F32)


def _hgrn_kernel(layer, has_s0, hps, q_ref, f_ref, i_ref, g_ref, lb_ref, og_ref, *rest):
    if has_s0:
        s0_ref, o_ref, sfin_ref, st_ref = rest
    else:
        s0_ref = None
        o_ref, sfin_ref, st_ref = rest
    for hh in range(hps):
        ls = slice(hh * HEAD_DIM, (hh + 1) * HEAD_DIM)
        _hgrn_head(layer, q_ref.at[:, ls], f_ref.at[:, ls], i_ref.at[:, ls], g_ref.at[:, ls], lb_ref.at[:, ls],
                   og_ref, None if s0_ref is None else s0_ref.at[hh], o_ref.at[:, ls], sfin_ref.at[hh], st_ref.at[hh])


def _hgrn_head(layer, q_ref, f_ref, i_ref, g_ref, lb_ref, og_ref, s0_ref, o_ref, sfin_ref, st_ref):
    ci = pl.program_id(2)
    c = q_ref.shape[0]

    @pl.when(ci == 0)
    def _():
        if s0_ref is not None:
            st_ref[...] = s0_ref[...].T
        else:
            st_ref[...] = jnp.zeros(st_ref.shape, F32)

    lbr = lb_ref[...]
    e = jnp.exp(lbr - jnp.max(lbr, axis=0, keepdims=True))
    lb = jnp.sum(e[:layer + 1], axis=0, keepdims=True) / jnp.sum(e, axis=0, keepdims=True)

    q = q_ref[...]
    fg = lb + (1.0 - lb) * jax.nn.sigmoid(f_ref[...])
    logf = jnp.log(fg)
    kk = 1.0 - fg
    v = i_ref[...]

    ti = lax.broadcasted_iota(jnp.int32, (c, c), 0)
    si = lax.broadcasted_iota(jnp.int32, (c, c), 1)
    tril = (si <= ti)
    a = _split3_dot(tril.astype(BF16), logf)
    a_last = a[c - 1:c, :]

    kk_b = kk.astype(BF16)
    d = HGRN_DIAG
    xs = [(q * jnp.exp(jnp.minimum(a - _bcast_row_in_blocks(a, d, j), 0.0))).astype(BF16) for j in range(d)]
    res = _nt_dot(jnp.concatenate(xs, axis=0), kk_b)
    attn = jnp.zeros((c, c), F32)
    for j in range(d):
        attn += jnp.where((si % d) == j, res[j * c:(j + 1) * c], 0.0)
    attn = jnp.where(((si // d) == (ti // d)) & tril, attn, 0.0)
    bs = 2 * d
    while bs <= c:
        half = bs // 2
        bnd = _bcast_row_in_blocks(a, bs, half - 1)
        qe = (q * jnp.exp(jnp.minimum(a - bnd, 0.0))).astype(BF16)
        ke = (kk * jnp.exp(jnp.minimum(bnd - a, 0.0))).astype(BF16)
        m = ((si // bs) == (ti // bs)) & ((ti % bs) >= half) & ((si % bs) < half)
        attn += jnp.where(m, _nt_dot(qe, ke), 0.0)
        bs *= 2

    st = st_ref[...]
    v_b = v.astype(BF16)
    o = jnp.dot(attn.astype(BF16), v_b, preferred_element_type=F32)
    o += _nt_dot((q * jnp.exp(a)).astype(BF16), st.astype(BF16))
    kd = (kk * jnp.exp(a_last - a)).astype(BF16)
    st_new = st * jnp.exp(a_last) + jnp.dot(v_b.T, kd, preferred_element_type=F32)
    st_ref[...] = st_new

    gr = g_ref[...]
    o_ref[...] = (_rms(o, og_ref[...]) * (gr * jax.nn.sigmoid(gr))).astype(o_ref.dtype)

    @pl.when(ci == pl.num_programs(2) - 1)
    def _():
        sfin_ref[...] = st_new.T


def hgrn2_mix(zh, hgrn_lb, out_gain, layer, batch, seq, heads, s0=None, chunk_pref=128, heads_per_step=4):
    n = batch * seq
    dk = HEAD_DIM
    c = _tile(seq, chunk_pref)
    assert c % HGRN_DIAG == 0 and (c // HGRN_DIAG) & (c // HGRN_DIAG - 1) == 0
    nc = seq // c
    nl = hgrn_lb.shape[0]
    hps = heads_per_step
    hg = heads // hps

    def zspec(sec):
        return pl.BlockSpec((c, hps * dk), lambda b, h, ci: (b * nc + ci, sec * hg + h))

    state_spec = pl.BlockSpec((None, hps, dk, dk), lambda b, h, ci: (b, h, 0, 0))
    in_specs = [zspec(0), zspec(1), zspec(2), zspec(3),
                pl.BlockSpec((nl, hps * dk), lambda b, h, ci: (0, h)),
                pl.BlockSpec((1, dk), lambda b, h, ci: (0, 0))]
    args = [zh, zh, zh, zh, hgrn_lb, out_gain.reshape(1, dk)]
    if s0 is not None:
        in_specs.append(state_spec)
        args.append(s0)
    return pl.pallas_call(
        functools.partial(_hgrn_kernel, layer, s0 is not None, hps),
        grid=(batch, hg, nc),
        in_specs=in_specs,
        out_specs=[pl.BlockSpec((c, hps * dk), lambda b, h, ci: (b * nc + ci, h)), state_spec],
        out_shape=[jax.ShapeDtypeStruct((n, heads * dk), BF16 if c % 16 == 0 else F32),
                   jax.ShapeDtypeStruct((batch, heads, dk, dk), F32)],
        scratch_shapes=[pltpu.VMEM((hps, dk, dk), F32)],
        compiler_params=_cparams("parallel", "parallel", "arbitrary"),
        name="hgrn2_mix",
    )(*args)


NSA_KV_HEADS = 4
NSA_GROUP = 4
NSA_HEADS = NSA_KV_HEADS * NSA_GROUP
KV_LANES = NSA_KV_HEADS * HEAD_DIM
PAGE = 128
CMP_BLOCK = 32
CMP_STRIDE = 16
CMP_PER_PAGE = PAGE // CMP_STRIDE
N_CMP_PAD = 128
SEL_BLOCK = 64
SEL_TOPK = 16
SEL_LOCAL = 2
SEL_FORCE = 1e3
SEL_INVALID = -1e9
WINDOW = 512
N_BUCKETS = 32
MAX_DISTANCE = 128
MASKED = -1e29


def _bias_kernel(q0, k0, kstride, ncols_valid, window, table_ref, o_ref):
    h = pl.program_id(0)
    rt, cols = o_ref.shape
    r = lax.broadcasted_iota(jnp.int32, (rt, cols), 0) + pl.program_id(1) * rt
    c = lax.broadcasted_iota(jnp.int32, (rt, cols), 1)
    dist = (q0 + r) - (k0 + c * kstride)
    dpos = jnp.maximum(dist, 0)
    max_exact = N_BUCKETS // 2
    log_ratio = jnp.log(jnp.maximum(dpos, 1).astype(F32) / max_exact) / math.log(MAX_DISTANCE / max_exact)
    large = jnp.minimum(max_exact + (log_ratio * (N_BUCKETS - max_exact)).astype(jnp.int32), N_BUCKETS - 1)
    bucket = jnp.where(dpos < max_exact, dpos, large)
    acc = jnp.zeros((rt, cols), F32)
    for b in range(N_BUCKETS):
        acc = jnp.where(bucket == b, table_ref[b, h], acc)
    valid = (dist >= 0) & (c < ncols_valid)
    if window is not None:
        valid = valid & (dist < window)
    o_ref[...] = jnp.where(valid, acc, NEG_INF)


def rel_bias(table, rows, cols, q0, k0, kstride, ncols_valid, window=None):
    rt = _tile(rows, 256)
    return pl.pallas_call(
        functools.partial(_bias_kernel, q0, k0, kstride, ncols_valid, window),
        grid=(NSA_HEADS, rows // rt),
        in_specs=[pl.BlockSpec(memory_space=pltpu.SMEM)],
        out_specs=pl.BlockSpec((None, rt, cols), lambda h, i: (h, i, 0)),
        out_shape=jax.ShapeDtypeStruct((NSA_HEADS, rows, cols), F32),
        compiler_params=_cparams("parallel", "parallel"),
        name="rel_bias",
    )(table)


def _compress_kernel(pt_ref, *refs):
    n_in = 2 * NSA_KV_HEADS
    page_refs = refs[:n_in]
    wck_ref, wcv_ref, posk_ref, posv_ref, w2k_ref, w2v_ref, kn_ref, kc_ref, vc_ref, xs_ref = refs[n_in:]
    p = pl.program_id(1)
    for slot in range(2):
        for kvh in range(NSA_KV_HEADS):
            page_ref = page_refs[slot * NSA_KV_HEADS + kvh]
            row0 = pl.multiple_of(kvh * N_CMP_PAD + p * CMP_PER_PAGE, CMP_PER_PAGE)
            for r in range(CMP_STRIDE):
                xs_ref[slot, pl.ds(row0, CMP_PER_PAGE), r * HEAD_DIM:(r + 1) * HEAD_DIM] = (
                    page_ref[pl.ds(r, CMP_PER_PAGE, stride=CMP_STRIDE), :])

    @pl.when(p == pl.num_programs(1) - 1)
    def _():
        kc = _compress_finish(xs_ref[0].astype(BF16), wck_ref[...], posk_ref[...], w2k_ref[...], kn_ref[...])
        vc = _compress_finish(xs_ref[1].astype(BF16), wcv_ref[...], posv_ref[...], w2v_ref[...], None)
        kc_ref[...] = kc.reshape(NSA_KV_HEADS, N_CMP_PAD, HEAD_DIM).astype(kc_ref.dtype)
        vc_ref[...] = vc.reshape(NSA_KV_HEADS, N_CMP_PAD, HEAD_DIM).astype(vc_ref.dtype)


def _compress_finish(x, wcat, posb, w2, k_gain):
    m_rows = x.shape[0]
    hid = w2.shape[0]
    pq = jnp.dot(x, wcat, preferred_element_type=F32)
    nxt = pltpu.roll(pq[:, hid:], m_rows - 1, 0)
    hcur = pq[:, :hid] + nxt + posb
    act = (hcur * jax.nn.sigmoid(hcur)).astype(BF16)
    out = jnp.dot(act, w2, preferred_element_type=F32)
    if k_gain is not None:
        out = _rms(out, k_gain)
    rowid = lax.broadcasted_iota(jnp.int32, out.shape, 0) % N_CMP_PAD
    return jnp.where(rowid == N_CMP_PAD - 1, 0.0, out)


def _posb_kernel(pos_ref, w1_ref, o_ref):
    o_ref[...] = jnp.dot(pos_ref[...].astype(BF16), w1_ref[...], preferred_element_type=F32)


def compress_weights(w1, w2, pos, k_norm_cmp):
    hid = w2.shape[1]
    out = []
    for s in range(2):
        wcat = w1[s].reshape(2, CMP_STRIDE, HEAD_DIM, hid).transpose(1, 2, 0, 3).reshape(CMP_STRIDE * HEAD_DIM, 2 * hid)
        posf = jnp.broadcast_to(pos[s].reshape(1, CMP_BLOCK * HEAD_DIM), (SUBLANE, CMP_BLOCK * HEAD_DIM))
        posb = pl.pallas_call(_posb_kernel, out_shape=jax.ShapeDtypeStruct((SUBLANE, hid), F32),
                              name="cmp_pos_bias")(posf, w1[s].astype(BF16))[0:1]
        out.append((wcat.astype(BF16), posb, w2[s].astype(BF16)))
    (wck, posk, w2k), (wcv, posv, w2v) = out
    return [wck, wcv, posk, posv, w2k, w2v, k_norm_cmp.reshape(1, HEAD_DIM)]


def compress_cache(pages, page_table, lane_block, cmp_w):
    bsz, n_pages = page_table.shape
    assert n_pages * CMP_PER_PAGE == N_CMP_PAD
    full = lambda a: pl.BlockSpec(a.shape, lambda b, p, pt: (0,) * a.ndim)
    n_in = 2 * NSA_KV_HEADS

    def page_map(lane_blk, b, p, pt):
        return (pt[b, p], 0, lane_blk)

    ins = cmp_w
    out_spec = pl.BlockSpec((None, NSA_KV_HEADS, N_CMP_PAD, HEAD_DIM), lambda b, p, pt: (b, 0, 0, 0))
    out_sds = jax.ShapeDtypeStruct((bsz, NSA_KV_HEADS, N_CMP_PAD, HEAD_DIM), BF16)
    return pl.pallas_call(
        _compress_kernel,
        grid_spec=pltpu.PrefetchScalarGridSpec(
            num_scalar_prefetch=1,
            grid=(bsz, n_pages),
            in_specs=[pl.BlockSpec((None, PAGE, HEAD_DIM), functools.partial(page_map, lane_block * n_in + j))
                      for j in range(n_in)] + [full(a) for a in ins],
            out_specs=[out_spec, out_spec],
            scratch_shapes=[pltpu.VMEM((2, NSA_KV_HEADS * N_CMP_PAD, CMP_STRIDE * HEAD_DIM), F32)],
        ),
        out_shape=[out_sds, out_sds],
        compiler_params=_cparams("parallel", "arbitrary"),
        name="compress_cache",
    )(page_table, *([pages] * n_in), *ins)


def _heads_to_rows(q_ref):
    return jnp.concatenate([q_ref[:, g * HEAD_DIM:(g + 1) * HEAD_DIM] for g in range(NSA_GROUP)], axis=0).astype(BF16)


def _dot3_lhs(x, rhs_bf16):
    hi = x.astype(BF16)
    r1 = x - hi.astype(F32)
    mid = r1.astype(BF16)
    lo = (r1 - mid.astype(F32)).astype(BF16)
    acc = jnp.dot(hi, rhs_bf16, preferred_element_type=F32)
    acc += jnp.dot(mid, rhs_bf16, preferred_element_type=F32)
    acc += jnp.dot(lo, rhs_bf16, preferred_element_type=F32)
    return acc


def _selected_key_mask(psum, overlap, expand, qpos, n_sel):
    r = psum.shape[0]
    imp = _dot3_lhs(psum, overlap)
    lane = lax.broadcasted_iota(jnp.int32, (r, LANE), 1)
    lag = qpos // SEL_BLOCK - lane
    forced = (lane == 0) | ((lag >= 0) & (lag < SEL_LOCAL))
    score = jnp.where(lag >= 0, imp + jnp.where(forced, SEL_FORCE, 0.0), SEL_INVALID)
    cnt = jnp.zeros((r, LANE), jnp.int32)
    for j in range(n_sel):
        col = score[:, j:j + 1]
        cnt += jnp.where(col > score, 1, jnp.where(col == score, jnp.where(lane > j, 1, 0), 0))
    sel = jnp.where(cnt < min(SEL_TOPK, n_sel), jnp.where(lane < n_sel, 1.0, 0.0), 0.0)
    lp = expand.shape[1]
    ex = jnp.dot(sel.astype(BF16), expand, preferred_element_type=F32)
    kpos = lax.broadcasted_iota(jnp.int32, (r, lp), 1)
    keep = jnp.where(kpos <= qpos, ex, 0.0) > 0.5
    return jnp.where(keep, 0.0, NEG_INF)


def _cmp_topk_kernel(q0, n_sel, q_ref, kc_ref, vc_ref, bias_ref, ovl_ref, exp_ref, ocmp_ref, mask_ref):
    qi = pl.program_id(2)
    tq = q_ref.shape[0]
    q4 = _heads_to_rows(q_ref)
    bias = bias_ref[...].reshape(NSA_GROUP * tq, N_CMP_PAD)
    s = _nt_dot(q4, kc_ref[...]) + bias
    m = jnp.max(s, axis=-1, keepdims=True)
    e = jnp.where(bias > MASKED, jnp.exp(s - m), 0.0)
    p = e / jnp.maximum(jnp.sum(e, axis=-1, keepdims=True), 1e-30)
    o = jnp.dot(p.astype(BF16), vc_ref[...], preferred_element_type=F32)
    psum = jnp.zeros((tq, N_CMP_PAD), F32)
    for g in range(NSA_GROUP):
        ocmp_ref[:, g * HEAD_DIM:(g + 1) * HEAD_DIM] = o[g * tq:(g + 1) * tq]
        psum += p[g * tq:(g + 1) * tq]
    qpos = q0 + qi * tq + lax.broadcasted_iota(jnp.int32, (tq, 1), 0)
    mask_ref[...] = _selected_key_mask(psum, ovl_ref[...], exp_ref[...], qpos, n_sel).astype(mask_ref.dtype)


def cmp_attention_topk(q, kc, vc, bias_c, bsz, seq, q0, key_len, tq_pref=256):
    n = bsz * seq
    tq = _tile(seq, tq_pref)
    nq = seq // tq
    n_sel = -(-key_len // SEL_BLOCK)
    lp = -(-(n_sel * SEL_BLOCK) // LANE) * LANE
    ci = np.arange(N_CMP_PAD)[:, None] * CMP_STRIDE
    sj = np.arange(LANE)[None, :] * SEL_BLOCK
    overlap = jnp.asarray(((ci < sj + SEL_BLOCK) & (ci + CMP_BLOCK > sj)).astype(np.float32), BF16)
    expand = jnp.asarray((np.arange(lp)[None, :] // SEL_BLOCK == np.arange(LANE)[:, None]).astype(np.float32), BF16)
    mask_dtype = BF16 if tq % 16 == 0 else F32
    return pl.pallas_call(
        functools.partial(_cmp_topk_kernel, q0, n_sel),
        grid=(bsz, NSA_KV_HEADS, nq),
        in_specs=[
            pl.BlockSpec((tq, NSA_GROUP * HEAD_DIM), lambda b, k, i: (b * nq + i, k)),
            pl.BlockSpec((None, None, N_CMP_PAD, HEAD_DIM), lambda b, k, i: (b, k, 0, 0)),
            pl.BlockSpec((None, None, N_CMP_PAD, HEAD_DIM), lambda b, k, i: (b, k, 0, 0)),
            pl.BlockSpec((NSA_GROUP, tq, N_CMP_PAD), lambda b, k, i: (k, i, 0)),
            pl.BlockSpec((N_CMP_PAD, LANE), lambda b, k, i: (0, 0)),
            pl.BlockSpec((LANE, lp), lambda b, k, i: (0, 0)),
        ],
        out_specs=[
            pl.BlockSpec((tq, NSA_GROUP * HEAD_DIM), lambda b, k, i: (b * nq + i, k)),
            pl.BlockSpec((None, None, tq, lp), lambda b, k, i: (b, k, i, 0)),
        ],
        out_shape=[jax.ShapeDtypeStruct((n, NSA_HEADS * HEAD_DIM), F32),
                   jax.ShapeDtypeStruct((bsz, NSA_KV_HEADS, seq, lp), mask_dtype)],
        compiler_params=_cparams("parallel", "parallel", "parallel"),
        name="cmp_attention_topk",
    )(q, kc, vc, bias_c, overlap, expand)


def _softmax_step(q4, k, v, bias, carry):
    m, l, acc = carry
    s = _nt_dot(q4, k) + bias
    m_new = jnp.maximum(m, jnp.max(s, axis=-1, keepdims=True))
    alpha = jnp.exp(m - m_new)
    pr = jnp.where(bias > MASKED, jnp.exp(s - m_new), 0.0)
    l = alpha * l + jnp.sum(pr, axis=-1, keepdims=True)
    acc = alpha * acc + jnp.dot(pr.astype(BF16), v, preferred_element_type=F32)
    return m_new, l, acc


def _nsa_prompt_kernel(q_ref, ks_ref, vs_ref, kw_ref, vw_ref, bias_ref, mask_ref, ocmp_ref, gate_ref, o_ref):
    qi = pl.program_id(2)
    tq = q_ref.shape[0]
    ck = 2 * tq
    rows = NSA_GROUP * tq
    q4 = _heads_to_rows(q_ref)
    n_win = WINDOW // tq

    def bias_pair(kc, far):
        tiles = []
        for half in range(2):
            delta = qi - 2 * kc - half
            if far is not None:
                delta = jnp.minimum(delta, far)
            off = pl.multiple_of((delta + 1) * tq, tq)
            tiles.append(bias_ref[:, pl.ds(off, tq), :].reshape(rows, tq))
        return jnp.concatenate(tiles, axis=1)

    def kv_chunk(k_ref, v_ref, kc):
        off = pl.multiple_of(kc * ck, ck)
        return k_ref[pl.ds(off, ck), :], v_ref[pl.ds(off, ck), :]

    init = (jnp.full((rows, 1), NEG_INF, F32), jnp.zeros((rows, 1), F32), jnp.zeros((rows, HEAD_DIM), F32))

    def sel_body(kc, carry):
        off = pl.multiple_of(kc * ck, ck)
        mb = mask_ref[:, pl.ds(off, ck)].astype(F32)
        bias = bias_pair(kc, 2) + jnp.concatenate([mb] * NSA_GROUP, axis=0)
        k, v = kv_chunk(ks_ref, vs_ref, kc)
        return _softmax_step(q4, k, v, bias, carry)

    _, l, acc = lax.fori_loop(0, qi // 2 + 1, sel_body, init)
    o_sel = acc / l

    def win_body(kc, carry):
        k, v = kv_chunk(kw_ref, vw_ref, kc)
        return _softmax_step(q4, k, v, bias_pair(kc, None), carry)

    _, l, acc = lax.fori_loop(jnp.maximum(qi - n_win, 0) // 2, qi // 2 + 1, win_body, init)
    o_win = acc / l

    gates = gate_ref[...]
    for g in range(NSA_GROUP):
        sl = slice(g * HEAD_DIM, (g + 1) * HEAD_DIM)
        rs = slice(g * tq, (g + 1) * tq)
        o = (gates[:, 3 * g:3 * g + 1] * ocmp_ref[:, sl] + gates[:, 3 * g + 1:3 * g + 2] * o_sel[rs]
             + gates[:, 3 * g + 2:3 * g + 3] * o_win[rs])
        o_ref[:, sl] = o.astype(o_ref.dtype)


def nsa_prompt_attention(q, kvb, winb, bias5, mask, o_cmp, gates, bsz, seq):
    n = bsz * seq
    tq = PAGE
    nq = seq // tq
    grp = NSA_GROUP * HEAD_DIM
    seq_col = lambda col0: pl.BlockSpec((seq, HEAD_DIM), lambda b, k, i: (b, col0 + k))
    return pl.pallas_call(
        _nsa_prompt_kernel,
        grid=(bsz, NSA_KV_HEADS, nq),
        in_specs=[
            pl.BlockSpec((tq, grp), lambda b, k, i: (b * nq + i, k)),
            seq_col(2 * NSA_KV_HEADS), seq_col(3 * NSA_KV_HEADS),
            seq_col(0), seq_col(NSA_KV_HEADS),
            pl.BlockSpec((NSA_GROUP, bias5.shape[1], tq), lambda b, k, i: (k, 0, 0)),
            pl.BlockSpec((None, None, tq, seq), lambda b, k, i: (b, k, i, 0)),
            pl.BlockSpec((tq, grp), lambda b, k, i: (b * nq + i, k)),
            pl.BlockSpec((tq, LANE), lambda b, k, i: (b * nq + i, k)),
        ],
        out_specs=pl.BlockSpec((tq, grp), lambda b, k, i: (b * nq + i, k)),
        out_shape=jax.ShapeDtypeStruct((n, NSA_HEADS * HEAD_DIM), BF16),
        compiler_params=_cparams("parallel", "parallel", "arbitrary"),
        name="nsa_prompt_attention",
    )(q, kvb, kvb, winb, winb, bias5, mask, o_cmp, gates)


ROWS_PER_TOKEN = 4 * NSA_KV_HEADS


def _softmax_attend(q, k, v, bias):
    s = _nt_dot(q, k) + bias
    e = jnp.where(bias > MASKED, jnp.exp(s - jnp.max(s, axis=-1, keepdims=True)), 0.0)
    o = jnp.dot(e.astype(BF16), v, preferred_element_type=F32)
    return o / jnp.maximum(jnp.sum(e, axis=-1, keepdims=True), 1e-30), e


def _nsa_sample_kernel(n_past, n_sel, pt_ref, q_ref, page_ref, knew_ref, wst_ref, wnew_ref, bcmp_ref, bsel_ref,
                       bwin_ref, wck_ref, wcv_ref, posk_ref, posv_ref, w2k_ref, w2v_ref, kn_ref, ovl_ref, exp_ref,
                       gate_ref, o_ref, xs_ref, ks_ref, vs_ref):
    p = pl.program_id(1)
    t = q_ref.shape[0]
    rpt = ROWS_PER_TOKEN
    for slot in range(2):
        for kvh in range(NSA_KV_HEADS):
            row0 = pl.multiple_of(kvh * N_CMP_PAD + p * CMP_PER_PAGE, CMP_PER_PAGE)
            for r in range(CMP_STRIDE):
                xs_ref[slot, pl.ds(row0, CMP_PER_PAGE), r * HEAD_DIM:(r + 1) * HEAD_DIM] = (
                    page_ref[pl.ds(r * rpt + slot * NSA_KV_HEADS + kvh, CMP_PER_PAGE, stride=CMP_STRIDE * rpt), :])
    k0 = pl.multiple_of(p * PAGE, PAGE)
    for kvh in range(NSA_KV_HEADS):
        ls = slice(kvh * HEAD_DIM, (kvh + 1) * HEAD_DIM)
        ks_ref[pl.ds(k0, PAGE), ls] = page_ref[pl.ds(2 * NSA_KV_HEADS + kvh, PAGE, stride=rpt), :].astype(BF16)
        vs_ref[pl.ds(k0, PAGE), ls] = page_ref[pl.ds(3 * NSA_KV_HEADS + kvh, PAGE, stride=rpt), :].astype(BF16)

    @pl.when(p == pl.num_programs(1) - 1)
    def _():
        def pad_rows(x):
            return jnp.concatenate([x, jnp.zeros((PAGE - x.shape[0], x.shape[1]), F32)], axis=0)

        def lanes_by_kvh(x):
            return jnp.concatenate([x[k * N_CMP_PAD:(k + 1) * N_CMP_PAD] for k in range(NSA_KV_HEADS)], axis=1)

        knew = knew_ref[...]
        ks_ref[n_past:n_past + PAGE, :] = pad_rows(knew[:, :KV_LANES]).astype(BF16)
        vs_ref[n_past:n_past + PAGE, :] = pad_rows(knew[:, KV_LANES:]).astype(BF16)

        q = q_ref[...]
        zero = jnp.zeros((t, HEAD_DIM), F32)
        qbd = jnp.concatenate(
            [jnp.concatenate([q[:, h * HEAD_DIM:(h + 1) * HEAD_DIM] if k == h // NSA_GROUP else zero
                              for k in range(NSA_KV_HEADS)], axis=1) for h in range(NSA_HEADS)], axis=0).astype(BF16)

        kc = _compress_finish(xs_ref[0].astype(BF16), wck_ref[...], posk_ref[...], w2k_ref[...], kn_ref[...])
        vc = _compress_finish(xs_ref[1].astype(BF16), wcv_ref[...], posv_ref[...], w2v_ref[...], None)
        o_cmp, e_c = _softmax_attend(qbd, lanes_by_kvh(kc).astype(BF16), lanes_by_kvh(vc).astype(BF16), bcmp_ref[...])
        p_c = e_c / jnp.maximum(jnp.sum(e_c, axis=-1, keepdims=True), 1e-30)
        psum = jnp.concatenate(
            [sum(p_c[(k * NSA_GROUP + g) * t:(k * NSA_GROUP + g + 1) * t] for g in range(NSA_GROUP))
             for k in range(NSA_KV_HEADS)], axis=0)
        qpos = n_past + lax.broadcasted_iota(jnp.int32, (NSA_KV_HEADS * t, 1), 0) % t
        mb = _selected_key_mask(psum, ovl_ref[...], exp_ref[...], qpos, n_sel)
        mb = jnp.concatenate([mb[k * t:(k + 1) * t] for k in range(NSA_KV_HEADS) for _ in range(NSA_GROUP)], axis=0)

        o_sel, _ = _softmax_attend(qbd, ks_ref[...], vs_ref[...], bsel_ref[...] + mb)

        n_st = wst_ref.shape[0] // (2 * NSA_KV_HEADS)
        wnew = wnew_ref[...]
        kw = jnp.concatenate([wst_ref[pl.ds(k, n_st, stride=2 * NSA_KV_HEADS), :] for k in range(NSA_KV_HEADS)], axis=1)
        vw = jnp.concatenate([wst_ref[pl.ds(NSA_KV_HEADS + k, n_st, stride=2 * NSA_KV_HEADS), :]
                              for k in range(NSA_KV_HEADS)], axis=1)
        kw = jnp.concatenate([kw, pad_rows(wnew[:, :KV_LANES])], axis=0).astype(BF16)
        vw = jnp.concatenate([vw, pad_rows(wnew[:, KV_LANES:])], axis=0).astype(BF16)
        o_win, _ = _softmax_attend(qbd, kw, vw, bwin_ref[...])

        gates = gate_ref[...]
        for h in range(NSA_HEADS):
            kvh, g = divmod(h, NSA_GROUP)
            rs = slice(h * t, (h + 1) * t)
            ls = slice(kvh * HEAD_DIM, (kvh + 1) * HEAD_DIM)
            gc = kvh * LANE + 3 * g
            o_ref[:, h * HEAD_DIM:(h + 1) * HEAD_DIM] = (
                gates[:, gc:gc + 1] * o_cmp[rs, ls] + gates[:, gc + 1:gc + 2] * o_sel[rs, ls]
                + gates[:, gc + 2:gc + 3] * o_win[rs, ls])


def nsa_sample_attention(q, pages, page_table, kv_new, win_state, win_new, bias_cmp, bias_sel, bias_win, cmp_w, gates, seq):
    bsz, n_pages = page_table.shape
    n = bsz * seq
    width = NSA_HEADS * HEAD_DIM
    n_past = n_pages * PAGE
    n_sel = -(-(n_past + seq) // SEL_BLOCK)
    lp = bias_sel.shape[1]
    assert lp == n_past + PAGE and lp >= n_sel * SEL_BLOCK
    ci = np.arange(N_CMP_PAD)[:, None] * CMP_STRIDE
    sj = np.arange(LANE)[None, :] * SEL_BLOCK
    overlap = jnp.asarray(((ci < sj + SEL_BLOCK) & (ci + CMP_BLOCK > sj)).astype(np.float32), BF16)
    expand = jnp.asarray((np.arange(lp)[None, :] // SEL_BLOCK == np.arange(LANE)[:, None]).astype(np.float32), BF16)
    cst = lambda a: pl.BlockSpec(a.shape, lambda b, p, pt: (0,) * a.ndim)
    per_seq = lambda w, j: pl.BlockSpec((seq, w), lambda b, p, pt: (b, j))
    consts = [bias_cmp, bias_sel, bias_win] + list(cmp_w) + [overlap, expand]
    return pl.pallas_call(
        functools.partial(_nsa_sample_kernel, n_past, n_sel),
        grid_spec=pltpu.PrefetchScalarGridSpec(
            num_scalar_prefetch=1,
            grid=(bsz, n_pages),
            in_specs=[
                per_seq(width, 0),
                pl.BlockSpec((None, PAGE * ROWS_PER_TOKEN, HEAD_DIM), lambda b, p, pt: (pt[b, p], 0, 0)),
                per_seq(2 * KV_LANES, 1),
                pl.BlockSpec((None, win_state.shape[1], HEAD_DIM), lambda b, p, pt: (b, 0, 0)),
                per_seq(2 * KV_LANES, 0),
            ] + [cst(a) for a in consts] + [per_seq(NSA_KV_HEADS * LANE, 0)],
            out_specs=per_seq(width, 0),
            scratch_shapes=[pltpu.VMEM((2, NSA_KV_HEADS * N_CMP_PAD, CMP_STRIDE * HEAD_DIM), F32),
                            pltpu.VMEM((lp, KV_LANES), BF16), pltpu.VMEM((lp, KV_LANES), BF16)],
        ),
        out_shape=jax.ShapeDtypeStruct((n, width), F32),
        compiler_params=_cparams("parallel", "arbitrary"),
        name="nsa_sample_attention",
    )(page_table, q, pages, kv_new, win_state, win_new, *consts, gates)


HGRN_HEADS = 16
HGRN_WIDTH = HGRN_HEADS * HEAD_DIM
NSA_WIDTH = NSA_HEADS * HEAD_DIM
SCALE = HEAD_DIM ** -0.5


def _layer_weights(i, ffn1_w_gate, ffn1_w_up, ffn1_w_down, w_in, w_out, nsa_q_norm, nsa_k_norm,
                   ffn2_w_gate, ffn2_w_up, ffn2_w_down, ple_w_gate, ple_w_proj):
    c0 = 4 * HGRN_WIDTH
    c1 = c0 + NSA_WIDTH
    c2 = c1 + 4 * KV_LANES
    c3 = c2 + 2 * KV_LANES
    wi = w_in[i]
    wgate = wi[:, c3:].reshape(-1, NSA_KV_HEADS, 3 * NSA_GROUP)
    wgate = jnp.pad(wgate, ((0, 0), (0, 0), (0, LANE - 3 * NSA_GROUP))).reshape(-1, NSA_KV_HEADS * LANE)
    ones = jnp.ones((1, 1, HEAD_DIM), F32)
    return dict(
        ffn1=(ffn1_w_gate[i].astype(BF16), ffn1_w_up[i].astype(BF16), ffn1_w_down[i].astype(BF16)),
        ffn2=(ffn2_w_gate[i].astype(BF16), ffn2_w_up[i].astype(BF16), ffn2_w_down[i].astype(BF16)),
        w_h=wi[:, :c0].astype(BF16), w_q=wi[:, c0:c1].astype(BF16), w_kv=wi[:, c1:c2].astype(BF16),
        w_win=wi[:, c2:c3].astype(BF16), w_gate=wgate.astype(BF16),
        g_q=jnp.broadcast_to(nsa_q_norm[i].reshape(1, 1, HEAD_DIM), (NSA_WIDTH // 512, 1, HEAD_DIM)),
        g_kv=jnp.concatenate([ones, ones, nsa_k_norm[i, 1].reshape(1, 1, HEAD_DIM), ones], axis=0),
        g_win=jnp.concatenate([nsa_k_norm[i, 2].reshape(1, 1, HEAD_DIM), ones], axis=0),
        wo_h=w_out[i, :HGRN_WIDTH].astype(BF16), wo_n=w_out[i, HGRN_WIDTH:].astype(BF16),
        ple_gate=ple_w_gate[i].astype(BF16), ple_proj=ple_w_proj[i].astype(BF16),
    )


def _run_layer(i, w, x, pemb, bsz, seq, s0, nsa_fn, norms, hgrn_lb):
    ffn1_norm, mix_norm, hgrn_out_norm, ffn2_norm, ple_norm, ple_post_norm = norms
    small = seq % 16 != 0
    x1 = ffn_residual(x, ffn1_norm[i], *w["ffn1"])
    (zh,) = norm_project(x1, mix_norm[i], w["w_h"], _epi_plain, (F32,))
    o_h, s_fin = hgrn2_mix(zh, hgrn_lb, hgrn_out_norm[i], i, bsz, seq, HGRN_HEADS, s0=s0,
                           heads_per_step=HGRN_HEADS if small else 4)
    qdt = F32 if small else BF16
    (q,) = norm_project(x1, mix_norm[i], w["w_q"], functools.partial(_epi_headnorm, (0, 1, 2, 3), SCALE, (qdt,)),
                        (qdt,), head_gain=w["g_q"])
    kv, kvb = norm_project(x1, mix_norm[i], w["w_kv"], functools.partial(_epi_headnorm, (2,), 1.0, (F32, BF16)),
                           (F32, BF16), head_gain=w["g_kv"])
    win, winb = norm_project(x1, mix_norm[i], w["w_win"], functools.partial(_epi_headnorm, (0,), 1.0, (F32, BF16)),
                             (F32, BF16), head_gain=w["g_win"])
    (gates,) = norm_project(x1, mix_norm[i], w["w_gate"], _epi_sigmoid, (F32,), tn_pref=LANE)
    o_n = nsa_fn(q, kv, kvb, win, winb, gates)
    x2 = out_project_residual(x1, o_h, o_n, w["wo_h"], w["wo_n"])
    x3 = ffn_residual(x2, ffn2_norm[i], *w["ffn2"])
    y = ple_residual(x3, ple_norm[i], w["ple_gate"], pemb, w["ple_proj"], ple_post_norm[i])
    return y, kv, win, s_fin


def kernel(x_prompt, x_sample, cache_kv, state_win_kv, state_hgrn, page_table, p_prompt, p_sample, ffn1_norm, ffn1_w_gate, ffn1_w_up, ffn1_w_down, mix_norm, w_in, w_out, hgrn_lb, hgrn_out_norm, nsa_q_norm, nsa_k_norm, cmp_pos, cmp_w1, cmp_w2, rel_bias_table, ffn2_norm, ffn2_w_gate, ffn2_w_up, ffn2_w_down, ple_norm, ple_w_gate, ple_w_proj, ple_post_norm):
    depth = cache_kv.shape[0]
    bp, tp, d = x_prompt.shape
    bs, ts, _ = x_sample.shape
    n_pool = cache_kv.shape[1]
    n_pages = page_table.shape[1]
    past = n_pages * PAGE
    win_keep = state_win_kv.shape[2]
    assert tp % PAGE == 0 and tp >= WINDOW and win_keep == WINDOW
    norms = (ffn1_norm, mix_norm, hgrn_out_norm, ffn2_norm, ple_norm, ple_post_norm)
    table = rel_bias_table.astype(F32)

    n_cmp = (tp - CMP_BLOCK) // CMP_STRIDE + 1
    bias_pc = rel_bias(table, tp, N_CMP_PAD, 0, CMP_BLOCK - 1, CMP_STRIDE, n_cmp)
    bias_p5 = rel_bias(table, WINDOW + 3 * PAGE, PAGE, -PAGE, 0, 1, PAGE, window=WINDOW)
    n_cmp_s = (past + ts - CMP_BLOCK) // CMP_STRIDE + 1
    assert n_cmp_s <= N_CMP_PAD - 1 and (n_cmp_s - 1) * CMP_STRIDE + CMP_BLOCK <= past
    sel_cols = past + PAGE
    bias_sc = rel_bias(table, ts, N_CMP_PAD, past, CMP_BLOCK - 1, CMP_STRIDE, n_cmp_s).reshape(NSA_HEADS * ts, N_CMP_PAD)
    bias_ss = rel_bias(table, ts, sel_cols, past, 0, 1, past + ts).reshape(NSA_HEADS * ts, sel_cols)
    bias_sw = rel_bias(table, ts, win_keep + PAGE, win_keep, 0, 1, win_keep + ts, window=WINDOW)
    bias_sw = bias_sw.reshape(NSA_HEADS * ts, win_keep + PAGE)

    xp = x_prompt.reshape(bp * tp, d)
    xs = x_sample.reshape(bs * ts, d)
    outs = [[] for _ in range(6)]
    for i in range(depth):
        w = _layer_weights(i, ffn1_w_gate, ffn1_w_up, ffn1_w_down, w_in, w_out, nsa_q_norm, nsa_k_norm,
                           ffn2_w_gate, ffn2_w_up, ffn2_w_down, ple_w_gate, ple_w_proj)
        cmp_w = compress_weights(cmp_w1[i], cmp_w2[i], cmp_pos[i], nsa_k_norm[i, 0])

        def nsa_prompt(q, kv, kvb, win, winb, gates):
            pages = kv.reshape(bp * tp // PAGE, PAGE, 4 * KV_LANES)
            pt = jnp.arange(bp * tp // PAGE, dtype=jnp.int32).reshape(bp, tp // PAGE)
            kc, vc = compress_cache(pages, pt, 0, cmp_w)
            o_cmp, mask = cmp_attention_topk(q, kc, vc, bias_pc, bp, tp, 0, tp)
            return nsa_prompt_attention(q, kvb, winb, bias_p5, mask, o_cmp, gates, bp, tp)

        def nsa_sample(q, kv, kvb, win, winb, gates):
            pages = cache_kv[i].reshape(n_pool, PAGE * ROWS_PER_TOKEN, HEAD_DIM)
            wst = state_win_kv[i].reshape(bs, win_keep * 2 * NSA_KV_HEADS, HEAD_DIM)
            return nsa_sample_attention(q, pages, page_table, kv, wst, win, bias_sc, bias_ss, bias_sw, cmp_w, gates, ts)

        xp, kv_p, win_p, h_p = _run_layer(i, w, xp, p_prompt[i].reshape(bp * tp, -1), bp, tp, None, nsa_prompt, norms, hgrn_lb)
        xs, kv_s, win_s, h_s = _run_layer(i, w, xs, p_sample[i].reshape(bs * ts, -1), bs, ts, state_hgrn[i], nsa_sample, norms, hgrn_lb)
        outs[0].append(kv_p.reshape(bp, tp, 4, NSA_KV_HEADS, HEAD_DIM))
        outs[1].append(win_p.reshape(bp, tp, 2, NSA_KV_HEADS, HEAD_DIM)[:, -WINDOW:])
        outs[2].append(h_p)
        outs[3].append(kv_s.reshape(bs, ts, 4, NSA_KV_HEADS, HEAD_DIM))
        win_new = win_s.reshape(bs, ts, 2, NSA_KV_HEADS, HEAD_DIM)
        outs[4].append(jnp.concatenate([state_win_kv[i], win_new], axis=1)[:, -win_keep:])
        outs[5].append(h_s.astype(state_hgrn.dtype))
    return (xp.reshape(bp, tp, d), xs.reshape(bs, ts, d)) + tuple(jnp.stack(o) for o in outs)
```

```python
import functools
import math

import numpy as np
import jax
import jax.numpy as jnp
from jax import lax
from jax.experimental import pallas as pl
from jax.experimental.pallas import tpu as pltpu

F32 = jnp.float32
BF16 = jnp.bfloat16

LANE = 128
SUBLANE = 8
VMEM_LIMIT_BYTES = 56 * 1024 * 1024

HEAD_DIM = 128
RMS_EPS = 1e-6
NEG_INF = -1e30


def _cparams(*sem):
    return pltpu.CompilerParams(dimension_semantics=sem, vmem_limit_bytes=VMEM_LIMIT_BYTES)


def _tile(n, pref):
    if n <= pref:
        return n
    t = pref
    while t >= SUBLANE:
        if n % t == 0:
            return t
        t -= SUBLANE
    return n


def _rms(x, gain):
    ms = jnp.mean(x * x, axis=-1, keepdims=True)
    return x * lax.rsqrt(ms + RMS_EPS) * gain


def _norm_rows_to(x_ref, gain_ref, dst_ref):
    rows = x_ref.shape[0]
    rc = 32 if rows % 32 == 0 else rows

    def body(i, carry):
        r = pl.multiple_of(i * rc, rc)
        dst_ref[pl.ds(r, rc), :] = _rms(x_ref[pl.ds(r, rc), :], gain_ref[...]).astype(dst_ref.dtype)
        return carry

    lax.fori_loop(0, rows // rc, body, 0)


def _ffn_kernel(x_ref, gain_ref, wg_ref, wu_ref, wd_ref, o_ref, hn_ref):
    @pl.when(pl.program_id(1) == 0)
    def _():
        _norm_rows_to(x_ref, gain_ref, hn_ref)
        o_ref[...] = x_ref[...]

    h = hn_ref[...]
    g = jnp.dot(h, wg_ref[...], preferred_element_type=F32)
    u = jnp.dot(h, wu_ref[...], preferred_element_type=F32)
    a = (g * jax.nn.sigmoid(g) * (0.5 * u)).astype(BF16)
    o_ref[...] += jnp.dot(a, wd_ref[...], preferred_element_type=F32)


def ffn_residual(x, gain, wg, wu, wd, tm_pref=512, tf_pref=256):
    n, d = x.shape
    f = wg.shape[1]
    tm, tf = _tile(n, tm_pref), _tile(f, tf_pref)
    return pl.pallas_call(
        _ffn_kernel,
        grid=(n // tm, f // tf),
        in_specs=[
            pl.BlockSpec((tm, d), lambda i, j: (i, 0)),
            pl.BlockSpec((1, d), lambda i, j: (0, 0)),
            pl.BlockSpec((d, tf), lambda i, j: (0, j)),
            pl.BlockSpec((d, tf), lambda i, j: (0, j)),
            pl.BlockSpec((tf, d), lambda i, j: (j, 0)),
        ],
        out_specs=pl.BlockSpec((tm, d), lambda i, j: (i, 0)),
        out_shape=jax.ShapeDtypeStruct((n, d), F32),
        scratch_shapes=[pltpu.VMEM((tm, d), BF16)],
        compiler_params=_cparams("parallel", "arbitrary"),
        name="ffn_residual",
    )(x, gain.reshape(1, d), wg, wu, wd)


PROJ_TILE = 512


def _mixproj_kernel(norm_tiles, gate_tile, x_ref, gain_ref, w_ref, hg_ref, z32_ref, z16_ref, hn_ref):
    j = pl.program_id(1)

    @pl.when(j == 0)
    def _():
        _norm_rows_to(x_ref, gain_ref, hn_ref)

    acc = jnp.dot(hn_ref[...], w_ref[...], preferred_element_type=F32)
    is_norm = functools.reduce(jnp.logical_or, [j == t for t in norm_tiles])
    is_gate = j == gate_tile

    def put(val, sl=slice(None)):
        z32_ref[:, sl] = val
        z16_ref[:, sl] = val.astype(BF16)

    @pl.when(is_norm)
    def _():
        gain = hg_ref[0]
        for h in range(acc.shape[1] // HEAD_DIM):
            sl = slice(h * HEAD_DIM, (h + 1) * HEAD_DIM)
            put(_rms(acc[:, sl], gain), sl)

    @pl.when(is_gate)
    def _():
        put(jax.nn.sigmoid(acc))

    @pl.when(jnp.logical_not(jnp.logical_or(is_norm, is_gate)))
    def _():
        put(acc)


def mixer_project(x, gain, w, head_gain, norm_tiles, gate_tile, tm_pref=512):
    n, d = x.shape
    ncols = w.shape[1]
    tm, tn = _tile(n, tm_pref), PROJ_TILE
    return pl.pallas_call(
        functools.partial(_mixproj_kernel, norm_tiles, gate_tile),
        grid=(n // tm, ncols // tn),
        in_specs=[
            pl.BlockSpec((tm, d), lambda i, j: (i, 0)),
            pl.BlockSpec((1, d), lambda i, j: (0, 0)),
            pl.BlockSpec((d, tn), lambda i, j: (0, j)),
            pl.BlockSpec((1, 1, HEAD_DIM), lambda i, j: (j, 0, 0)),
        ],
        out_specs=[pl.BlockSpec((tm, tn), lambda i, j: (i, j))] * 2,
        out_shape=[jax.ShapeDtypeStruct((n, ncols), F32), jax.ShapeDtypeStruct((n, ncols), BF16)],
        scratch_shapes=[pltpu.VMEM((tm, d), BF16)],
        compiler_params=_cparams("parallel", "arbitrary"),
        name="mixer_project",
    )(x, gain.reshape(1, d), w, head_gain)


def _outproj_kernel(x_ref, a_ref, b_ref, wa_ref, wb_ref, o_ref):
    acc = jnp.dot(a_ref[...].astype(BF16), wa_ref[...], preferred_element_type=F32)
    acc += jnp.dot(b_ref[...].astype(BF16), wb_ref[...], preferred_element_type=F32)
    o_ref[...] = x_ref[...] + acc


def out_project_residual(x, a, b, wa, wb, tm_pref=512, tn_pref=1024):
    n, d = x.shape
    ka, kb = a.shape[1], b.shape[1]
    tm, tn = _tile(n, tm_pref), _tile(d, tn_pref)
    return pl.pallas_call(
        _outproj_kernel,
        grid=(n // tm, d // tn),
        in_specs=[
            pl.BlockSpec((tm, tn), lambda i, j: (i, j)),
            pl.BlockSpec((tm, ka), lambda i, j: (i, 0)),
            pl.BlockSpec((tm, kb), lambda i, j: (i, 0)),
            pl.BlockSpec((ka, tn), lambda i, j: (0, j)),
            pl.BlockSpec((kb, tn), lambda i, j: (0, j)),
        ],
        out_specs=pl.BlockSpec((tm, tn), lambda i, j: (i, j)),
        out_shape=jax.ShapeDtypeStruct((n, d), F32),
        compiler_params=_cparams("parallel", "arbitrary"),
        name="out_project_residual",
    )(x, a, b, wa, wb)


def _ple_kernel(tn, x_ref, gain_ref, wg_ref, p_ref, wp_ref, pg_ref, o_ref, hn_ref, pe_ref):
    j = pl.program_id(1)

    @pl.when(j == 0)
    def _():
        _norm_rows_to(x_ref, gain_ref, hn_ref)
        pe_ref[...] = jnp.dot(p_ref[...].astype(BF16), wp_ref[...], preferred_element_type=F32)
        _norm_rows_to(pe_ref, pg_ref, pe_ref)

    c = pl.multiple_of(j * tn, LANE)
    gate = jax.nn.sigmoid(jnp.dot(hn_ref[...], wg_ref[...], preferred_element_type=F32))
    o_ref[...] = x_ref[:, pl.ds(c, tn)] + gate * pe_ref[:, pl.ds(c, tn)]


def ple_residual(x, gain, wg, p, wp, post_gain, tm_pref=512, tn_pref=512):
    n, d = x.shape
    pd = p.shape[1]
    tm, tn = _tile(n, tm_pref), _tile(d, tn_pref)
    return pl.pallas_call(
        functools.partial(_ple_kernel, tn),
        grid=(n // tm, d // tn),
        in_specs=[
            pl.BlockSpec((tm, d), lambda i, j: (i, 0)),
            pl.BlockSpec((1, d), lambda i, j: (0, 0)),
            pl.BlockSpec((d, tn), lambda i, j: (0, j)),
            pl.BlockSpec((tm, pd), lambda i, j: (i, 0)),
            pl.BlockSpec((pd, d), lambda i, j: (0, 0)),
            pl.BlockSpec((1, d), lambda i, j: (0, 0)),
        ],
        out_specs=pl.BlockSpec((tm, tn), lambda i, j: (i, j)),
        out_shape=jax.ShapeDtypeStruct((n, d), F32),
        scratch_shapes=[pltpu.VMEM((tm, d), BF16), pltpu.VMEM((tm, d), F32)],
        compiler_params=_cparams("parallel", "arbitrary"),
        name="ple_residual",
    )(x, gain.reshape(1, d), wg, p, wp, post_gain.reshape(1, d))


HGRN_DIAG = SUBLANE


def _split3_dot(lhs_bf16, x):
    hi = x.astype(BF16)
    r1 = x - hi.astype(F32)
    mid = r1.astype(BF16)
    lo = (r1 - mid.astype(F32)).astype(BF16)
    acc = jnp.dot(lhs_bf16, hi, preferred_element_type=F32)
    acc += jnp.dot(lhs_bf16, mid, preferred_element_type=F32)
    acc += jnp.dot(lhs_bf16, lo, preferred_element_type=F32)
    return acc


def _bcast_row_in_blocks(a, bs, row):
    c, k = a.shape
    a3 = a.reshape(c // bs, bs, k)
    return jnp.broadcast_to(a3[:, row:row + 1, :], (c // bs, bs, k)).reshape(c, k)


def _nt_dot(a, b):
    return lax.dot_general(a, b, (((1,), (1,)), ((), ())), preferred_element_type=F32)


def _hgrn_kernel(layer, has_s0, hps, q_ref, f_ref, i_ref, g_ref, lb_ref, og_ref, *rest):
    if has_s0:
        s0_ref, o_ref, sfin_ref, st_ref = rest
    else:
        s0_ref = None
        o_ref, sfin_ref, st_ref = rest
    for hh in range(hps):
        ls = slice(hh * HEAD_DIM, (hh + 1) * HEAD_DIM)
        _hgrn_head(layer, q_ref.at[:, ls], f_ref.at[:, ls], i_ref.at[:, ls], g_ref.at[:, ls], lb_ref.at[:, ls],
                   og_ref, None if s0_ref is None else s0_ref.at[hh], o_ref.at[:, ls], sfin_ref.at[hh], st_ref.at[hh])


def _hgrn_head(layer, q_ref, f_ref, i_ref, g_ref, lb_ref, og_ref, s0_ref, o_ref, sfin_ref, st_ref):
    ci = pl.program_id(2)
    c = q_ref.shape[0]

    @pl.when(ci == 0)
    def _():
        if s0_ref is not None:
            st_ref[...] = s0_ref[...].T
        else:
            st_ref[...] = jnp.zeros(st_ref.shape, F32)

    lbr = lb_ref[...]
    e = jnp.exp(lbr - jnp.max(lbr, axis=0, keepdims=True))
    lb = jnp.sum(e[:layer + 1], axis=0, keepdims=True) / jnp.sum(e, axis=0, keepdims=True)

    q = q_ref[...]
    fg = lb + (1.0 - lb) * jax.nn.sigmoid(f_ref[...])
    logf = jnp.log(fg)
    kk = 1.0 - fg
    v = i_ref[...]

    ti = lax.broadcasted_iota(jnp.int32, (c, c), 0)
    si = lax.broadcasted_iota(jnp.int32, (c, c), 1)
    tril = (si <= ti)
    a = _split3_dot(tril.astype(BF16), logf)
    a_last = a[c - 1:c, :]

    kk_b = kk.astype(BF16)
    d = HGRN_DIAG
    xs = [(q * jnp.exp(jnp.minimum(a - _bcast_row_in_blocks(a, d, j), 0.0))).astype(BF16) for j in range(d)]
    res = _nt_dot(jnp.concatenate(xs, axis=0), kk_b)
    attn = jnp.zeros((c, c), F32)
    for j in range(d):
        attn += jnp.where((si % d) == j, res[j * c:(j + 1) * c], 0.0)
    attn = jnp.where(((si // d) == (ti // d)) & tril, attn, 0.0)
    bs = 2 * d
    while bs <= c:
        half = bs // 2
        bnd = _bcast_row_in_blocks(a, bs, half - 1)
        qe = (q * jnp.exp(jnp.minimum(a - bnd, 0.0))).astype(BF16)
        ke = (kk * jnp.exp(jnp.minimum(bnd - a, 0.0))).astype(BF16)
        m = ((si // bs) == (ti // bs)) & ((ti % bs) >= half) & ((si % bs) < half)
        attn += jnp.where(m, _nt_dot(qe, ke), 0.0)
        bs *= 2

    st = st_ref[...]
    v_b = v.astype(BF16)
    o = jnp.dot(attn.astype(BF16), v_b, preferred_element_type=F32)
    o += _nt_dot((q * jnp.exp(a)).astype(BF16), st.astype(BF16))
    kd = (kk * jnp.exp(a_last - a)).astype(BF16)
    st_new = st * jnp.exp(a_last) + jnp.dot(v_b.T, kd, preferred_element_type=F32)
    st_ref[...] = st_new

    gr = g_ref[...]
    o_ref[...] = (_rms(o, og_ref[...]) * (gr * jax.nn.sigmoid(gr))).astype(o_ref.dtype)

    @pl.when(ci == pl.num_programs(2) - 1)
    def _():
        sfin_ref[...] = st_new.T


def hgrn2_mix(zh, hgrn_lb, out_gain, layer, batch, seq, heads, s0=None, chunk_pref=128, heads_per_step=4):
    n = batch * seq
    dk = HEAD_DIM
    c = _tile(seq, chunk_pref)
    assert c % HGRN_DIAG == 0 and (c // HGRN_DIAG) & (c // HGRN_DIAG - 1) == 0
    nc = seq // c
    nl = hgrn_lb.shape[0]
    hps = heads_per_step
    hg = heads // hps

    def zspec(sec):
        return pl.BlockSpec((c, hps * dk), lambda b, h, ci: (b * nc + ci, sec * hg + h))

    state_spec = pl.BlockSpec((None, hps, dk, dk), lambda b, h, ci: (b, h, 0, 0))
    in_specs = [zspec(0), zspec(1), zspec(2), zspec(3),
                pl.BlockSpec((nl, hps * dk), lambda b, h, ci: (0, h)),
                pl.BlockSpec((1, dk), lambda b, h, ci: (0, 0))]
    args = [zh, zh, zh, zh, hgrn_lb, out_gain.reshape(1, dk)]
    if s0 is not None:
        in_specs.append(state_spec)
        args.append(s0)
    return pl.pallas_call(
        functools.partial(_hgrn_kernel, layer, s0 is not None, hps),
        grid=(batch, hg, nc),
        in_specs=in_specs,
        out_specs=[pl.BlockSpec((c, hps * dk), lambda b, h, ci: (b * nc + ci, h)), state_spec],
        out_shape=[jax.ShapeDtypeStruct((n, heads * dk), BF16 if c % 16 == 0 else F32),
                   jax.ShapeDtypeStruct((batch, heads, dk, dk), F32)],
        scratch_shapes=[pltpu.VMEM((hps, dk, dk), F32)],
        compiler_params=_cparams("parallel", "parallel", "arbitrary"),
        name="hgrn2_mix",
    )(*args)


NSA_KV_HEADS = 4
NSA_GROUP = 4
NSA_HEADS = NSA_KV_HEADS * NSA_GROUP
KV_LANES = NSA_KV_HEADS * HEAD_DIM
PAGE = 128
CMP_BLOCK = 32
CMP_STRIDE = 16
CMP_PER_PAGE = PAGE // CMP_STRIDE
N_CMP_PAD = 128
SEL_BLOCK = 64
SEL_TOPK = 16
SEL_LOCAL = 2
SEL_FORCE = 1e3
SEL_INVALID = -1e9
WINDOW = 512
N_BUCKETS = 32
MAX_DISTANCE = 128
MASKED = -1e29

HGRN_HEADS = 16
HGRN_WIDTH = HGRN_HEADS * HEAD_DIM
NSA_WIDTH = NSA_HEADS * HEAD_DIM
COL_Q = 4 * HGRN_WIDTH
COL_KV = COL_Q + NSA_WIDTH
COL_WIN = COL_KV + 4 * KV_LANES
COL_GATE = COL_WIN + 2 * KV_LANES
Z_COLS = COL_GATE + NSA_KV_HEADS * LANE


def _bias_kernel(q0, k0, kstride, ncols_valid, window, table_ref, o_ref):
    h = pl.program_id(0)
    rt, cols = o_ref.shape
    r = lax.broadcasted_iota(jnp.int32, (rt, cols), 0) + pl.program_id(1) * rt
    c = lax.broadcasted_iota(jnp.int32, (rt, cols), 1)
    dist = (q0 + r) - (k0 + c * kstride)
    dpos = jnp.maximum(dist, 0)
    max_exact = N_BUCKETS // 2
    log_ratio = jnp.log(jnp.maximum(dpos, 1).astype(F32) / max_exact) / math.log(MAX_DISTANCE / max_exact)
    large = jnp.minimum(max_exact + (log_ratio * (N_BUCKETS - max_exact)).astype(jnp.int32), N_BUCKETS - 1)
    bucket = jnp.where(dpos < max_exact, dpos, large)
    acc = jnp.zeros((rt, cols), F32)
    for b in range(N_BUCKETS):
        acc = jnp.where(bucket == b, table_ref[b, h], acc)
    valid = (dist >= 0) & (c < ncols_valid)
    if window is not None:
        valid = valid & (dist < window)
    o_ref[...] = jnp.where(valid, acc, NEG_INF)


def rel_bias(table, rows, cols, q0, k0, kstride, ncols_valid, window=None):
    rt = _tile(rows, 256)
    return pl.pallas_call(
        functools.partial(_bias_kernel, q0, k0, kstride, ncols_valid, window),
        grid=(NSA_HEADS, rows // rt),
        in_specs=[pl.BlockSpec(memory_space=pltpu.SMEM)],
        out_specs=pl.BlockSpec((None, rt, cols), lambda h, i: (h, i, 0)),
        out_shape=jax.ShapeDtypeStruct((NSA_HEADS, rows, cols), F32),
        compiler_params=_cparams("parallel", "parallel"),
        name="rel_bias",
    )(table)


def _compress_kernel(pt_ref, *refs):
    n_in = 2 * NSA_KV_HEADS
    page_refs = refs[:n_in]
    wck_ref, wcv_ref, posk_ref, posv_ref, w2k_ref, w2v_ref, kn_ref, kc_ref, vc_ref, xs_ref = refs[n_in:]
    p = pl.program_id(1)
    for slot in range(2):
        for kvh in range(NSA_KV_HEADS):
            page_ref = page_refs[slot * NSA_KV_HEADS + kvh]
            row0 = pl.multiple_of(kvh * N_CMP_PAD + p * CMP_PER_PAGE, CMP_PER_PAGE)
            for r in range(CMP_STRIDE):
                xs_ref[slot, pl.ds(row0, CMP_PER_PAGE), r * HEAD_DIM:(r + 1) * HEAD_DIM] = (
                    page_ref[pl.ds(r, CMP_PER_PAGE, stride=CMP_STRIDE), :])

    @pl.when(p == pl.num_programs(1) - 1)
    def _():
        kc = _compress_finish(xs_ref[0].astype(BF16), wck_ref[...], posk_ref[...], w2k_ref[...], kn_ref[...])
        vc = _compress_finish(xs_ref[1].astype(BF16), wcv_ref[...], posv_ref[...], w2v_ref[...], None)
        kc_ref[...] = kc.reshape(NSA_KV_HEADS, N_CMP_PAD, HEAD_DIM).astype(kc_ref.dtype)
        vc_ref[...] = vc.reshape(NSA_KV_HEADS, N_CMP_PAD, HEAD_DIM).astype(vc_ref.dtype)


def _compress_finish(x, wcat, posb, w2, k_gain, kvh_minor=False):
    m_rows = x.shape[0]
    hid = w2.shape[0]
    step = NSA_KV_HEADS if kvh_minor else 1
    pq = jnp.dot(x, wcat, preferred_element_type=F32)
    nxt = pltpu.roll(pq[:, hid:], m_rows - step, 0)
    hcur = pq[:, :hid] + nxt + posb
    act = (hcur * jax.nn.sigmoid(hcur)).astype(BF16)
    out = jnp.dot(act, w2, preferred_element_type=F32)
    if k_gain is not None:
        out = _rms(out, k_gain)
    rowid = lax.broadcasted_iota(jnp.int32, out.shape, 0)
    blk = rowid // step if kvh_minor else rowid % N_CMP_PAD
    return jnp.where(blk == N_CMP_PAD - 1, 0.0, out)


def _posb_kernel(pos_ref, w1_ref, o_ref):
    o_ref[...] = jnp.dot(pos_ref[...].astype(BF16), w1_ref[...], preferred_element_type=F32)


def compress_weights(w1, w2, pos, k_norm_cmp):
    hid = w2.shape[1]
    out = []
    for s in range(2):
        wcat = w1[s].reshape(2, CMP_STRIDE, HEAD_DIM, hid).transpose(1, 2, 0, 3).reshape(CMP_STRIDE * HEAD_DIM, 2 * hid)
        posf = jnp.broadcast_to(pos[s].reshape(1, CMP_BLOCK * HEAD_DIM), (SUBLANE, CMP_BLOCK * HEAD_DIM))
        posb = pl.pallas_call(_posb_kernel, out_shape=jax.ShapeDtypeStruct((SUBLANE, hid), F32),
                              name="cmp_pos_bias")(posf, w1[s].astype(BF16))[0:1]
        out.append((wcat.astype(BF16), posb, w2[s].astype(BF16)))
    (wck, posk, w2k), (wcv, posv, w2v) = out
    return [wck, wcv, posk, posv, w2k, w2v, k_norm_cmp.reshape(1, HEAD_DIM)]


def compress_cache(pages, page_table, lane_block, cmp_w):
    bsz, n_pages = page_table.shape
    assert n_pages * CMP_PER_PAGE == N_CMP_PAD
    full = lambda a: pl.BlockSpec(a.shape, lambda b, p, pt: (0,) * a.ndim)
    n_in = 2 * NSA_KV_HEADS

    def page_map(lane_blk, b, p, pt):
        return (pt[b, p], 0, lane_blk)

    ins = cmp_w
    out_spec = pl.BlockSpec((None, NSA_KV_HEADS, N_CMP_PAD, HEAD_DIM), lambda b, p, pt: (b, 0, 0, 0))
    out_sds = jax.ShapeDtypeStruct((bsz, NSA_KV_HEADS, N_CMP_PAD, HEAD_DIM), BF16)
    return pl.pallas_call(
        _compress_kernel,
        grid_spec=pltpu.PrefetchScalarGridSpec(
            num_scalar_prefetch=1,
            grid=(bsz, n_pages),
            in_specs=[pl.BlockSpec((None, PAGE, HEAD_DIM), functools.partial(page_map, lane_block + j))
                      for j in range(n_in)] + [full(a) for a in ins],
            out_specs=[out_spec, out_spec],
            scratch_shapes=[pltpu.VMEM((2, NSA_KV_HEADS * N_CMP_PAD, CMP_STRIDE * HEAD_DIM), F32)],
        ),
        out_shape=[out_sds, out_sds],
        compiler_params=_cparams("parallel", "arbitrary"),
        name="compress_cache",
    )(page_table, *([pages] * n_in), *ins)


def _heads_to_rows(q_ref):
    return jnp.concatenate([q_ref[:, g * HEAD_DIM:(g + 1) * HEAD_DIM] for g in range(NSA_GROUP)], axis=0).astype(BF16)


def _dot3_lhs(x, rhs_bf16):
    hi = x.astype(BF16)
    r1 = x - hi.astype(F32)
    mid = r1.astype(BF16)
    lo = (r1 - mid.astype(F32)).astype(BF16)
    acc = jnp.dot(hi, rhs_bf16, preferred_element_type=F32)
    acc += jnp.dot(mid, rhs_bf16, preferred_element_type=F32)
    acc += jnp.dot(lo, rhs_bf16, preferred_element_type=F32)
    return acc


def _selected_key_mask(imp, expand, qpos, kpos, n_sel):
    r = imp.shape[0]
    lane = lax.broadcasted_iota(jnp.int32, (r, LANE), 1)
    lag = qpos // SEL_BLOCK - lane
    forced = (lane == 0) | ((lag >= 0) & (lag < SEL_LOCAL))
    score = jnp.where(lag >= 0, imp + jnp.where(forced, SEL_FORCE, 0.0), SEL_INVALID)
    cnt = jnp.zeros((r, LANE), jnp.int32)
    for j in range(n_sel):
        col = score[:, j:j + 1]
        cnt += jnp.where(col > score, 1, jnp.where(col == score, jnp.where(lane > j, 1, 0), 0))
    sel = jnp.where(cnt < min(SEL_TOPK, n_sel), jnp.where(lane < n_sel, 1.0, 0.0), 0.0)
    ex = jnp.dot(sel.astype(BF16), expand, preferred_element_type=F32)
    keep = jnp.where(kpos <= qpos, ex, 0.0) > 0.5
    return jnp.where(keep, 0.0, NEG_INF)


def _cmp_topk_kernel(q0, n_sel, q_ref, kc_ref, vc_ref, bias_ref, ovl_ref, exp_ref, ocmp_ref, mask_ref):
    qi = pl.program_id(2)
    tq = q_ref.shape[0]
    q4 = _heads_to_rows(q_ref)
    bias = bias_ref[...].reshape(NSA_GROUP * tq, N_CMP_PAD)
    s = _nt_dot(q4, kc_ref[...]) + bias
    m = jnp.max(s, axis=-1, keepdims=True)
    e = jnp.where(bias > MASKED, jnp.exp(s - m), 0.0)
    p = e / jnp.maximum(jnp.sum(e, axis=-1, keepdims=True), 1e-30)
    o = jnp.dot(p.astype(BF16), vc_ref[...], preferred_element_type=F32)
    psum = jnp.zeros((tq, N_CMP_PAD), F32)
    for g in range(NSA_GROUP):
        ocmp_ref[:, g * HEAD_DIM:(g + 1) * HEAD_DIM] = o[g * tq:(g + 1) * tq]
        psum += p[g * tq:(g + 1) * tq]
    qpos = q0 + qi * tq + lax.broadcasted_iota(jnp.int32, (tq, 1), 0)
    kpos = lax.broadcasted_iota(jnp.int32, (1, mask_ref.shape[1]), 1)
    imp = _dot3_lhs(psum, ovl_ref[...])
    mask_ref[...] = _selected_key_mask(imp, exp_ref[...], qpos, kpos, n_sel).astype(mask_ref.dtype)


def cmp_attention_topk(z16, kc, vc, bias_c, bsz, seq, q0, key_len, tq_pref=256):
    n = bsz * seq
    tq = _tile(seq, tq_pref)
    nq = seq // tq
    q_blk = COL_Q // (NSA_GROUP * HEAD_DIM)
    n_sel = -(-key_len // SEL_BLOCK)
    lp = -(-(n_sel * SEL_BLOCK) // LANE) * LANE
    ci = np.arange(N_CMP_PAD)[:, None] * CMP_STRIDE
    sj = np.arange(LANE)[None, :] * SEL_BLOCK
    overlap = jnp.asarray(((ci < sj + SEL_BLOCK) & (ci + CMP_BLOCK > sj)).astype(np.float32), BF16)
    expand = jnp.asarray((np.arange(lp)[None, :] // SEL_BLOCK == np.arange(LANE)[:, None]).astype(np.float32), BF16)
    mask_dtype = BF16 if tq % 16 == 0 else F32
    return pl.pallas_call(
        functools.partial(_cmp_topk_kernel, q0, n_sel),
        grid=(bsz, NSA_KV_HEADS, nq),
        in_specs=[
            pl.BlockSpec((tq, NSA_GROUP * HEAD_DIM), lambda b, k, i: (b * nq + i, q_blk + k)),
            pl.BlockSpec((None, None, N_CMP_PAD, HEAD_DIM), lambda b, k, i: (b, k, 0, 0)),
            pl.BlockSpec((None, None, N_CMP_PAD, HEAD_DIM), lambda b, k, i: (b, k, 0, 0)),
            pl.BlockSpec((NSA_GROUP, tq, N_CMP_PAD), lambda b, k, i: (k, i, 0)),
            pl.BlockSpec((N_CMP_PAD, LANE), lambda b, k, i: (0, 0)),
            pl.BlockSpec((LANE, lp), lambda b, k, i: (0, 0)),
        ],
        out_specs=[
            pl.BlockSpec((tq, NSA_GROUP * HEAD_DIM), lambda b, k, i: (b * nq + i, k)),
            pl.BlockSpec((None, None, tq, lp), lambda b, k, i: (b, k, i, 0)),
        ],
        out_shape=[jax.ShapeDtypeStruct((n, NSA_HEADS * HEAD_DIM), F32),
                   jax.ShapeDtypeStruct((bsz, NSA_KV_HEADS, seq, lp), mask_dtype)],
        compiler_params=_cparams("parallel", "parallel", "parallel"),
        name="cmp_attention_topk",
    )(z16, kc, vc, bias_c, overlap, expand)


def _softmax_step(q4, k, v, bias, carry):
    m, l, acc = carry
    s = _nt_dot(q4, k) + bias
    m_new = jnp.maximum(m, jnp.max(s, axis=-1, keepdims=True))
    alpha = jnp.exp(m - m_new)
    pr = jnp.where(bias > MASKED, jnp.exp(s - m_new), 0.0)
    l = alpha * l + jnp.sum(pr, axis=-1, keepdims=True)
    acc = alpha * acc + jnp.dot(pr.astype(BF16), v, preferred_element_type=F32)
    return m_new, l, acc


def _nsa_prompt_kernel(q_ref, ks_ref, vs_ref, kw_ref, vw_ref, bias_ref, mask_ref, ocmp_ref, gate_ref, o_ref):
    qi = pl.program_id(2)
    tq = q_ref.shape[0]
    ck = 2 * tq
    rows = NSA_GROUP * tq
    q4 = _heads_to_rows(q_ref)
    n_win = WINDOW // tq

    def bias_pair(kc, far):
        tiles = []
        for half in range(2):
            delta = qi - 2 * kc - half
            if far is not None:
                delta = jnp.minimum(delta, far)
            off = pl.multiple_of((delta + 1) * tq, tq)
            tiles.append(bias_ref[:, pl.ds(off, tq), :].reshape(rows, tq))
        return jnp.concatenate(tiles, axis=1)

    def kv_chunk(k_ref, v_ref, kc):
        off = pl.multiple_of(kc * ck, ck)
        return k_ref[pl.ds(off, ck), :], v_ref[pl.ds(off, ck), :]

    init = (jnp.full((rows, 1), NEG_INF, F32), jnp.zeros((rows, 1), F32), jnp.zeros((rows, HEAD_DIM), F32))

    def sel_body(kc, carry):
        off = pl.multiple_of(kc * ck, ck)
        mb = mask_ref[:, pl.ds(off, ck)].astype(F32)
        bias = bias_pair(kc, 2) + jnp.concatenate([mb] * NSA_GROUP, axis=0)
        k, v = kv_chunk(ks_ref, vs_ref, kc)
        return _softmax_step(q4, k, v, bias, carry)

    _, l, acc = lax.fori_loop(0, qi // 2 + 1, sel_body, init)
    o_sel = acc / l

    def win_body(kc, carry):
        k, v = kv_chunk(kw_ref, vw_ref, kc)
        return _softmax_step(q4, k, v, bias_pair(kc, None), carry)

    _, l, acc = lax.fori_loop(jnp.maximum(qi - n_win, 0) // 2, qi // 2 + 1, win_body, init)
    o_win = acc / l

    gates = gate_ref[...]
    for g in range(NSA_GROUP):
        sl = slice(g * HEAD_DIM, (g + 1) * HEAD_DIM)
        rs = slice(g * tq, (g + 1) * tq)
        o = (gates[:, 3 * g:3 * g + 1] * ocmp_ref[:, sl] + gates[:, 3 * g + 1:3 * g + 2] * o_sel[rs]
             + gates[:, 3 * g + 2:3 * g + 3] * o_win[rs])
        o_ref[:, sl] = o.astype(o_ref.dtype)


def nsa_prompt_attention(z16, z32, bias5, mask, o_cmp, bsz, seq):
    n = bsz * seq
    tq = PAGE
    nq = seq // tq
    grp = NSA_GROUP * HEAD_DIM
    seq_col = lambda col: pl.BlockSpec((seq, HEAD_DIM), lambda b, k, i: (b, col // HEAD_DIM + k))
    return pl.pallas_call(
        _nsa_prompt_kernel,
        grid=(bsz, NSA_KV_HEADS, nq),
        in_specs=[
            pl.BlockSpec((tq, grp), lambda b, k, i: (b * nq + i, COL_Q // grp + k)),
            seq_col(COL_KV + 2 * KV_LANES), seq_col(COL_KV + 3 * KV_LANES),
            seq_col(COL_WIN), seq_col(COL_WIN + KV_LANES),
            pl.BlockSpec((NSA_GROUP, bias5.shape[1], tq), lambda b, k, i: (k, 0, 0)),
            pl.BlockSpec((None, None, tq, seq), lambda b, k, i: (b, k, i, 0)),
            pl.BlockSpec((tq, grp), lambda b, k, i: (b * nq + i, k)),
            pl.BlockSpec((tq, LANE), lambda b, k, i: (b * nq + i, COL_GATE // LANE + k)),
        ],
        out_specs=pl.BlockSpec((tq, grp), lambda b, k, i: (b * nq + i, k)),
        out_shape=jax.ShapeDtypeStruct((n, NSA_HEADS * HEAD_DIM), BF16),
        compiler_params=_cparams("parallel", "parallel", "arbitrary"),
        name="nsa_prompt_attention",
    )(z16, z16, z16, z16, z16, bias5, mask, o_cmp, z32)


ROWS_PER_TOKEN = 4 * NSA_KV_HEADS


def _softmax_attend(q, k, v, bias):
    s = _nt_dot(q, k) + bias
    e = jnp.where(bias > MASKED, jnp.exp(s - jnp.max(s, axis=-1, keepdims=True)), 0.0)
    o = jnp.dot(e.astype(BF16), v, preferred_element_type=F32)
    return o / jnp.maximum(jnp.sum(e, axis=-1, keepdims=True), 1e-30), e


PAGES_PER_STEP = 2


def _low_half_mask():
    return lax.broadcasted_iota(jnp.int32, (SUBLANE, HEAD_DIM), 0) < NSA_KV_HEADS


def _split_token_pairs(toks):
    lo = _low_half_mask()
    a = [jnp.where(lo, toks[i], pltpu.roll(toks[i + 1], NSA_KV_HEADS, 0)) for i in (0, 2)]
    b = [jnp.where(lo, pltpu.roll(toks[i], NSA_KV_HEADS, 0), toks[i + 1]) for i in (0, 2)]
    return jnp.concatenate(a, axis=0).astype(BF16), jnp.concatenate(b, axis=0).astype(BF16)


def _nsa_sample_kernel(n_past, n_sel, pt_ref, q_ref, *refs):
    page_refs = refs[:PAGES_PER_STEP]
    (knew_ref, wst_ref, wnew_ref, bcmp_ref, bsel_ref, bwin_ref, kpos_ref, wck_ref, wcv_ref, posk_ref, posv_ref,
     w2k_ref, w2v_ref, kn_ref, ovl_ref, exp_ref, gate_ref, o_ref, xs_ref, ks_ref, vs_ref, kw_ref, vw_ref
     ) = refs[PAGES_PER_STEP:]
    p = pl.program_id(1)
    t = q_ref.shape[0]
    rpt = ROWS_PER_TOKEN
    kvh_n = NSA_KV_HEADS
    lo = _low_half_mask()
    for u, page_ref in enumerate(page_refs):
        pg = p * PAGES_PER_STEP + u
        row0 = pl.multiple_of(pg * (CMP_PER_PAGE * kvh_n), CMP_PER_PAGE * kvh_n)
        for r in range(CMP_STRIDE):
            tl = [page_ref[pl.ds((CMP_STRIDE * hb + r) * rpt, SUBLANE), :] for hb in range(CMP_PER_PAGE)]
            for w in range(CMP_PER_PAGE // 4):
                xk, xv = _split_token_pairs(tl[4 * w:4 * w + 4])
                xs_ref[0, pl.ds(row0 + 16 * w, 16), r * HEAD_DIM:(r + 1) * HEAD_DIM] = xk
                xs_ref[1, pl.ds(row0 + 16 * w, 16), r * HEAD_DIM:(r + 1) * HEAD_DIM] = xv
        k0 = pl.multiple_of(pg * (PAGE * kvh_n), PAGE * kvh_n)
        for w in range(PAGE // 4):
            tl = [page_ref[pl.ds((4 * w + i) * rpt + SUBLANE, SUBLANE), :] for i in range(4)]
            kk, vv = _split_token_pairs(tl)
            ks_ref[pl.ds(k0 + 16 * w, 16), :] = kk
            vs_ref[pl.ds(k0 + 16 * w, 16), :] = vv

    @pl.when(p == pl.num_programs(1) - 1)
    def _():
        def new_rows(x):
            rows = jnp.concatenate([x[:, k * HEAD_DIM:(k + 1) * HEAD_DIM] for k in range(kvh_n)], axis=0)
            return jnp.concatenate([rows, jnp.zeros((PAGE * kvh_n - rows.shape[0], HEAD_DIM), F32)], axis=0).astype(BF16)

        knew = knew_ref[...]
        ks_ref[kvh_n * n_past:kvh_n * (n_past + PAGE), :] = new_rows(knew[:, :KV_LANES])
        vs_ref[kvh_n * n_past:kvh_n * (n_past + PAGE), :] = new_rows(knew[:, KV_LANES:])

        q = q_ref[...]
        qall = jnp.concatenate([q[:, h * HEAD_DIM:(h + 1) * HEAD_DIM] for h in range(NSA_HEADS)], axis=0).astype(BF16)

        kc = _compress_finish(xs_ref[0], wck_ref[...], posk_ref[...], w2k_ref[...], kn_ref[...], kvh_minor=True)
        vc = _compress_finish(xs_ref[1], wcv_ref[...], posv_ref[...], w2v_ref[...], None, kvh_minor=True)
        o_cmp, e_c = _softmax_attend(qall, kc.astype(BF16), vc.astype(BF16), bcmp_ref[...])
        p_c = e_c / jnp.maximum(jnp.sum(e_c, axis=-1, keepdims=True), 1e-30)
        imp_h = _dot3_lhs(p_c, ovl_ref[...])
        imp = jnp.concatenate(
            [sum(imp_h[(k * NSA_GROUP + g) * t:(k * NSA_GROUP + g + 1) * t] for g in range(NSA_GROUP))
             for k in range(kvh_n)], axis=0)
        qpos = n_past + lax.broadcasted_iota(jnp.int32, (kvh_n * t, 1), 0) % t
        mb = _selected_key_mask(imp, exp_ref[...], qpos, kpos_ref[...], n_sel)
        mb = jnp.concatenate([mb[k * t:(k + 1) * t] for k in range(kvh_n) for _ in range(NSA_GROUP)], axis=0)

        o_sel, _ = _softmax_attend(qall, ks_ref[...], vs_ref[...], bsel_ref[...] + mb)

        n_st = wst_ref.shape[0] // (2 * kvh_n)

        def win_body(w, carry):
            base = pl.multiple_of(w * 32, 32)
            tl = [wst_ref[pl.ds(base + SUBLANE * i, SUBLANE), :] for i in range(4)]
            kk, vv = _split_token_pairs(tl)
            dst = pl.multiple_of(w * 16, 16)
            kw_ref[pl.ds(dst, 16), :] = kk
            vw_ref[pl.ds(dst, 16), :] = vv
            return carry

        lax.fori_loop(0, n_st // 4, win_body, 0, unroll=4)
        wnew = wnew_ref[...]
        kw_ref[kvh_n * n_st:kvh_n * (n_st + PAGE), :] = new_rows(wnew[:, :KV_LANES])
        vw_ref[kvh_n * n_st:kvh_n * (n_st + PAGE), :] = new_rows(wnew[:, KV_LANES:])
        o_win, _ = _softmax_attend(qall, kw_ref[...], vw_ref[...], bwin_ref[...])

        gates = gate_ref[...]
        for h in range(NSA_HEADS):
            kvh, g = divmod(h, NSA_GROUP)
            rs = slice(h * t, (h + 1) * t)
            gc = kvh * LANE + 3 * g
            o_ref[:, h * HEAD_DIM:(h + 1) * HEAD_DIM] = (
                gates[:, gc:gc + 1] * o_cmp[rs] + gates[:, gc + 1:gc + 2] * o_sel[rs] + gates[:, gc + 2:gc + 3] * o_win[rs])


def _sample_column_tables(bias_cmp, bias_sel, bias_win, seq, n_past, n_win):
    kvh_n = NSA_KV_HEADS
    row_kvh = np.arange(NSA_HEADS * seq) // (NSA_GROUP * seq)

    def columns(n_old):
        c = np.arange((n_old + PAGE) * kvh_n)
        old = c < n_old * kvh_n
        cn = c - n_old * kvh_n
        valid = old | (cn < kvh_n * seq)
        pos = np.where(old, c // kvh_n, n_old + cn % seq)
        kvh = np.where(old, c % kvh_n, cn // seq)
        return np.where(valid, pos, 0), kvh, valid

    def widen(bias, pos, kvh, valid):
        ok = valid[None, :] & (kvh[None, :] == row_kvh[:, None])
        return jnp.where(jnp.asarray(ok), jnp.take(bias, jnp.asarray(pos), axis=1), NEG_INF)

    cc = np.arange(N_CMP_PAD * kvh_n)
    b_cmp = widen(bias_cmp, cc // kvh_n, cc % kvh_n, np.ones_like(cc, bool))
    pos_s, kvh_s, valid_s = columns(n_past)
    b_sel = widen(bias_sel, pos_s, kvh_s, valid_s)
    pos_w, kvh_w, valid_w = columns(n_win)
    b_win = widen(bias_win, pos_w, kvh_w, valid_w)
    kpos = jnp.asarray(np.where(valid_s, pos_s, np.iinfo(np.int32).max)[None, :].astype(np.int32))
    ci = np.arange(N_CMP_PAD)[:, None] * CMP_STRIDE
    sj = np.arange(LANE)[None, :] * SEL_BLOCK
    overlap = np.repeat(((ci < sj + SEL_BLOCK) & (ci + CMP_BLOCK > sj)).astype(np.float32), kvh_n, axis=0)
    expand = ((pos_s[None, :] // SEL_BLOCK == np.arange(LANE)[:, None]) & valid_s[None, :]).astype(np.float32)
    return b_cmp, b_sel, b_win, kpos, jnp.asarray(overlap, BF16), jnp.asarray(expand, BF16)


def nsa_sample_attention(z32, pages, page_table, win_state, bias_cmp, bias_sel, bias_win, cmp_w, seq):
    bsz, n_pages = page_table.shape
    n = bsz * seq
    width = NSA_HEADS * HEAD_DIM
    kvh_n = NSA_KV_HEADS
    n_past = n_pages * PAGE
    n_win = win_state.shape[1] // (2 * kvh_n)
    n_sel = -(-(n_past + seq) // SEL_BLOCK)
    assert n_pages % PAGES_PER_STEP == 0 and bias_sel.shape[1] == n_past + PAGE
    b_cmp, b_sel, b_win, kpos, overlap, expand = _sample_column_tables(bias_cmp, bias_sel, bias_win, seq, n_past, n_win)
    cst = lambda a: pl.BlockSpec(a.shape, lambda b, p, pt: (0,) * a.ndim, pipeline_mode=pl.Buffered(1))
    per_seq = lambda w, col: pl.BlockSpec((seq, w), lambda b, p, pt: (b, col // w))

    def page_map(u, b, p, pt):
        return (pt[b, p * PAGES_PER_STEP + u], 0, 0)

    consts = [b_cmp, b_sel, b_win, kpos] + list(cmp_w) + [overlap, expand]
    return pl.pallas_call(
        functools.partial(_nsa_sample_kernel, n_past, n_sel),
        grid_spec=pltpu.PrefetchScalarGridSpec(
            num_scalar_prefetch=1,
            grid=(bsz, n_pages // PAGES_PER_STEP),
            in_specs=[per_seq(width, COL_Q)]
                     + [pl.BlockSpec((None, PAGE * ROWS_PER_TOKEN, HEAD_DIM), functools.partial(page_map, u))
                        for u in range(PAGES_PER_STEP)]
                     + [per_seq(2 * KV_LANES, COL_KV + 2 * KV_LANES),
                        pl.BlockSpec((None, win_state.shape[1], HEAD_DIM), lambda b, p, pt: (b, 0, 0)),
                        per_seq(2 * KV_LANES, COL_WIN)]
                     + [cst(a) for a in consts] + [per_seq(kvh_n * LANE, COL_GATE)],
            out_specs=per_seq(width, 0),
            scratch_shapes=[pltpu.VMEM((2, kvh_n * N_CMP_PAD, CMP_STRIDE * HEAD_DIM), BF16),
                            pltpu.VMEM((kvh_n * (n_past + PAGE), HEAD_DIM), BF16),
                            pltpu.VMEM((kvh_n * (n_past + PAGE), HEAD_DIM), BF16),
                            pltpu.VMEM((kvh_n * (n_win + PAGE), HEAD_DIM), BF16),
                            pltpu.VMEM((kvh_n * (n_win + PAGE), HEAD_DIM), BF16)],
        ),
        out_shape=jax.ShapeDtypeStruct((n, width), F32),
        compiler_params=_cparams("parallel", "arbitrary"),
        name="nsa_sample_attention",
    )(page_table, z32, *([pages] * PAGES_PER_STEP), z32, win_state, z32, *consts, z32)


SCALE = HEAD_DIM ** -0.5


def _layer_weights(i, ffn1_w_gate, ffn1_w_up, ffn1_w_down, w_in, w_out, nsa_q_norm, nsa_k_norm,
                   ffn2_w_gate, ffn2_w_up, ffn2_w_down, ple_w_gate, ple_w_proj):
    wi = w_in[i]
    wgate = wi[:, COL_GATE:].reshape(-1, NSA_KV_HEADS, 3 * NSA_GROUP)
    wgate = jnp.pad(wgate, ((0, 0), (0, 0), (0, LANE - 3 * NSA_GROUP))).reshape(-1, NSA_KV_HEADS * LANE)
    w_mix = jnp.concatenate([wi[:, :COL_GATE], wgate], axis=1).astype(BF16)
    t_q, t_ks, t_kw = COL_Q // PROJ_TILE, (COL_KV + 2 * KV_LANES) // PROJ_TILE, COL_WIN // PROJ_TILE
    n_q = NSA_WIDTH // PROJ_TILE
    gains = jnp.ones((Z_COLS // PROJ_TILE, 1, HEAD_DIM), F32)
    gains = gains.at[t_q:t_q + n_q].set(nsa_q_norm[i].astype(F32) * SCALE)
    gains = gains.at[t_ks].set(nsa_k_norm[i, 1].astype(F32)).at[t_kw].set(nsa_k_norm[i, 2].astype(F32))
    return dict(
        ffn1=(ffn1_w_gate[i].astype(BF16), ffn1_w_up[i].astype(BF16), ffn1_w_down[i].astype(BF16)),
        ffn2=(ffn2_w_gate[i].astype(BF16), ffn2_w_up[i].astype(BF16), ffn2_w_down[i].astype(BF16)),
        w_mix=w_mix, mix_gains=gains, norm_tiles=tuple(range(t_q, t_q + n_q)) + (t_ks, t_kw),
        gate_tile=COL_GATE // PROJ_TILE,
        wo_h=w_out[i, :HGRN_WIDTH].astype(BF16), wo_n=w_out[i, HGRN_WIDTH:].astype(BF16),
        ple_gate=ple_w_gate[i].astype(BF16), ple_proj=ple_w_proj[i].astype(BF16),
    )


def _run_layer(i, w, x, pemb, bsz, seq, s0, nsa_fn, norms, hgrn_lb):
    ffn1_norm, mix_norm, hgrn_out_norm, ffn2_norm, ple_norm, ple_post_norm = norms
    x1 = ffn_residual(x, ffn1_norm[i], *w["ffn1"])
    z32, z16 = mixer_project(x1, mix_norm[i], w["w_mix"], w["mix_gains"], w["norm_tiles"], w["gate_tile"])
    o_h, s_fin = hgrn2_mix(z32, hgrn_lb, hgrn_out_norm[i], i, bsz, seq, HGRN_HEADS, s0=s0,
                           heads_per_step=HGRN_HEADS if seq % 16 else 4)
    o_n = nsa_fn(z32, z16)
    x2 = out_project_residual(x1, o_h, o_n, w["wo_h"], w["wo_n"])
    x3 = ffn_residual(x2, ffn2_norm[i], *w["ffn2"])
    y = ple_residual(x3, ple_norm[i], w["ple_gate"], pemb, w["ple_proj"], ple_post_norm[i])
    return y, z32[:, COL_KV:COL_WIN], z32[:, COL_WIN:COL_GATE], s_fin


def kernel(x_prompt, x_sample, cache_kv, state_win_kv, state_hgrn, page_table, p_prompt, p_sample, ffn1_norm, ffn1_w_gate, ffn1_w_up, ffn1_w_down, mix_norm, w_in, w_out, hgrn_lb, hgrn_out_norm, nsa_q_norm, nsa_k_norm, cmp_pos, cmp_w1, cmp_w2, rel_bias_table, ffn2_norm, ffn2_w_gate, ffn2_w_up, ffn2_w_down, ple_norm, ple_w_gate, ple_w_proj, ple_post_norm):
    depth = cache_kv.shape[0]
    bp, tp, d = x_prompt.shape
    bs, ts, _ = x_sample.shape
    n_pool = cache_kv.shape[1]
    n_pages = page_table.shape[1]
    past = n_pages * PAGE
    win_keep = state_win_kv.shape[2]
    assert tp % PAGE == 0 and tp >= WINDOW and win_keep == WINDOW
    norms = (ffn1_norm, mix_norm, hgrn_out_norm, ffn2_norm, ple_norm, ple_post_norm)
    table = rel_bias_table.astype(F32)

    n_cmp = (tp - CMP_BLOCK) // CMP_STRIDE + 1
    bias_pc = rel_bias(table, tp, N_CMP_PAD, 0, CMP_BLOCK - 1, CMP_STRIDE, n_cmp)
    bias_p5 = rel_bias(table, WINDOW + 3 * PAGE, PAGE, -PAGE, 0, 1, PAGE, window=WINDOW)
    n_cmp_s = (past + ts - CMP_BLOCK) // CMP_STRIDE + 1
    assert n_cmp_s <= N_CMP_PAD - 1 and (n_cmp_s - 1) * CMP_STRIDE + CMP_BLOCK <= past
    sel_cols = past + PAGE
    bias_sc = rel_bias(table, ts, N_CMP_PAD, past, CMP_BLOCK - 1, CMP_STRIDE, n_cmp_s).reshape(NSA_HEADS * ts, N_CMP_PAD)
    bias_ss = rel_bias(table, ts, sel_cols, past, 0, 1, past + ts).reshape(NSA_HEADS * ts, sel_cols)
    bias_sw = rel_bias(table, ts, win_keep + PAGE, win_keep, 0, 1, win_keep + ts, window=WINDOW)
    bias_sw = bias_sw.reshape(NSA_HEADS * ts, win_keep + PAGE)

    xp = x_prompt.reshape(bp * tp, d)
    xs = x_sample.reshape(bs * ts, d)
    outs = [[] for _ in range(6)]
    for i in range(depth):
        w = _layer_weights(i, ffn1_w_gate, ffn1_w_up, ffn1_w_down, w_in, w_out, nsa_q_norm, nsa_k_norm,
                           ffn2_w_gate, ffn2_w_up, ffn2_w_down, ple_w_gate, ple_w_proj)
        cmp_w = compress_weights(cmp_w1[i], cmp_w2[i], cmp_pos[i], nsa_k_norm[i, 0])

        def nsa_prompt(z32, z16):
            pages = z32.reshape(bp * tp // PAGE, PAGE, Z_COLS)
            pt = jnp.arange(bp * tp // PAGE, dtype=jnp.int32).reshape(bp, tp // PAGE)
            kc, vc = compress_cache(pages, pt, COL_KV // HEAD_DIM, cmp_w)
            o_cmp, mask = cmp_attention_topk(z16, kc, vc, bias_pc, bp, tp, 0, tp)
            return nsa_prompt_attention(z16, z32, bias_p5, mask, o_cmp, bp, tp)

        def nsa_sample(z32, z16):
            pages = cache_kv[i].reshape(n_pool, PAGE * ROWS_PER_TOKEN, HEAD_DIM)
            wst = state_win_kv[i].reshape(bs, win_keep * 2 * NSA_KV_HEADS, HEAD_DIM)
            return nsa_sample_attention(z32, pages, page_table, wst, bias_sc, bias_ss, bias_sw, cmp_w, ts)

        xp, kv_p, win_p, h_p = _run_layer(i, w, xp, p_prompt[i].reshape(bp * tp, -1), bp, tp, None, nsa_prompt, norms, hgrn_lb)
        xs, kv_s, win_s, h_s = _run_layer(i, w, xs, p_sample[i].reshape(bs * ts, -1), bs, ts, state_hgrn[i], nsa_sample, norms, hgrn_lb)
        outs[0].append(kv_p.reshape(bp, tp, 4, NSA_KV_HEADS, HEAD_DIM))
        outs[1].append(win_p.reshape(bp, tp, 2, NSA_KV_HEADS, HEAD_DIM)[:, -WINDOW:])
        outs[2].append(h_p)
        outs[3].append(kv_s.reshape(bs, ts, 4, NSA_KV_HEADS, HEAD_DIM))
        win_new = win_s.reshape(bs, ts, 2, NSA_KV_HEADS, HEAD_DIM)
        outs[4].append(jnp.concatenate([state_win_kv[i], win_new], axis=1)[:, -win_keep:])
        outs[5].append(h_s.astype(state_hgrn.dtype))
    return (xp.reshape(bp, tp, d), xs.reshape(bs, ts, d)) + tuple(jnp.stack(o) for o in outs)
```

```python
import functools
import math

import numpy as np
import jax
import jax.numpy as jnp
from jax import lax
from jax.experimental import pallas as pl
from jax.experimental.pallas import tpu as pltpu

F32 = jnp.float32
BF16 = jnp.bfloat16

LANE = 128
SUBLANE = 8
VMEM_LIMIT_BYTES = 56 * 1024 * 1024

HEAD_DIM = 128
RMS_EPS = 1e-6
NEG_INF = -1e30


def _cparams(*sem):
    return pltpu.CompilerParams(dimension_semantics=sem, vmem_limit_bytes=VMEM_LIMIT_BYTES)


def _tile(n, pref):
    if n <= pref:
        return n
    t = pref
    while t >= SUBLANE:
        if n % t == 0:
            return t
        t -= SUBLANE
    return n


def _rms(x, gain):
    ms = jnp.mean(x * x, axis=-1, keepdims=True)
    return x * lax.rsqrt(ms + RMS_EPS) * gain


def _norm_rows_to(x_ref, gain_ref, dst_ref):
    rows = x_ref.shape[0]
    rc = 32 if rows % 32 == 0 else rows

    def body(i, carry):
        r = pl.multiple_of(i * rc, rc)
        dst_ref[pl.ds(r, rc), :] = _rms(x_ref[pl.ds(r, rc), :], gain_ref[...]).astype(dst_ref.dtype)
        return carry

    lax.fori_loop(0, rows // rc, body, 0)


def _ffn_kernel(x_ref, gain_ref, wg_ref, wu_ref, wd_ref, o_ref, hn_ref):
    @pl.when(pl.program_id(1) == 0)
    def _():
        _norm_rows_to(x_ref, gain_ref, hn_ref)
        o_ref[...] = x_ref[...]

    h = hn_ref[...]
    g = jnp.dot(h, wg_ref[...], preferred_element_type=F32)
    u = jnp.dot(h, wu_ref[...], preferred_element_type=F32)
    a = (g * jax.nn.sigmoid(g) * (0.5 * u)).astype(BF16)
    o_ref[...] += jnp.dot(a, wd_ref[...], preferred_element_type=F32)


def ffn_residual(x, gain, wg, wu, wd, tm_pref=512, tf_pref=256):
    n, d = x.shape
    f = wg.shape[1]
    tm, tf = _tile(n, tm_pref), _tile(f, tf_pref)
    return pl.pallas_call(
        _ffn_kernel,
        grid=(n // tm, f // tf),
        in_specs=[
            pl.BlockSpec((tm, d), lambda i, j: (i, 0)),
            pl.BlockSpec((1, d), lambda i, j: (0, 0)),
            pl.BlockSpec((d, tf), lambda i, j: (0, j)),
            pl.BlockSpec((d, tf), lambda i, j: (0, j)),
            pl.BlockSpec((tf, d), lambda i, j: (j, 0)),
        ],
        out_specs=pl.BlockSpec((tm, d), lambda i, j: (i, 0)),
        out_shape=jax.ShapeDtypeStruct((n, d), F32),
        scratch_shapes=[pltpu.VMEM((tm, d), BF16)],
        compiler_params=_cparams("parallel", "arbitrary"),
        name="ffn_residual",
    )(x, gain.reshape(1, d), wg, wu, wd)


PROJ_TILE = 512


def _mixproj_kernel(norm_tiles, gate_tile, x_ref, gain_ref, w_ref, hg_ref, z32_ref, z16_ref, hn_ref):
    j = pl.program_id(1)

    @pl.when(j == 0)
    def _():
        _norm_rows_to(x_ref, gain_ref, hn_ref)

    acc = jnp.dot(hn_ref[...], w_ref[...], preferred_element_type=F32)
    is_norm = functools.reduce(jnp.logical_or, [j == t for t in norm_tiles])
    is_gate = j == gate_tile

    def put(val, sl=slice(None)):
        z32_ref[:, sl] = val
        z16_ref[:, sl] = val.astype(BF16)

    @pl.when(is_norm)
    def _():
        gain = hg_ref[0]
        for h in range(acc.shape[1] // HEAD_DIM):
            sl = slice(h * HEAD_DIM, (h + 1) * HEAD_DIM)
            put(_rms(acc[:, sl], gain), sl)

    @pl.when(is_gate)
    def _():
        put(jax.nn.sigmoid(acc))

    @pl.when(jnp.logical_not(jnp.logical_or(is_norm, is_gate)))
    def _():
        put(acc)


def mixer_project(x, gain, w, head_gain, norm_tiles, gate_tile, tm_pref=512):
    n, d = x.shape
    ncols = w.shape[1]
    tm, tn = _tile(n, tm_pref), PROJ_TILE
    return pl.pallas_call(
        functools.partial(_mixproj_kernel, norm_tiles, gate_tile),
        grid=(n // tm, ncols // tn),
        in_specs=[
            pl.BlockSpec((tm, d), lambda i, j: (i, 0)),
            pl.BlockSpec((1, d), lambda i, j: (0, 0)),
            pl.BlockSpec((d, tn), lambda i, j: (0, j)),
            pl.BlockSpec((1, 1, HEAD_DIM), lambda i, j: (j, 0, 0)),
        ],
        out_specs=[pl.BlockSpec((tm, tn), lambda i, j: (i, j))] * 2,
        out_shape=[jax.ShapeDtypeStruct((n, ncols), F32), jax.ShapeDtypeStruct((n, ncols), BF16)],
        scratch_shapes=[pltpu.VMEM((tm, d), BF16)],
        compiler_params=_cparams("parallel", "arbitrary"),
        name="mixer_project",
    )(x, gain.reshape(1, d), w, head_gain)


def _outproj_kernel(x_ref, a_ref, b_ref, wa_ref, wb_ref, o_ref):
    acc = jnp.dot(a_ref[...].astype(BF16), wa_ref[...], preferred_element_type=F32)
    acc += jnp.dot(b_ref[...].astype(BF16), wb_ref[...], preferred_element_type=F32)
    o_ref[...] = x_ref[...] + acc


def out_project_residual(x, a, b, wa, wb, tm_pref=512, tn_pref=1024):
    n, d = x.shape
    ka, kb = a.shape[1], b.shape[1]
    tm, tn = _tile(n, tm_pref), _tile(d, tn_pref)
    return pl.pallas_call(
        _outproj_kernel,
        grid=(n // tm, d // tn),
        in_specs=[
            pl.BlockSpec((tm, tn), lambda i, j: (i, j)),
            pl.BlockSpec((tm, ka), lambda i, j: (i, 0)),
            pl.BlockSpec((tm, kb), lambda i, j: (i, 0)),
            pl.BlockSpec((ka, tn), lambda i, j: (0, j)),
            pl.BlockSpec((kb, tn), lambda i, j: (0, j)),
        ],
        out_specs=pl.BlockSpec((tm, tn), lambda i, j: (i, j)),
        out_shape=jax.ShapeDtypeStruct((n, d), F32),
        compiler_params=_cparams("parallel", "arbitrary"),
        name="out_project_residual",
    )(x, a, b, wa, wb)


def _ple_kernel(tn, x_ref, gain_ref, wg_ref, p_ref, wp_ref, pg_ref, o_ref, hn_ref, pe_ref):
    j = pl.program_id(1)

    @pl.when(j == 0)
    def _():
        _norm_rows_to(x_ref, gain_ref, hn_ref)
        pe_ref[...] = jnp.dot(p_ref[...].astype(BF16), wp_ref[...], preferred_element_type=F32)
        _norm_rows_to(pe_ref, pg_ref, pe_ref)

    c = pl.multiple_of(j * tn, LANE)
    gate = jax.nn.sigmoid(jnp.dot(hn_ref[...], wg_ref[...], preferred_element_type=F32))
    o_ref[...] = x_ref[:, pl.ds(c, tn)] + gate * pe_ref[:, pl.ds(c, tn)]


def ple_residual(x, gain, wg, p, wp, post_gain, tm_pref=512, tn_pref=512):
    n, d = x.shape
    pd = p.shape[1]
    tm, tn = _tile(n, tm_pref), _tile(d, tn_pref)
    return pl.pallas_call(
        functools.partial(_ple_kernel, tn),
        grid=(n // tm, d // tn),
        in_specs=[
            pl.BlockSpec((tm, d), lambda i, j: (i, 0)),
            pl.BlockSpec((1, d), lambda i, j: (0, 0)),
            pl.BlockSpec((d, tn), lambda i, j: (0, j)),
            pl.BlockSpec((tm, pd), lambda i, j: (i, 0)),
            pl.BlockSpec((pd, d), lambda i, j: (0, 0)),
            pl.BlockSpec((1, d), lambda i, j: (0, 0)),
        ],
        out_specs=pl.BlockSpec((tm, tn), lambda i, j: (i, j)),
        out_shape=jax.ShapeDtypeStruct((n, d), F32),
        scratch_shapes=[pltpu.VMEM((tm, d), BF16), pltpu.VMEM((tm, d), F32)],
        compiler_params=_cparams("parallel", "arbitrary"),
        name="ple_residual",
    )(x, gain.reshape(1, d), wg, p, wp, post_gain.reshape(1, d))


HGRN_DIAG = SUBLANE


def _split3_dot(lhs_bf16, x):
    hi = x.astype(BF16)
    r1 = x - hi.astype(F32)
    mid = r1.astype(BF16)
    lo = (r1 - mid.astype(F32)).astype(BF16)
    acc = jnp.dot(lhs_bf16, hi, preferred_element_type=F32)
    acc += jnp.dot(lhs_bf16, mid, preferred_element_type=F32)
    acc += jnp.dot(lhs_bf16, lo, preferred_element_type=F32)
    return acc


def _bcast_row_in_blocks(a, bs, row):
    c, k = a.shape
    a3 = a.reshape(c // bs, bs, k)
    return jnp.broadcast_to(a3[:, row:row + 1, :], (c // bs, bs, k)).reshape(c, k)


def _nt_dot(a, b):
    return lax.dot_general(a, b, (((1,), (1,)), ((), ())), preferred_element_type=F32)


def _hgrn_kernel(layer, has_s0, hps, q_ref, f_ref, i_ref, g_ref, lb_ref, og_ref, *rest):
    if has_s0:
        s0_ref, o_ref, sfin_ref, st_ref = rest
    else:
        s0_ref = None
        o_ref, sfin_ref, st_ref = rest
    ci = pl.program_id(2)

    @pl.when(ci == 0)
    def _():
        if s0_ref is not None:
            st_ref[...] = jnp.stack([s0_ref[hh].T for hh in range(hps)])
        else:
            st_ref[...] = jnp.zeros(st_ref.shape, F32)

    outs, states = [], []
    for hh in range(hps):
        ls = slice(hh * HEAD_DIM, (hh + 1) * HEAD_DIM)
        o, st_new = _hgrn_head(layer, q_ref[:, ls], f_ref[:, ls], i_ref[:, ls], g_ref[:, ls], lb_ref[:, ls],
                               og_ref[...], st_ref[hh])
        outs.append(o)
        states.append(st_new)
    o_ref[...] = jnp.concatenate(outs, axis=1).astype(o_ref.dtype)
    st_ref[...] = jnp.stack(states)

    @pl.when(ci == pl.num_programs(2) - 1)
    def _():
        sfin_ref[...] = jnp.stack([s.T for s in states])


def _hgrn_head(layer, q, f_raw, v, gr, lbr, out_gain, st):
    c = q.shape[0]
    e = jnp.exp(lbr - jnp.max(lbr, axis=0, keepdims=True))
    lb = jnp.sum(e[:layer + 1], axis=0, keepdims=True) / jnp.sum(e, axis=0, keepdims=True)

    fg = lb + (1.0 - lb) * jax.nn.sigmoid(f_raw)
    logf = jnp.log(fg)
    kk = 1.0 - fg

    ti = lax.broadcasted_iota(jnp.int32, (c, c), 0)
    si = lax.broadcasted_iota(jnp.int32, (c, c), 1)
    tril = (si <= ti)
    a = _split3_dot(tril.astype(BF16), logf)
    a_last = a[c - 1:c, :]

    kk_b = kk.astype(BF16)
    d = HGRN_DIAG
    xs = [(q * jnp.exp(jnp.minimum(a - _bcast_row_in_blocks(a, d, j), 0.0))).astype(BF16) for j in range(d)]
    res = _nt_dot(jnp.concatenate(xs, axis=0), kk_b)
    attn = jnp.zeros((c, c), F32)
    for j in range(d):
        attn += jnp.where((si % d) == j, res[j * c:(j + 1) * c], 0.0)
    attn = jnp.where(((si // d) == (ti // d)) & tril, attn, 0.0)
    bs = 2 * d
    while bs <= c:
        half = bs // 2
        bnd = _bcast_row_in_blocks(a, bs, half - 1)
        qe = (q * jnp.exp(jnp.minimum(a - bnd, 0.0))).astype(BF16)
        ke = (kk * jnp.exp(jnp.minimum(bnd - a, 0.0))).astype(BF16)
        m = ((si // bs) == (ti // bs)) & ((ti % bs) >= half) & ((si % bs) < half)
        attn += jnp.where(m, _nt_dot(qe, ke), 0.0)
        bs *= 2

    v_b = v.astype(BF16)
    o = jnp.dot(attn.astype(BF16), v_b, preferred_element_type=F32)
    o += _nt_dot((q * jnp.exp(a)).astype(BF16), st.astype(BF16))
    kd = (kk * jnp.exp(a_last - a)).astype(BF16)
    st_new = st * jnp.exp(a_last) + jnp.dot(v_b.T, kd, preferred_element_type=F32)
    return _rms(o, out_gain) * (gr * jax.nn.sigmoid(gr)), st_new


def hgrn2_mix(zh, hgrn_lb, out_gain, layer, batch, seq, heads, s0=None, chunk_pref=128, heads_per_step=4):
    n = batch * seq
    dk = HEAD_DIM
    c = _tile(seq, chunk_pref)
    assert c % HGRN_DIAG == 0 and (c // HGRN_DIAG) & (c // HGRN_DIAG - 1) == 0
    nc = seq // c
    nl = hgrn_lb.shape[0]
    hps = heads_per_step
    hg = heads // hps

    def zspec(sec):
        return pl.BlockSpec((c, hps * dk), lambda b, h, ci: (b * nc + ci, sec * hg + h))

    state_spec = pl.BlockSpec((None, hps, dk, dk), lambda b, h, ci: (b, h, 0, 0))
    in_specs = [zspec(0), zspec(1), zspec(2), zspec(3),
                pl.BlockSpec((nl, hps * dk), lambda b, h, ci: (0, h)),
                pl.BlockSpec((1, dk), lambda b, h, ci: (0, 0))]
    args = [zh, zh, zh, zh, hgrn_lb, out_gain.reshape(1, dk)]
    if s0 is not None:
        in_specs.append(state_spec)
        args.append(s0)
    return pl.pallas_call(
        functools.partial(_hgrn_kernel, layer, s0 is not None, hps),
        grid=(batch, hg, nc),
        in_specs=in_specs,
        out_specs=[pl.BlockSpec((c, hps * dk), lambda b, h, ci: (b * nc + ci, h)), state_spec],
        out_shape=[jax.ShapeDtypeStruct((n, heads * dk), BF16 if c % 16 == 0 else F32),
                   jax.ShapeDtypeStruct((batch, heads, dk, dk), F32)],
        scratch_shapes=[pltpu.VMEM((hps, dk, dk), F32)],
        compiler_params=_cparams("parallel", "parallel", "arbitrary"),
        name="hgrn2_mix",
    )(*args)


NSA_KV_HEADS = 4
NSA_GROUP = 4
NSA_HEADS = NSA_KV_HEADS * NSA_GROUP
KV_LANES = NSA_KV_HEADS * HEAD_DIM
PAGE = 128
CMP_BLOCK = 32
CMP_STRIDE = 16
CMP_PER_PAGE = PAGE // CMP_STRIDE
N_CMP_PAD = 128
SEL_BLOCK = 64
SEL_TOPK = 16
SEL_LOCAL = 2
SEL_FORCE = 1e3
SEL_INVALID = -1e9
WINDOW = 512
N_BUCKETS = 32
MAX_DISTANCE = 128
MASKED = -1e29

HGRN_HEADS = 16
HGRN_WIDTH = HGRN_HEADS * HEAD_DIM
NSA_WIDTH = NSA_HEADS * HEAD_DIM
COL_Q = 4 * HGRN_WIDTH
COL_KV = COL_Q + NSA_WIDTH
COL_WIN = COL_KV + 4 * KV_LANES
COL_GATE = COL_WIN + 2 * KV_LANES
Z_COLS = COL_GATE + NSA_KV_HEADS * LANE


def _bias_kernel(q0, k0, kstride, ncols_valid, window, table_ref, o_ref):
    h = pl.program_id(0)
    rt, cols = o_ref.shape
    r = lax.broadcasted_iota(jnp.int32, (rt, cols), 0) + pl.program_id(1) * rt
    c = lax.broadcasted_iota(jnp.int32, (rt, cols), 1)
    dist = (q0 + r) - (k0 + c * kstride)
    dpos = jnp.maximum(dist, 0)
    max_exact = N_BUCKETS // 2
    log_ratio = jnp.log(jnp.maximum(dpos, 1).astype(F32) / max_exact) / math.log(MAX_DISTANCE / max_exact)
    large = jnp.minimum(max_exact + (log_ratio * (N_BUCKETS - max_exact)).astype(jnp.int32), N_BUCKETS - 1)
    bucket = jnp.where(dpos < max_exact, dpos, large)
    acc = jnp.zeros((rt, cols), F32)
    for b in range(N_BUCKETS):
        acc = jnp.where(bucket == b, table_ref[b, h], acc)
    valid = (dist >= 0) & (c < ncols_valid)
    if window is not None:
        valid = valid & (dist < window)
    o_ref[...] = jnp.where(valid, acc, NEG_INF)


def rel_bias(table, rows, cols, q0, k0, kstride, ncols_valid, window=None):
    rt = _tile(rows, 256)
    return pl.pallas_call(
        functools.partial(_bias_kernel, q0, k0, kstride, ncols_valid, window),
        grid=(NSA_HEADS, rows // rt),
        in_specs=[pl.BlockSpec(memory_space=pltpu.SMEM)],
        out_specs=pl.BlockSpec((None, rt, cols), lambda h, i: (h, i, 0)),
        out_shape=jax.ShapeDtypeStruct((NSA_HEADS, rows, cols), F32),
        compiler_params=_cparams("parallel", "parallel"),
        name="rel_bias",
    )(table)


def _compress_kernel(pt_ref, *refs):
    n_in = 2 * NSA_KV_HEADS
    page_refs = refs[:n_in]
    wck_ref, wcv_ref, posk_ref, posv_ref, w2k_ref, w2v_ref, kn_ref, kc_ref, vc_ref, xs_ref = refs[n_in:]
    p = pl.program_id(1)
    for slot in range(2):
        for kvh in range(NSA_KV_HEADS):
            page_ref = page_refs[slot * NSA_KV_HEADS + kvh]
            row0 = pl.multiple_of(kvh * N_CMP_PAD + p * CMP_PER_PAGE, CMP_PER_PAGE)
            for r in range(CMP_STRIDE):
                xs_ref[slot, pl.ds(row0, CMP_PER_PAGE), r * HEAD_DIM:(r + 1) * HEAD_DIM] = (
                    page_ref[pl.ds(r, CMP_PER_PAGE, stride=CMP_STRIDE), :])

    @pl.when(p == pl.num_programs(1) - 1)
    def _():
        kc = _compress_finish(xs_ref[0].astype(BF16), wck_ref[...], posk_ref[...], w2k_ref[...], kn_ref[...])
        vc = _compress_finish(xs_ref[1].astype(BF16), wcv_ref[...], posv_ref[...], w2v_ref[...], None)
        kc_ref[...] = kc.reshape(NSA_KV_HEADS, N_CMP_PAD, HEAD_DIM).astype(kc_ref.dtype)
        vc_ref[...] = vc.reshape(NSA_KV_HEADS, N_CMP_PAD, HEAD_DIM).astype(vc_ref.dtype)


def _compress_finish(x, wcat, posb, w2, k_gain, kvh_minor=False):
    m_rows = x.shape[0]
    hid = w2.shape[0]
    step = NSA_KV_HEADS if kvh_minor else 1
    pq = jnp.dot(x, wcat, preferred_element_type=F32)
    nxt = pltpu.roll(pq[:, hid:], m_rows - step, 0)
    hcur = pq[:, :hid] + nxt + posb
    act = (hcur * jax.nn.sigmoid(hcur)).astype(BF16)
    out = jnp.dot(act, w2, preferred_element_type=F32)
    if k_gain is not None:
        out = _rms(out, k_gain)
    rowid = lax.broadcasted_iota(jnp.int32, out.shape, 0)
    blk = rowid // step if kvh_minor else rowid % N_CMP_PAD
    return jnp.where(blk == N_CMP_PAD - 1, 0.0, out)


def _posb_kernel(pos_ref, w1_ref, o_ref):
    o_ref[...] = jnp.dot(pos_ref[...].astype(BF16), w1_ref[...], preferred_element_type=F32)


def compress_weights(w1, w2, pos, k_norm_cmp):
    hid = w2.shape[1]
    out = []
    for s in range(2):
        wcat = w1[s].reshape(2, CMP_STRIDE, HEAD_DIM, hid).transpose(1, 2, 0, 3).reshape(CMP_STRIDE * HEAD_DIM, 2 * hid)
        posf = jnp.broadcast_to(pos[s].reshape(1, CMP_BLOCK * HEAD_DIM), (SUBLANE, CMP_BLOCK * HEAD_DIM))
        posb = pl.pallas_call(_posb_kernel, out_shape=jax.ShapeDtypeStruct((SUBLANE, hid), F32),
                              name="cmp_pos_bias")(posf, w1[s].astype(BF16))[0:1]
        out.append((wcat.astype(BF16), posb, w2[s].astype(BF16)))
    (wck, posk, w2k), (wcv, posv, w2v) = out
    return [wck, wcv, posk, posv, w2k, w2v, k_norm_cmp.reshape(1, HEAD_DIM)]


def compress_cache(pages, page_table, lane_block, cmp_w):
    bsz, n_pages = page_table.shape
    assert n_pages * CMP_PER_PAGE == N_CMP_PAD
    full = lambda a: pl.BlockSpec(a.shape, lambda b, p, pt: (0,) * a.ndim)
    n_in = 2 * NSA_KV_HEADS

    def page_map(lane_blk, b, p, pt):
        return (pt[b, p], 0, lane_blk)

    ins = cmp_w
    out_spec = pl.BlockSpec((None, NSA_KV_HEADS, N_CMP_PAD, HEAD_DIM), lambda b, p, pt: (b, 0, 0, 0))
    out_sds = jax.ShapeDtypeStruct((bsz, NSA_KV_HEADS, N_CMP_PAD, HEAD_DIM), BF16)
    return pl.pallas_call(
        _compress_kernel,
        grid_spec=pltpu.PrefetchScalarGridSpec(
            num_scalar_prefetch=1,
            grid=(bsz, n_pages),
            in_specs=[pl.BlockSpec((None, PAGE, HEAD_DIM), functools.partial(page_map, lane_block + j))
                      for j in range(n_in)] + [full(a) for a in ins],
            out_specs=[out_spec, out_spec],
            scratch_shapes=[pltpu.VMEM((2, NSA_KV_HEADS * N_CMP_PAD, CMP_STRIDE * HEAD_DIM), F32)],
        ),
        out_shape=[out_sds, out_sds],
        compiler_params=_cparams("parallel", "arbitrary"),
        name="compress_cache",
    )(page_table, *([pages] * n_in), *ins)


def _heads_to_rows(q_ref):
    return jnp.concatenate([q_ref[:, g * HEAD_DIM:(g + 1) * HEAD_DIM] for g in range(NSA_GROUP)], axis=0).astype(BF16)


def _dot3_lhs(x, rhs_bf16):
    hi = x.astype(BF16)
    r1 = x - hi.astype(F32)
    mid = r1.astype(BF16)
    lo = (r1 - mid.astype(F32)).astype(BF16)
    acc = jnp.dot(hi, rhs_bf16, preferred_element_type=F32)
    acc += jnp.dot(mid, rhs_bf16, preferred_element_type=F32)
    acc += jnp.dot(lo, rhs_bf16, preferred_element_type=F32)
    return acc


def _selected_key_mask(imp, expand, qpos, kpos, n_sel):
    r = imp.shape[0]
    lane = lax.broadcasted_iota(jnp.int32, (r, LANE), 1)
    lag = qpos // SEL_BLOCK - lane
    forced = (lane == 0) | ((lag >= 0) & (lag < SEL_LOCAL))
    score = jnp.where(lag >= 0, imp + jnp.where(forced, SEL_FORCE, 0.0), SEL_INVALID)
    cnt = jnp.zeros((r, LANE), jnp.int32)
    for j in range(n_sel):
        col = score[:, j:j + 1]
        cnt += jnp.where(col > score, 1, jnp.where(col == score, jnp.where(lane > j, 1, 0), 0))
    sel = jnp.where(cnt < min(SEL_TOPK, n_sel), jnp.where(lane < n_sel, 1.0, 0.0), 0.0)
    ex = jnp.dot(sel.astype(BF16), expand, preferred_element_type=F32)
    keep = jnp.where(kpos <= qpos, ex, 0.0) > 0.5
    return jnp.where(keep, 0.0, NEG_INF)


def _cmp_topk_kernel(q0, n_sel, q_ref, kc_ref, vc_ref, bias_ref, ovl_ref, exp_ref, ocmp_ref, mask_ref):
    qi = pl.program_id(2)
    tq = q_ref.shape[0]
    q4 = _heads_to_rows(q_ref)
    bias = bias_ref[...].reshape(NSA_GROUP * tq, N_CMP_PAD)
    s = _nt_dot(q4, kc_ref[...]) + bias
    m = jnp.max(s, axis=-1, keepdims=True)
    e = jnp.where(bias > MASKED, jnp.exp(s - m), 0.0)
    p = e / jnp.maximum(jnp.sum(e, axis=-1, keepdims=True), 1e-30)
    o = jnp.dot(p.astype(BF16), vc_ref[...], preferred_element_type=F32)
    psum = jnp.zeros((tq, N_CMP_PAD), F32)
    for g in range(NSA_GROUP):
        ocmp_ref[:, g * HEAD_DIM:(g + 1) * HEAD_DIM] = o[g * tq:(g + 1) * tq]
        psum += p[g * tq:(g + 1) * tq]
    qpos = q0 + qi * tq + lax.broadcasted_iota(jnp.int32, (tq, 1), 0)
    kpos = lax.broadcasted_iota(jnp.int32, (1, mask_ref.shape[1]), 1)
    imp = _dot3_lhs(psum, ovl_ref[...])
    mask_ref[...] = _selected_key_mask(imp, exp_ref[...], qpos, kpos, n_sel).astype(mask_ref.dtype)


def cmp_attention_topk(z16, kc, vc, bias_c, bsz, seq, q0, key_len, tq_pref=256):
    n = bsz * seq
    tq = _tile(seq, tq_pref)
    nq = seq // tq
    q_blk = COL_Q // (NSA_GROUP * HEAD_DIM)
    n_sel = -(-key_len // SEL_BLOCK)
    lp = -(-(n_sel * SEL_BLOCK) // LANE) * LANE
    ci = np.arange(N_CMP_PAD)[:, None] * CMP_STRIDE
    sj = np.arange(LANE)[None, :] * SEL_BLOCK
    overlap = jnp.asarray(((ci < sj + SEL_BLOCK) & (ci + CMP_BLOCK > sj)).astype(np.float32), BF16)
    expand = jnp.asarray((np.arange(lp)[None, :] // SEL_BLOCK == np.arange(LANE)[:, None]).astype(np.float32), BF16)
    mask_dtype = BF16 if tq % 16 == 0 else F32
    return pl.pallas_call(
        functools.partial(_cmp_topk_kernel, q0, n_sel),
        grid=(bsz, NSA_KV_HEADS, nq),
        in_specs=[
            pl.BlockSpec((tq, NSA_GROUP * HEAD_DIM), lambda b, k, i: (b * nq + i, q_blk + k)),
            pl.BlockSpec((None, None, N_CMP_PAD, HEAD_DIM), lambda b, k, i: (b, k, 0, 0)),
            pl.BlockSpec((None, None, N_CMP_PAD, HEAD_DIM), lambda b, k, i: (b, k, 0, 0)),
            pl.BlockSpec((NSA_GROUP, tq, N_CMP_PAD), lambda b, k, i: (k, i, 0)),
            pl.BlockSpec((N_CMP_PAD, LANE), lambda b, k, i: (0, 0)),
            pl.BlockSpec((LANE, lp), lambda b, k, i: (0, 0)),
        ],
        out_specs=[
            pl.BlockSpec((tq, NSA_GROUP * HEAD_DIM), lambda b, k, i: (b * nq + i, k)),
            pl.BlockSpec((None, None, tq, lp), lambda b, k, i: (b, k, i, 0)),
        ],
        out_shape=[jax.ShapeDtypeStruct((n, NSA_HEADS * HEAD_DIM), F32),
                   jax.ShapeDtypeStruct((bsz, NSA_KV_HEADS, seq, lp), mask_dtype)],
        compiler_params=_cparams("parallel", "parallel", "parallel"),
        name="cmp_attention_topk",
    )(z16, kc, vc, bias_c, overlap, expand)


def _softmax_step(q4, k, v, bias, carry):
    m, l, acc = carry
    s = _nt_dot(q4, k) + bias
    m_new = jnp.maximum(m, jnp.max(s, axis=-1, keepdims=True))
    alpha = jnp.exp(m - m_new)
    pr = jnp.exp(s - m_new)
    l = alpha * l + jnp.sum(pr, axis=-1, keepdims=True)
    acc = alpha * acc + jnp.dot(pr.astype(BF16), v, preferred_element_type=F32)
    return m_new, l, acc


def _nsa_prompt_kernel(q_ref, ks_ref, vs_ref, kw_ref, vw_ref, bias_ref, mask_ref, ocmp_ref, gate_ref, o_ref):
    qi = pl.program_id(2)
    tq = q_ref.shape[0]
    ck = 2 * tq
    rows = NSA_GROUP * tq
    q4 = _heads_to_rows(q_ref)
    n_win = WINDOW // tq

    def bias_pair(kc, far):
        tiles = []
        for half in range(2):
            delta = qi - 2 * kc - half
            if far is not None:
                delta = jnp.minimum(delta, far)
            off = pl.multiple_of((delta + 1) * tq, tq)
            tiles.append(bias_ref[:, pl.ds(off, tq), :].reshape(rows, tq))
        return jnp.concatenate(tiles, axis=1)

    def kv_chunk(k_ref, v_ref, kc):
        off = pl.multiple_of(kc * ck, ck)
        return k_ref[pl.ds(off, ck), :], v_ref[pl.ds(off, ck), :]

    init = (jnp.full((rows, 1), NEG_INF, F32), jnp.zeros((rows, 1), F32), jnp.zeros((rows, HEAD_DIM), F32))

    def sel_body(kc, carry):
        off = pl.multiple_of(kc * ck, ck)
        mb = mask_ref[:, pl.ds(off, ck)].astype(F32)
        bias = bias_pair(kc, 2) + jnp.concatenate([mb] * NSA_GROUP, axis=0)
        k, v = kv_chunk(ks_ref, vs_ref, kc)
        return _softmax_step(q4, k, v, bias, carry)

    _, l, acc = lax.fori_loop(0, qi // 2 + 1, sel_body, init)
    o_sel = acc / l

    def win_body(kc, carry):
        k, v = kv_chunk(kw_ref, vw_ref, kc)
        return _softmax_step(q4, k, v, bias_pair(kc, None), carry)

    _, l, acc = lax.fori_loop(jnp.maximum(qi - n_win, 0) // 2, qi // 2 + 1, win_body, init)
    o_win = acc / l

    gates = gate_ref[...]
    for g in range(NSA_GROUP):
        sl = slice(g * HEAD_DIM, (g + 1) * HEAD_DIM)
        rs = slice(g * tq, (g + 1) * tq)
        o = (gates[:, 3 * g:3 * g + 1] * ocmp_ref[:, sl] + gates[:, 3 * g + 1:3 * g + 2] * o_sel[rs]
             + gates[:, 3 * g + 2:3 * g + 3] * o_win[rs])
        o_ref[:, sl] = o.astype(o_ref.dtype)


def nsa_prompt_attention(z16, z32, bias5, mask, o_cmp, bsz, seq):
    n = bsz * seq
    tq = PAGE
    nq = seq // tq
    grp = NSA_GROUP * HEAD_DIM
    seq_col = lambda col: pl.BlockSpec((seq, HEAD_DIM), lambda b, k, i: (b, col // HEAD_DIM + k))
    return pl.pallas_call(
        _nsa_prompt_kernel,
        grid=(bsz, NSA_KV_HEADS, nq),
        in_specs=[
            pl.BlockSpec((tq, grp), lambda b, k, i: (b * nq + i, COL_Q // grp + k)),
            seq_col(COL_KV + 2 * KV_LANES), seq_col(COL_KV + 3 * KV_LANES),
            seq_col(COL_WIN), seq_col(COL_WIN + KV_LANES),
            pl.BlockSpec((NSA_GROUP, bias5.shape[1], tq), lambda b, k, i: (k, 0, 0)),
            pl.BlockSpec((None, None, tq, seq), lambda b, k, i: (b, k, i, 0)),
            pl.BlockSpec((tq, grp), lambda b, k, i: (b * nq + i, k)),
            pl.BlockSpec((tq, LANE), lambda b, k, i: (b * nq + i, COL_GATE // LANE + k)),
        ],
        out_specs=pl.BlockSpec((tq, grp), lambda b, k, i: (b * nq + i, k)),
        out_shape=jax.ShapeDtypeStruct((n, NSA_HEADS * HEAD_DIM), BF16),
        compiler_params=_cparams("parallel", "parallel", "arbitrary"),
        name="nsa_prompt_attention",
    )(z16, z16, z16, z16, z16, bias5, mask, o_cmp, z32)


ROWS_PER_TOKEN = 4 * NSA_KV_HEADS


def _softmax_attend(q, k, v, bias):
    s = _nt_dot(q, k) + bias
    e = jnp.exp(s - jnp.max(s, axis=-1, keepdims=True))
    o = jnp.dot(e.astype(BF16), v, preferred_element_type=F32)
    return o / jnp.sum(e, axis=-1, keepdims=True), e


PAGE_BUFFERS = 8


def _low_half_mask():
    return lax.broadcasted_iota(jnp.int32, (SUBLANE, HEAD_DIM), 0) < NSA_KV_HEADS


def _split_token_pairs(toks):
    lo = _low_half_mask()
    a = [jnp.where(lo, toks[i], pltpu.roll(toks[i + 1], NSA_KV_HEADS, 0)) for i in (0, 2)]
    b = [jnp.where(lo, pltpu.roll(toks[i], NSA_KV_HEADS, 0), toks[i + 1]) for i in (0, 2)]
    return jnp.concatenate(a, axis=0).astype(BF16), jnp.concatenate(b, axis=0).astype(BF16)


def _nsa_sample_kernel(n_pages, n_sel, pt_ref, q_ref, pages_hbm, knew_ref, wst_ref, wnew_ref, bcmp_ref, bsel_ref,
                       bwin_ref, kpos_ref, wck_ref, wcv_ref, posk_ref, posv_ref, w2k_ref, w2v_ref, kn_ref, ovl_ref,
                       exp_ref, gate_ref, o_ref, xs_ref, ks_ref, vs_ref, kw_ref, vw_ref, pbuf_ref, sem_ref):
    b = pl.program_id(0)
    n_total = pl.num_programs(0) * n_pages
    n_past = n_pages * PAGE
    t = q_ref.shape[0]
    rpt = ROWS_PER_TOKEN
    kvh_n = NSA_KV_HEADS

    def page_copy(g):
        slot = g % PAGE_BUFFERS
        return pltpu.make_async_copy(pages_hbm.at[pt_ref[g // n_pages, g % n_pages]], pbuf_ref.at[slot],
                                     sem_ref.at[slot])

    @pl.when(b == 0)
    def _():
        for g in range(PAGE_BUFFERS):
            page_copy(g).start()

    def page_body(pg, carry):
        g = b * n_pages + pg
        page_copy(g).wait()
        page_ref = pbuf_ref.at[g % PAGE_BUFFERS]
        row0 = pl.multiple_of(pg * (CMP_PER_PAGE * kvh_n), CMP_PER_PAGE * kvh_n)
        for r in range(CMP_STRIDE):
            tl = [page_ref[pl.ds((CMP_STRIDE * hb + r) * rpt, SUBLANE), :] for hb in range(CMP_PER_PAGE)]
            for w in range(CMP_PER_PAGE // 4):
                xk, xv = _split_token_pairs(tl[4 * w:4 * w + 4])
                xs_ref[0, pl.ds(row0 + 16 * w, 16), r * HEAD_DIM:(r + 1) * HEAD_DIM] = xk
                xs_ref[1, pl.ds(row0 + 16 * w, 16), r * HEAD_DIM:(r + 1) * HEAD_DIM] = xv
        k0 = pl.multiple_of(pg * (PAGE * kvh_n), PAGE * kvh_n)
        for w in range(PAGE // 4):
            tl = [page_ref[pl.ds((4 * w + i) * rpt + SUBLANE, SUBLANE), :] for i in range(4)]
            kk, vv = _split_token_pairs(tl)
            ks_ref[pl.ds(k0 + 16 * w, 16), :] = kk
            vs_ref[pl.ds(k0 + 16 * w, 16), :] = vv

        @pl.when(g + PAGE_BUFFERS < n_total)
        def _():
            page_copy(g + PAGE_BUFFERS).start()

        return carry

    lax.fori_loop(0, n_pages, page_body, 0)
    _nsa_sample_finish(n_past, n_sel, t, q_ref, knew_ref, wst_ref, wnew_ref, bcmp_ref, bsel_ref, bwin_ref, kpos_ref,
                       wck_ref, wcv_ref, posk_ref, posv_ref, w2k_ref, w2v_ref, kn_ref, ovl_ref, exp_ref, gate_ref,
                       o_ref, xs_ref, ks_ref, vs_ref, kw_ref, vw_ref)


def _nsa_sample_finish(n_past, n_sel, t, q_ref, knew_ref, wst_ref, wnew_ref, bcmp_ref, bsel_ref, bwin_ref, kpos_ref,
                       wck_ref, wcv_ref, posk_ref, posv_ref, w2k_ref, w2v_ref, kn_ref, ovl_ref, exp_ref, gate_ref,
                       o_ref, xs_ref, ks_ref, vs_ref, kw_ref, vw_ref):
    kvh_n = NSA_KV_HEADS

    def new_rows(x):
        rows = jnp.concatenate([x[:, k * HEAD_DIM:(k + 1) * HEAD_DIM] for k in range(kvh_n)], axis=0)
        return jnp.concatenate([rows, jnp.zeros((PAGE * kvh_n - rows.shape[0], HEAD_DIM), F32)], axis=0).astype(BF16)

    knew = knew_ref[...]
    ks_ref[kvh_n * n_past:kvh_n * (n_past + PAGE), :] = new_rows(knew[:, :KV_LANES])
    vs_ref[kvh_n * n_past:kvh_n * (n_past + PAGE), :] = new_rows(knew[:, KV_LANES:])

    q = q_ref[...]
    qall = jnp.concatenate([q[:, h * HEAD_DIM:(h + 1) * HEAD_DIM] for h in range(NSA_HEADS)], axis=0).astype(BF16)

    kc = _compress_finish(xs_ref[0], wck_ref[...], posk_ref[...], w2k_ref[...], kn_ref[...], kvh_minor=True)
    vc = _compress_finish(xs_ref[1], wcv_ref[...], posv_ref[...], w2v_ref[...], None, kvh_minor=True)
    o_cmp, e_c = _softmax_attend(qall, kc.astype(BF16), vc.astype(BF16), bcmp_ref[...])
    p_c = e_c / jnp.sum(e_c, axis=-1, keepdims=True)
    imp_h = _dot3_lhs(p_c, ovl_ref[...])
    imp = jnp.concatenate(
        [sum(imp_h[(k * NSA_GROUP + g) * t:(k * NSA_GROUP + g + 1) * t] for g in range(NSA_GROUP))
         for k in range(kvh_n)], axis=0)
    qpos = n_past + lax.broadcasted_iota(jnp.int32, (kvh_n * t, 1), 0) % t
    mb = _selected_key_mask(imp, exp_ref[...], qpos, kpos_ref[...], n_sel)
    mb = jnp.concatenate([mb[k * t:(k + 1) * t] for k in range(kvh_n) for _ in range(NSA_GROUP)], axis=0)

    o_sel, _ = _softmax_attend(qall, ks_ref[...], vs_ref[...], bsel_ref[...] + mb)

    n_st = wst_ref.shape[0] // (2 * kvh_n)

    def win_body(w, carry):
        base = pl.multiple_of(w * 32, 32)
        tl = [wst_ref[pl.ds(base + SUBLANE * i, SUBLANE), :] for i in range(4)]
        kk, vv = _split_token_pairs(tl)
        dst = pl.multiple_of(w * 16, 16)
        kw_ref[pl.ds(dst, 16), :] = kk
        vw_ref[pl.ds(dst, 16), :] = vv
        return carry

    lax.fori_loop(0, n_st // 4, win_body, 0, unroll=4)
    wnew = wnew_ref[...]
    kw_ref[kvh_n * n_st:kvh_n * (n_st + PAGE), :] = new_rows(wnew[:, :KV_LANES])
    vw_ref[kvh_n * n_st:kvh_n * (n_st + PAGE), :] = new_rows(wnew[:, KV_LANES:])
    o_win, _ = _softmax_attend(qall, kw_ref[...], vw_ref[...], bwin_ref[...])

    gates = gate_ref[...]
    for h in range(NSA_HEADS):
        kvh, g = divmod(h, NSA_GROUP)
        rs = slice(h * t, (h + 1) * t)
        gc = kvh * LANE + 3 * g
        o_ref[:, h * HEAD_DIM:(h + 1) * HEAD_DIM] = (
            gates[:, gc:gc + 1] * o_cmp[rs] + gates[:, gc + 1:gc + 2] * o_sel[rs] + gates[:, gc + 2:gc + 3] * o_win[rs])


def _sample_column_tables(bias_cmp, bias_sel, bias_win, seq, n_past, n_win):
    kvh_n = NSA_KV_HEADS
    row_kvh = np.arange(NSA_HEADS * seq) // (NSA_GROUP * seq)

    def columns(n_old):
        c = np.arange((n_old + PAGE) * kvh_n)
        old = c < n_old * kvh_n
        cn = c - n_old * kvh_n
        valid = old | (cn < kvh_n * seq)
        pos = np.where(old, c // kvh_n, n_old + cn % seq)
        kvh = np.where(old, c % kvh_n, cn // seq)
        return np.where(valid, pos, 0), kvh, valid

    def widen(bias, pos, kvh, valid):
        ok = valid[None, :] & (kvh[None, :] == row_kvh[:, None])
        return jnp.where(jnp.asarray(ok), jnp.take(bias, jnp.asarray(pos), axis=1), NEG_INF)

    cc = np.arange(N_CMP_PAD * kvh_n)
    b_cmp = widen(bias_cmp, cc // kvh_n, cc % kvh_n, np.ones_like(cc, bool))
    pos_s, kvh_s, valid_s = columns(n_past)
    b_sel = widen(bias_sel, pos_s, kvh_s, valid_s)
    pos_w, kvh_w, valid_w = columns(n_win)
    b_win = widen(bias_win, pos_w, kvh_w, valid_w)
    kpos = jnp.asarray(np.where(valid_s, pos_s, np.iinfo(np.int32).max)[None, :].astype(np.int32))
    ci = np.arange(N_CMP_PAD)[:, None] * CMP_STRIDE
    sj = np.arange(LANE)[None, :] * SEL_BLOCK
    overlap = np.repeat(((ci < sj + SEL_BLOCK) & (ci + CMP_BLOCK > sj)).astype(np.float32), kvh_n, axis=0)
    expand = ((pos_s[None, :] // SEL_BLOCK == np.arange(LANE)[:, None]) & valid_s[None, :]).astype(np.float32)
    return b_cmp, b_sel, b_win, kpos, jnp.asarray(overlap, BF16), jnp.asarray(expand, BF16)


def nsa_sample_attention(z32, pages, page_table, win_state, bias_cmp, bias_sel, bias_win, cmp_w, seq):
    bsz, n_pages = page_table.shape
    n = bsz * seq
    width = NSA_HEADS * HEAD_DIM
    kvh_n = NSA_KV_HEADS
    n_past = n_pages * PAGE
    n_win = win_state.shape[1] // (2 * kvh_n)
    n_sel = -(-(n_past + seq) // SEL_BLOCK)
    assert bsz * n_pages >= PAGE_BUFFERS and bias_sel.shape[1] == n_past + PAGE
    b_cmp, b_sel, b_win, kpos, overlap, expand = _sample_column_tables(bias_cmp, bias_sel, bias_win, seq, n_past, n_win)
    cst = lambda a: pl.BlockSpec(a.shape, lambda b, pt: (0,) * a.ndim, pipeline_mode=pl.Buffered(1))
    per_seq = lambda w, col: pl.BlockSpec((seq, w), lambda b, pt: (b, col // w))
    consts = [b_cmp, b_sel, b_win, kpos] + list(cmp_w) + [overlap, expand]
    return pl.pallas_call(
        functools.partial(_nsa_sample_kernel, n_pages, n_sel),
        grid_spec=pltpu.PrefetchScalarGridSpec(
            num_scalar_prefetch=1,
            grid=(bsz,),
            in_specs=[per_seq(width, COL_Q),
                      pl.BlockSpec(memory_space=pl.ANY),
                      per_seq(2 * KV_LANES, COL_KV + 2 * KV_LANES),
                      pl.BlockSpec((None, win_state.shape[1], HEAD_DIM), lambda b, pt: (b, 0, 0)),
                      per_seq(2 * KV_LANES, COL_WIN)]
                     + [cst(a) for a in consts] + [per_seq(kvh_n * LANE, COL_GATE)],
            out_specs=per_seq(width, 0),
            scratch_shapes=[pltpu.VMEM((2, kvh_n * N_CMP_PAD, CMP_STRIDE * HEAD_DIM), BF16),
                            pltpu.VMEM((kvh_n * (n_past + PAGE), HEAD_DIM), BF16),
                            pltpu.VMEM((kvh_n * (n_past + PAGE), HEAD_DIM), BF16),
                            pltpu.VMEM((kvh_n * (n_win + PAGE), HEAD_DIM), BF16),
                            pltpu.VMEM((kvh_n * (n_win + PAGE), HEAD_DIM), BF16),
                            pltpu.VMEM((PAGE_BUFFERS, PAGE * ROWS_PER_TOKEN, HEAD_DIM), F32),
                            pltpu.SemaphoreType.DMA((PAGE_BUFFERS,))],
        ),
        out_shape=jax.ShapeDtypeStruct((n, width), F32),
        compiler_params=_cparams("arbitrary"),
        name="nsa_sample_attention",
    )(page_table, z32, pages, z32, win_state, z32, *consts, z32)


SCALE = HEAD_DIM ** -0.5


def _layer_weights(i, ffn1_w_gate, ffn1_w_up, ffn1_w_down, w_in, w_out, nsa_q_norm, nsa_k_norm,
                   ffn2_w_gate, ffn2_w_up, ffn2_w_down, ple_w_gate, ple_w_proj):
    wi = w_in[i]
    wgate = wi[:, COL_GATE:].reshape(-1, NSA_KV_HEADS, 3 * NSA_GROUP)
    wgate = jnp.pad(wgate, ((0, 0), (0, 0), (0, LANE - 3 * NSA_GROUP))).reshape(-1, NSA_KV_HEADS * LANE)
    w_mix = jnp.concatenate([wi[:, :COL_GATE], wgate], axis=1).astype(BF16)
    t_q, t_ks, t_kw = COL_Q // PROJ_TILE, (COL_KV + 2 * KV_LANES) // PROJ_TILE, COL_WIN // PROJ_TILE
    n_q = NSA_WIDTH // PROJ_TILE
    gains = jnp.ones((Z_COLS // PROJ_TILE, 1, HEAD_DIM), F32)
    gains = gains.at[t_q:t_q + n_q].set(nsa_q_norm[i].astype(F32) * SCALE)
    gains = gains.at[t_ks].set(nsa_k_norm[i, 1].astype(F32)).at[t_kw].set(nsa_k_norm[i, 2].astype(F32))
    return dict(
        ffn1=(ffn1_w_gate[i].astype(BF16), ffn1_w_up[i].astype(BF16), ffn1_w_down[i].astype(BF16)),
        ffn2=(ffn2_w_gate[i].astype(BF16), ffn2_w_up[i].astype(BF16), ffn2_w_down[i].astype(BF16)),
        w_mix=w_mix, mix_gains=gains, norm_tiles=tuple(range(t_q, t_q + n_q)) + (t_ks, t_kw),
        gate_tile=COL_GATE // PROJ_TILE,
        wo_h=w_out[i, :HGRN_WIDTH].astype(BF16), wo_n=w_out[i, HGRN_WIDTH:].astype(BF16),
        ple_gate=ple_w_gate[i].astype(BF16), ple_proj=ple_w_proj[i].astype(BF16),
    )


def _run_layer(i, w, x, pemb, bsz, seq, s0, nsa_fn, norms, hgrn_lb):
    ffn1_norm, mix_norm, hgrn_out_norm, ffn2_norm, ple_norm, ple_post_norm = norms
    x1 = ffn_residual(x, ffn1_norm[i], *w["ffn1"])
    z32, z16 = mixer_project(x1, mix_norm[i], w["w_mix"], w["mix_gains"], w["norm_tiles"], w["gate_tile"])
    o_h, s_fin = hgrn2_mix(z32, hgrn_lb, hgrn_out_norm[i], i, bsz, seq, HGRN_HEADS, s0=s0,
                           heads_per_step=HGRN_HEADS if seq % 16 else 4)
    o_n = nsa_fn(z32, z16)
    x2 = out_project_residual(x1, o_h, o_n, w["wo_h"], w["wo_n"])
    x3 = ffn_residual(x2, ffn2_norm[i], *w["ffn2"])
    y = ple_residual(x3, ple_norm[i], w["ple_gate"], pemb, w["ple_proj"], ple_post_norm[i])
    return y, z32[:, COL_KV:COL_WIN], z32[:, COL_WIN:COL_GATE], s_fin


def kernel(x_prompt, x_sample, cache_kv, state_win_kv, state_hgrn, page_table, p_prompt, p_sample, ffn1_norm, ffn1_w_gate, ffn1_w_up, ffn1_w_down, mix_norm, w_in, w_out, hgrn_lb, hgrn_out_norm, nsa_q_norm, nsa_k_norm, cmp_pos, cmp_w1, cmp_w2, rel_bias_table, ffn2_norm, ffn2_w_gate, ffn2_w_up, ffn2_w_down, ple_norm, ple_w_gate, ple_w_proj, ple_post_norm):
    depth = cache_kv.shape[0]
    bp, tp, d = x_prompt.shape
    bs, ts, _ = x_sample.shape
    n_pool = cache_kv.shape[1]
    n_pages = page_table.shape[1]
    past = n_pages * PAGE
    win_keep = state_win_kv.shape[2]
    assert tp % PAGE == 0 and tp >= WINDOW and win_keep == WINDOW
    norms = (ffn1_norm, mix_norm, hgrn_out_norm, ffn2_norm, ple_norm, ple_post_norm)
    table = rel_bias_table.astype(F32)

    n_cmp = (tp - CMP_BLOCK) // CMP_STRIDE + 1
    bias_pc = rel_bias(table, tp, N_CMP_PAD, 0, CMP_BLOCK - 1, CMP_STRIDE, n_cmp)
    bias_p5 = rel_bias(table, WINDOW + 3 * PAGE, PAGE, -PAGE, 0, 1, PAGE, window=WINDOW)
    n_cmp_s = (past + ts - CMP_BLOCK) // CMP_STRIDE + 1
    assert n_cmp_s <= N_CMP_PAD - 1 and (n_cmp_s - 1) * CMP_STRIDE + CMP_BLOCK <= past
    sel_cols = past + PAGE
    bias_sc = rel_bias(table, ts, N_CMP_PAD, past, CMP_BLOCK - 1, CMP_STRIDE, n_cmp_s).reshape(NSA_HEADS * ts, N_CMP_PAD)
    bias_ss = rel_bias(table, ts, sel_cols, past, 0, 1, past + ts).reshape(NSA_HEADS * ts, sel_cols)
    bias_sw = rel_bias(table, ts, win_keep + PAGE, win_keep, 0, 1, win_keep + ts, window=WINDOW)
    bias_sw = bias_sw.reshape(NSA_HEADS * ts, win_keep + PAGE)

    xp = x_prompt.reshape(bp * tp, d)
    xs = x_sample.reshape(bs * ts, d)
    outs = [[] for _ in range(6)]
    for i in range(depth):
        w = _layer_weights(i, ffn1_w_gate, ffn1_w_up, ffn1_w_down, w_in, w_out, nsa_q_norm, nsa_k_norm,
                           ffn2_w_gate, ffn2_w_up, ffn2_w_down, ple_w_gate, ple_w_proj)
        cmp_w = compress_weights(cmp_w1[i], cmp_w2[i], cmp_pos[i], nsa_k_norm[i, 0])

        def nsa_prompt(z32, z16):
            pages = z32.reshape(bp * tp // PAGE, PAGE, Z_COLS)
            pt = jnp.arange(bp * tp // PAGE, dtype=jnp.int32).reshape(bp, tp // PAGE)
            kc, vc = compress_cache(pages, pt, COL_KV // HEAD_DIM, cmp_w)
            o_cmp, mask = cmp_attention_topk(z16, kc, vc, bias_pc, bp, tp, 0, tp)
            return nsa_prompt_attention(z16, z32, bias_p5, mask, o_cmp, bp, tp)

        def nsa_sample(z32, z16):
            pages = cache_kv[i].reshape(n_pool, PAGE * ROWS_PER_TOKEN, HEAD_DIM)
            wst = state_win_kv[i].reshape(bs, win_keep * 2 * NSA_KV_HEADS, HEAD_DIM)
            return nsa_sample_attention(z32, pages, page_table, wst, bias_sc, bias_ss, bias_sw, cmp_w, ts)

        xp, kv_p, win_p, h_p = _run_layer(i, w, xp, p_prompt[i].reshape(bp * tp, -1), bp, tp, None, nsa_prompt, norms, hgrn_lb)
        xs, kv_s, win_s, h_s = _run_layer(i, w, xs, p_sample[i].reshape(bs * ts, -1), bs, ts, state_hgrn[i], nsa_sample, norms, hgrn_lb)
        outs[0].append(kv_p.reshape(bp, tp, 4, NSA_KV_HEADS, HEAD_DIM))
        outs[1].append(win_p.reshape(bp, tp, 2, NSA_KV_HEADS, HEAD_DIM)[:, -WINDOW:])
        outs[2].append(h_p)
        outs[3].append(kv_s.reshape(bs, ts, 4, NSA_KV_HEADS, HEAD_DIM))
        win_new = win_s.reshape(bs, ts, 2, NSA_KV_HEADS, HEAD_DIM)
        outs[4].append(jnp.concatenate([state_win_kv[i], win_new], axis=1)[:, -win_keep:])
        outs[5].append(h_s.astype(state_hgrn.dtype))
    return (xp.reshape(bp, tp, d), xs.reshape(bs, ts, d)) + tuple(jnp.stack(o) for o in outs)
```

```python
import functools
import math

import numpy as np
import jax
import jax.numpy as jnp
from jax import lax
from jax.experimental import pallas as pl
from jax.experimental.pallas import tpu as pltpu

F32 = jnp.float32
BF16 = jnp.bfloat16

LANE = 128
SUBLANE = 8
VMEM_LIMIT_BYTES = 56 * 1024 * 1024

HEAD_DIM = 128
RMS_EPS = 1e-6
NEG_INF = -1e30


def _cparams(*sem):
    return pltpu.CompilerParams(dimension_semantics=sem, vmem_limit_bytes=VMEM_LIMIT_BYTES)


def _tile(n, pref):
    if n <= pref:
        return n
    t = pref
    while t >= SUBLANE:
        if n % t == 0:
            return t
        t -= SUBLANE
    return n


def _rms(x, gain):
    ms = jnp.mean(x * x, axis=-1, keepdims=True)
    return x * lax.rsqrt(ms + RMS_EPS) * gain


def _norm_rows_to(x_ref, gain_ref, dst_ref):
    rows = x_ref.shape[0]
    rc = 32 if rows % 32 == 0 else rows

    def body(i, carry):
        r = pl.multiple_of(i * rc, rc)
        dst_ref[pl.ds(r, rc), :] = _rms(x_ref[pl.ds(r, rc), :], gain_ref[...]).astype(dst_ref.dtype)
        return carry

    lax.fori_loop(0, rows // rc, body, 0)


def _ffn_kernel(x_ref, gain_ref, wg_ref, wu_ref, wd_ref, o_ref, hn_ref):
    @pl.when(pl.program_id(1) == 0)
    def _():
        _norm_rows_to(x_ref, gain_ref, hn_ref)
        o_ref[...] = x_ref[...]

    h = hn_ref[...]
    g = jnp.dot(h, wg_ref[...], preferred_element_type=F32)
    u = jnp.dot(h, wu_ref[...], preferred_element_type=F32)
    a = (g * jax.nn.sigmoid(g) * (0.5 * u)).astype(BF16)
    o_ref[...] += jnp.dot(a, wd_ref[...], preferred_element_type=F32)


def ffn_residual(x, gain, wg, wu, wd, tm_pref=512, tf_pref=256):
    n, d = x.shape
    f = wg.shape[1]
    tm, tf = _tile(n, tm_pref), _tile(f, tf_pref)
    return pl.pallas_call(
        _ffn_kernel,
        grid=(n // tm, f // tf),
        in_specs=[
            pl.BlockSpec((tm, d), lambda i, j: (i, 0)),
            pl.BlockSpec((1, d), lambda i, j: (0, 0)),
            pl.BlockSpec((d, tf), lambda i, j: (0, j)),
            pl.BlockSpec((d, tf), lambda i, j: (0, j)),
            pl.BlockSpec((tf, d), lambda i, j: (j, 0)),
        ],
        out_specs=pl.BlockSpec((tm, d), lambda i, j: (i, 0)),
        out_shape=jax.ShapeDtypeStruct((n, d), F32),
        scratch_shapes=[pltpu.VMEM((tm, d), BF16)],
        compiler_params=_cparams("parallel", "arbitrary"),
        name="ffn_residual",
    )(x, gain.reshape(1, d), wg, wu, wd)


PROJ_TILE = 512


def _mixproj_kernel(norm_tiles, gate_tile, x_ref, gain_ref, w_ref, hg_ref, z32_ref, z16_ref, hn_ref):
    j = pl.program_id(1)

    @pl.when(j == 0)
    def _():
        _norm_rows_to(x_ref, gain_ref, hn_ref)

    acc = jnp.dot(hn_ref[...], w_ref[...], preferred_element_type=F32)
    is_norm = functools.reduce(jnp.logical_or, [j == t for t in norm_tiles])
    is_gate = j == gate_tile

    def put(val, sl=slice(None)):
        z32_ref[:, sl] = val
        z16_ref[:, sl] = val.astype(BF16)

    @pl.when(is_norm)
    def _():
        gain = hg_ref[0]
        for h in range(acc.shape[1] // HEAD_DIM):
            sl = slice(h * HEAD_DIM, (h + 1) * HEAD_DIM)
            put(_rms(acc[:, sl], gain), sl)

    @pl.when(is_gate)
    def _():
        put(jax.nn.sigmoid(acc))

    @pl.when(jnp.logical_not(jnp.logical_or(is_norm, is_gate)))
    def _():
        put(acc)


def mixer_project(x, gain, w, head_gain, norm_tiles, gate_tile, tm_pref=512):
    n, d = x.shape
    ncols = w.shape[1]
    tm, tn = _tile(n, tm_pref), PROJ_TILE
    return pl.pallas_call(
        functools.partial(_mixproj_kernel, norm_tiles, gate_tile),
        grid=(n // tm, ncols // tn),
        in_specs=[
            pl.BlockSpec((tm, d), lambda i, j: (i, 0)),
            pl.BlockSpec((1, d), lambda i, j: (0, 0)),
            pl.BlockSpec((d, tn), lambda i, j: (0, j)),
            pl.BlockSpec((1, 1, HEAD_DIM), lambda i, j: (j, 0, 0)),
        ],
        out_specs=[pl.BlockSpec((tm, tn), lambda i, j: (i, j))] * 2,
        out_shape=[jax.ShapeDtypeStruct((n, ncols), F32), jax.ShapeDtypeStruct((n, ncols), BF16)],
        scratch_shapes=[pltpu.VMEM((tm, d), BF16)],
        compiler_params=_cparams("parallel", "arbitrary"),
        name="mixer_project",
    )(x, gain.reshape(1, d), w, head_gain)


def _outproj_kernel(x_ref, a_ref, b_ref, wa_ref, wb_ref, o_ref):
    acc = jnp.dot(a_ref[...].astype(BF16), wa_ref[...], preferred_element_type=F32)
    acc += jnp.dot(b_ref[...].astype(BF16), wb_ref[...], preferred_element_type=F32)
    o_ref[...] = x_ref[...] + acc


def out_project_residual(x, a, b, wa, wb, tm_pref=512, tn_pref=1024):
    n, d = x.shape
    ka, kb = a.shape[1], b.shape[1]
    tm, tn = _tile(n, tm_pref), _tile(d, tn_pref)
    return pl.pallas_call(
        _outproj_kernel,
        grid=(n // tm, d // tn),
        in_specs=[
            pl.BlockSpec((tm, tn), lambda i, j: (i, j)),
            pl.BlockSpec((tm, ka), lambda i, j: (i, 0)),
            pl.BlockSpec((tm, kb), lambda i, j: (i, 0)),
            pl.BlockSpec((ka, tn), lambda i, j: (0, j)),
            pl.BlockSpec((kb, tn), lambda i, j: (0, j)),
        ],
        out_specs=pl.BlockSpec((tm, tn), lambda i, j: (i, j)),
        out_shape=jax.ShapeDtypeStruct((n, d), F32),
        compiler_params=_cparams("parallel", "arbitrary"),
        name="out_project_residual",
    )(x, a, b, wa, wb)


def _ple_kernel(tn, x_ref, gain_ref, wg_ref, p_ref, wp_ref, pg_ref, o_ref, hn_ref, pe_ref):
    j = pl.program_id(1)

    @pl.when(j == 0)
    def _():
        _norm_rows_to(x_ref, gain_ref, hn_ref)
        pe_ref[...] = jnp.dot(p_ref[...].astype(BF16), wp_ref[...], preferred_element_type=F32)
        _norm_rows_to(pe_ref, pg_ref, pe_ref)

    c = pl.multiple_of(j * tn, LANE)
    gate = jax.nn.sigmoid(jnp.dot(hn_ref[...], wg_ref[...], preferred_element_type=F32))
    o_ref[...] = x_ref[:, pl.ds(c, tn)] + gate * pe_ref[:, pl.ds(c, tn)]


def ple_residual(x, gain, wg, p, wp, post_gain, tm_pref=512, tn_pref=512):
    n, d = x.shape
    pd = p.shape[1]
    tm, tn = _tile(n, tm_pref), _tile(d, tn_pref)
    return pl.pallas_call(
        functools.partial(_ple_kernel, tn),
        grid=(n // tm, d // tn),
        in_specs=[
            pl.BlockSpec((tm, d), lambda i, j: (i, 0)),
            pl.BlockSpec((1, d), lambda i, j: (0, 0)),
            pl.BlockSpec((d, tn), lambda i, j: (0, j)),
            pl.BlockSpec((tm, pd), lambda i, j: (i, 0)),
            pl.BlockSpec((pd, d), lambda i, j: (0, 0)),
            pl.BlockSpec((1, d), lambda i, j: (0, 0)),
        ],
        out_specs=pl.BlockSpec((tm, tn), lambda i, j: (i, j)),
        out_shape=jax.ShapeDtypeStruct((n, d), F32),
        scratch_shapes=[pltpu.VMEM((tm, d), BF16), pltpu.VMEM((tm, d), F32)],
        compiler_params=_cparams("parallel", "arbitrary"),
        name="ple_residual",
    )(x, gain.reshape(1, d), wg, p, wp, post_gain.reshape(1, d))


HGRN_DIAG = SUBLANE


def _split3_dot(lhs_bf16, x):
    hi = x.astype(BF16)
    r1 = x - hi.astype(F32)
    mid = r1.astype(BF16)
    lo = (r1 - mid.astype(F32)).astype(BF16)
    acc = jnp.dot(lhs_bf16, hi, preferred_element_type=F32)
    acc += jnp.dot(lhs_bf16, mid, preferred_element_type=F32)
    acc += jnp.dot(lhs_bf16, lo, preferred_element_type=F32)
    return acc


def _bcast_row_in_blocks(a, bs, row):
    c, k = a.shape
    a3 = a.reshape(c // bs, bs, k)
    return jnp.broadcast_to(a3[:, row:row + 1, :], (c // bs, bs, k)).reshape(c, k)


def _nt_dot(a, b):
    return lax.dot_general(a, b, (((1,), (1,)), ((), ())), preferred_element_type=F32)


def _hgrn_kernel(layer, has_s0, hps, q_ref, f_ref, i_ref, g_ref, lb_ref, og_ref, *rest):
    if has_s0:
        s0_ref, o_ref, sfin_ref, st_ref = rest
    else:
        s0_ref = None
        o_ref, sfin_ref, st_ref = rest
    ci = pl.program_id(2)

    @pl.when(ci == 0)
    def _():
        if s0_ref is not None:
            st_ref[...] = jnp.stack([s0_ref[hh].T for hh in range(hps)])
        else:
            st_ref[...] = jnp.zeros(st_ref.shape, F32)

    outs, states = [], []
    for hh in range(hps):
        ls = slice(hh * HEAD_DIM, (hh + 1) * HEAD_DIM)
        o, st_new = _hgrn_head(layer, q_ref[:, ls], f_ref[:, ls], i_ref[:, ls], g_ref[:, ls], lb_ref[:, ls],
                               og_ref[...], st_ref[hh])
        outs.append(o)
        states.append(st_new)
    o_ref[...] = jnp.concatenate(outs, axis=1).astype(o_ref.dtype)
    st_ref[...] = jnp.stack(states)

    @pl.when(ci == pl.num_programs(2) - 1)
    def _():
        sfin_ref[...] = jnp.stack([s.T for s in states])


def _hgrn_head(layer, q, f_raw, v, gr, lbr, out_gain, st):
    c = q.shape[0]
    e = jnp.exp(lbr - jnp.max(lbr, axis=0, keepdims=True))
    lb = jnp.sum(e[:layer + 1], axis=0, keepdims=True) / jnp.sum(e, axis=0, keepdims=True)

    fg = lb + (1.0 - lb) * jax.nn.sigmoid(f_raw)
    logf = jnp.log(fg)
    kk = 1.0 - fg

    ti = lax.broadcasted_iota(jnp.int32, (c, c), 0)
    si = lax.broadcasted_iota(jnp.int32, (c, c), 1)
    tril = (si <= ti)
    a = _split3_dot(tril.astype(BF16), logf)
    a_last = a[c - 1:c, :]

    kk_b = kk.astype(BF16)
    d = HGRN_DIAG
    xs = [(q * jnp.exp(jnp.minimum(a - _bcast_row_in_blocks(a, d, j), 0.0))).astype(BF16) for j in range(d)]
    res = _nt_dot(jnp.concatenate(xs, axis=0), kk_b)
    attn = jnp.zeros((c, c), F32)
    for j in range(d):
        attn += jnp.where((si % d) == j, res[j * c:(j + 1) * c], 0.0)
    attn = jnp.where(((si // d) == (ti // d)) & tril, attn, 0.0)
    bs = 2 * d
    while bs <= c:
        half = bs // 2
        bnd = _bcast_row_in_blocks(a, bs, half - 1)
        qe = (q * jnp.exp(jnp.minimum(a - bnd, 0.0))).astype(BF16)
        ke = (kk * jnp.exp(jnp.minimum(bnd - a, 0.0))).astype(BF16)
        m = ((si // bs) == (ti // bs)) & ((ti % bs) >= half) & ((si % bs) < half)
        attn += jnp.where(m, _nt_dot(qe, ke), 0.0)
        bs *= 2

    v_b = v.astype(BF16)
    o = jnp.dot(attn.astype(BF16), v_b, preferred_element_type=F32)
    o += _nt_dot((q * jnp.exp(a)).astype(BF16), st.astype(BF16))
    kd = (kk * jnp.exp(a_last - a)).astype(BF16)
    st_new = st * jnp.exp(a_last) + jnp.dot(v_b.T, kd, preferred_element_type=F32)
    return _rms(o, out_gain) * (gr * jax.nn.sigmoid(gr)), st_new


def hgrn2_mix(zh, hgrn_lb, out_gain, layer, batch, seq, heads, s0=None, chunk_pref=128, heads_per_step=4):
    n = batch * seq
    dk = HEAD_DIM
    c = _tile(seq, chunk_pref)
    assert c % HGRN_DIAG == 0 and (c // HGRN_DIAG) & (c // HGRN_DIAG - 1) == 0
    nc = seq // c
    nl = hgrn_lb.shape[0]
    hps = heads_per_step
    hg = heads // hps

    def zspec(sec):
        return pl.BlockSpec((c, hps * dk), lambda b, h, ci: (b * nc + ci, sec * hg + h))

    state_spec = pl.BlockSpec((None, hps, dk, dk), lambda b, h, ci: (b, h, 0, 0))
    in_specs = [zspec(0), zspec(1), zspec(2), zspec(3),
                pl.BlockSpec((nl, hps * dk), lambda b, h, ci: (0, h)),
                pl.BlockSpec((1, dk), lambda b, h, ci: (0, 0))]
    args = [zh, zh, zh, zh, hgrn_lb, out_gain.reshape(1, dk)]
    if s0 is not None:
        in_specs.append(state_spec)
        args.append(s0)
    return pl.pallas_call(
        functools.partial(_hgrn_kernel, layer, s0 is not None, hps),
        grid=(batch, hg, nc),
        in_specs=in_specs,
        out_specs=[pl.BlockSpec((c, hps * dk), lambda b, h, ci: (b * nc + ci, h)), state_spec],
        out_shape=[jax.ShapeDtypeStruct((n, heads * dk), BF16 if c % 16 == 0 else F32),
                   jax.ShapeDtypeStruct((batch, heads, dk, dk), F32)],
        scratch_shapes=[pltpu.VMEM((hps, dk, dk), F32)],
        compiler_params=_cparams("parallel", "parallel", "arbitrary"),
        name="hgrn2_mix",
    )(*args)


NSA_KV_HEADS = 4
NSA_GROUP = 4
NSA_HEADS = NSA_KV_HEADS * NSA_GROUP
KV_LANES = NSA_KV_HEADS * HEAD_DIM
PAGE = 128
CMP_BLOCK = 32
CMP_STRIDE = 16
CMP_PER_PAGE = PAGE // CMP_STRIDE
N_CMP_PAD = 128
SEL_BLOCK = 64
SEL_TOPK = 16
SEL_LOCAL = 2
SEL_FORCE = 1e3
SEL_INVALID = -1e9
WINDOW = 512
N_BUCKETS = 32
MAX_DISTANCE = 128
MASKED = -1e29

HGRN_HEADS = 16
HGRN_WIDTH = HGRN_HEADS * HEAD_DIM
NSA_WIDTH = NSA_HEADS * HEAD_DIM
COL_Q = 4 * HGRN_WIDTH
COL_KV = COL_Q + NSA_WIDTH
COL_WIN = COL_KV + 4 * KV_LANES
COL_GATE = COL_WIN + 2 * KV_LANES
Z_COLS = COL_GATE + NSA_KV_HEADS * LANE


def _bias_kernel(q0, k0, kstride, ncols_valid, window, key_major_tiles, table_ref, o_ref):
    h = pl.program_id(0)
    rt, cols = o_ref.shape
    r = lax.broadcasted_iota(jnp.int32, (rt, cols), 0)
    c = lax.broadcasted_iota(jnp.int32, (rt, cols), 1)
    if key_major_tiles:
        dist = (q0 + pl.program_id(1) * rt + c) - (k0 + r * kstride)
    else:
        dist = (q0 + r + pl.program_id(1) * rt) - (k0 + c * kstride)
    dpos = jnp.maximum(dist, 0)
    max_exact = N_BUCKETS // 2
    log_ratio = jnp.log(jnp.maximum(dpos, 1).astype(F32) / max_exact) / math.log(MAX_DISTANCE / max_exact)
    large = jnp.minimum(max_exact + (log_ratio * (N_BUCKETS - max_exact)).astype(jnp.int32), N_BUCKETS - 1)
    bucket = jnp.where(dpos < max_exact, dpos, large)
    acc = jnp.zeros((rt, cols), F32)
    for b in range(N_BUCKETS):
        acc = jnp.where(bucket == b, table_ref[b, h], acc)
    valid = (dist >= 0) & (c < ncols_valid)
    if window is not None:
        valid = valid & (dist < window)
    o_ref[...] = jnp.where(valid, acc, NEG_INF)


def rel_bias(table, rows, cols, q0, k0, kstride, ncols_valid, window=None, key_major_tiles=False):
    rt = cols if key_major_tiles else _tile(rows, 256)
    return pl.pallas_call(
        functools.partial(_bias_kernel, q0, k0, kstride, ncols_valid, window, key_major_tiles),
        grid=(NSA_HEADS, rows // rt),
        in_specs=[pl.BlockSpec(memory_space=pltpu.SMEM)],
        out_specs=pl.BlockSpec((None, rt, cols), lambda h, i: (h, i, 0)),
        out_shape=jax.ShapeDtypeStruct((NSA_HEADS, rows, cols), F32),
        compiler_params=_cparams("parallel", "parallel"),
        name="rel_bias",
    )(table)


def _compress_kernel(pt_ref, *refs):
    n_in = 2 * NSA_KV_HEADS
    page_refs = refs[:n_in]
    wck_ref, wcv_ref, posk_ref, posv_ref, w2k_ref, w2v_ref, kn_ref, kc_ref, vc_ref, xs_ref = refs[n_in:]
    p = pl.program_id(1)
    for slot in range(2):
        for kvh in range(NSA_KV_HEADS):
            page_ref = page_refs[slot * NSA_KV_HEADS + kvh]
            row0 = pl.multiple_of(kvh * N_CMP_PAD + p * CMP_PER_PAGE, CMP_PER_PAGE)
            for r in range(CMP_STRIDE):
                xs_ref[slot, pl.ds(row0, CMP_PER_PAGE), r * HEAD_DIM:(r + 1) * HEAD_DIM] = (
                    page_ref[pl.ds(r, CMP_PER_PAGE, stride=CMP_STRIDE), :])

    @pl.when(p == pl.num_programs(1) - 1)
    def _():
        kc = _compress_finish(xs_ref[0].astype(BF16), wck_ref[...], posk_ref[...], w2k_ref[...], kn_ref[...])
        vc = _compress_finish(xs_ref[1].astype(BF16), wcv_ref[...], posv_ref[...], w2v_ref[...], None)
        kc_ref[...] = kc.reshape(NSA_KV_HEADS, N_CMP_PAD, HEAD_DIM).astype(kc_ref.dtype)
        vc_ref[...] = vc.reshape(NSA_KV_HEADS, N_CMP_PAD, HEAD_DIM).astype(vc_ref.dtype)


def _compress_finish(x, wcat, posb, w2, k_gain, kvh_minor=False):
    m_rows = x.shape[0]
    hid = w2.shape[0]
    step = NSA_KV_HEADS if kvh_minor else 1
    pq = jnp.dot(x, wcat, preferred_element_type=F32)
    nxt = pltpu.roll(pq[:, hid:], m_rows - step, 0)
    hcur = pq[:, :hid] + nxt + posb
    act = (hcur * jax.nn.sigmoid(hcur)).astype(BF16)
    out = jnp.dot(act, w2, preferred_element_type=F32)
    if k_gain is not None:
        out = _rms(out, k_gain)
    rowid = lax.broadcasted_iota(jnp.int32, out.shape, 0)
    blk = rowid // step if kvh_minor else rowid % N_CMP_PAD
    return jnp.where(blk == N_CMP_PAD - 1, 0.0, out)


def _posb_kernel(pos_ref, w1_ref, o_ref):
    o_ref[...] = jnp.dot(pos_ref[...].astype(BF16), w1_ref[...], preferred_element_type=F32)


def compress_weights(w1, w2, pos, k_norm_cmp):
    hid = w2.shape[1]
    out = []
    for s in range(2):
        wcat = w1[s].reshape(2, CMP_STRIDE, HEAD_DIM, hid).transpose(1, 2, 0, 3).reshape(CMP_STRIDE * HEAD_DIM, 2 * hid)
        posf = jnp.broadcast_to(pos[s].reshape(1, CMP_BLOCK * HEAD_DIM), (SUBLANE, CMP_BLOCK * HEAD_DIM))
        posb = pl.pallas_call(_posb_kernel, out_shape=jax.ShapeDtypeStruct((SUBLANE, hid), F32),
                              name="cmp_pos_bias")(posf, w1[s].astype(BF16))[0:1]
        out.append((wcat.astype(BF16), posb, w2[s].astype(BF16)))
    (wck, posk, w2k), (wcv, posv, w2v) = out
    return [wck, wcv, posk, posv, w2k, w2v, k_norm_cmp.reshape(1, HEAD_DIM)]


def compress_cache(pages, page_table, lane_block, cmp_w):
    bsz, n_pages = page_table.shape
    assert n_pages * CMP_PER_PAGE == N_CMP_PAD
    full = lambda a: pl.BlockSpec(a.shape, lambda b, p, pt: (0,) * a.ndim)
    n_in = 2 * NSA_KV_HEADS

    def page_map(lane_blk, b, p, pt):
        return (pt[b, p], 0, lane_blk)

    ins = cmp_w
    out_spec = pl.BlockSpec((None, NSA_KV_HEADS, N_CMP_PAD, HEAD_DIM), lambda b, p, pt: (b, 0, 0, 0))
    out_sds = jax.ShapeDtypeStruct((bsz, NSA_KV_HEADS, N_CMP_PAD, HEAD_DIM), BF16)
    return pl.pallas_call(
        _compress_kernel,
        grid_spec=pltpu.PrefetchScalarGridSpec(
            num_scalar_prefetch=1,
            grid=(bsz, n_pages),
            in_specs=[pl.BlockSpec((None, PAGE, HEAD_DIM), functools.partial(page_map, lane_block + j))
                      for j in range(n_in)] + [full(a) for a in ins],
            out_specs=[out_spec, out_spec],
            scratch_shapes=[pltpu.VMEM((2, NSA_KV_HEADS * N_CMP_PAD, CMP_STRIDE * HEAD_DIM), F32)],
        ),
        out_shape=[out_sds, out_sds],
        compiler_params=_cparams("parallel", "arbitrary"),
        name="compress_cache",
    )(page_table, *([pages] * n_in), *ins)


def _heads_to_rows(q_ref):
    return jnp.concatenate([q_ref[:, g * HEAD_DIM:(g + 1) * HEAD_DIM] for g in range(NSA_GROUP)], axis=0).astype(BF16)


def _dot3_lhs(x, rhs_bf16):
    hi = x.astype(BF16)
    r1 = x - hi.astype(F32)
    mid = r1.astype(BF16)
    lo = (r1 - mid.astype(F32)).astype(BF16)
    acc = jnp.dot(hi, rhs_bf16, preferred_element_type=F32)
    acc += jnp.dot(mid, rhs_bf16, preferred_element_type=F32)
    acc += jnp.dot(lo, rhs_bf16, preferred_element_type=F32)
    return acc


def _selected_key_mask(imp, expand, qpos, kpos, n_sel, key_major=False, qpos_row=None):
    r = imp.shape[0]
    lane = lax.broadcasted_iota(jnp.int32, (r, LANE), 1)
    lag = qpos // SEL_BLOCK - lane
    forced = (lane == 0) | ((lag >= 0) & (lag < SEL_LOCAL))
    score = jnp.where(lag >= 0, imp + jnp.where(forced, SEL_FORCE, 0.0), SEL_INVALID)
    cnt = jnp.zeros((r, LANE), jnp.int32)
    for j in range(n_sel):
        col = score[:, j:j + 1]
        cnt += jnp.where(col > score, 1, jnp.where(col == score, jnp.where(lane > j, 1, 0), 0))
    sel = jnp.where(cnt < min(SEL_TOPK, n_sel), jnp.where(lane < n_sel, 1.0, 0.0), 0.0)
    if key_major:
        ex = _nt_dot(expand, sel.astype(BF16))
        keep = jnp.where(kpos <= qpos_row, ex, 0.0) > 0.5
    else:
        ex = jnp.dot(sel.astype(BF16), expand, preferred_element_type=F32)
        keep = jnp.where(kpos <= qpos, ex, 0.0) > 0.5
    return jnp.where(keep, 0.0, NEG_INF)


def _cmp_topk_kernel(q0, n_sel, q_ref, kc_ref, vc_ref, bias_ref, ovl_ref, exp_ref, ocmp_ref, mask_ref):
    qi = pl.program_id(2)
    tq = q_ref.shape[0]
    q4 = _heads_to_rows(q_ref)
    bias = bias_ref[...].reshape(NSA_GROUP * tq, N_CMP_PAD)
    s = _nt_dot(q4, kc_ref[...]) + bias
    m = jnp.max(s, axis=-1, keepdims=True)
    e = jnp.where(bias > MASKED, jnp.exp(s - m), 0.0)
    p = e / jnp.maximum(jnp.sum(e, axis=-1, keepdims=True), 1e-30)
    o = jnp.dot(p.astype(BF16), vc_ref[...], preferred_element_type=F32)
    psum = jnp.zeros((tq, N_CMP_PAD), F32)
    for g in range(NSA_GROUP):
        ocmp_ref[:, g * HEAD_DIM:(g + 1) * HEAD_DIM] = o[g * tq:(g + 1) * tq]
        psum += p[g * tq:(g + 1) * tq]
    qpos = q0 + qi * tq + lax.broadcasted_iota(jnp.int32, (tq, 1), 0)
    qpos_row = q0 + qi * tq + lax.broadcasted_iota(jnp.int32, (1, tq), 1)
    kpos = lax.broadcasted_iota(jnp.int32, (mask_ref.shape[0], 1), 0)
    imp = _dot3_lhs(psum, ovl_ref[...])
    mask_ref[...] = _selected_key_mask(imp, exp_ref[...], qpos, kpos, n_sel, key_major=True,
                                       qpos_row=qpos_row).astype(mask_ref.dtype)


def cmp_attention_topk(z16, kc, vc, bias_c, bsz, seq, q0, key_len, tq_pref=256):
    n = bsz * seq
    tq = _tile(seq, tq_pref)
    nq = seq // tq
    q_blk = COL_Q // (NSA_GROUP * HEAD_DIM)
    n_sel = -(-key_len // SEL_BLOCK)
    lp = -(-(n_sel * SEL_BLOCK) // LANE) * LANE
    ci = np.arange(N_CMP_PAD)[:, None] * CMP_STRIDE
    sj = np.arange(LANE)[None, :] * SEL_BLOCK
    overlap = jnp.asarray(((ci < sj + SEL_BLOCK) & (ci + CMP_BLOCK > sj)).astype(np.float32), BF16)
    expand = jnp.asarray((np.arange(lp)[:, None] // SEL_BLOCK == np.arange(LANE)[None, :]).astype(np.float32), BF16)
    return pl.pallas_call(
        functools.partial(_cmp_topk_kernel, q0, n_sel),
        grid=(bsz, NSA_KV_HEADS, nq),
        in_specs=[
            pl.BlockSpec((tq, NSA_GROUP * HEAD_DIM), lambda b, k, i: (b * nq + i, q_blk + k)),
            pl.BlockSpec((None, None, N_CMP_PAD, HEAD_DIM), lambda b, k, i: (b, k, 0, 0)),
            pl.BlockSpec((None, None, N_CMP_PAD, HEAD_DIM), lambda b, k, i: (b, k, 0, 0)),
            pl.BlockSpec((NSA_GROUP, tq, N_CMP_PAD), lambda b, k, i: (k, i, 0)),
            pl.BlockSpec((N_CMP_PAD, LANE), lambda b, k, i: (0, 0)),
            pl.BlockSpec((lp, LANE), lambda b, k, i: (0, 0)),
        ],
        out_specs=[
            pl.BlockSpec((tq, NSA_GROUP * HEAD_DIM), lambda b, k, i: (b * nq + i, k)),
            pl.BlockSpec((None, None, lp, tq), lambda b, k, i: (b, k, 0, i)),
        ],
        out_shape=[jax.ShapeDtypeStruct((n, NSA_HEADS * HEAD_DIM), F32),
                   jax.ShapeDtypeStruct((bsz, NSA_KV_HEADS, lp, seq), BF16)],
        compiler_params=_cparams("parallel", "parallel", "parallel"),
        name="cmp_attention_topk",
    )(z16, kc, vc, bias_c, overlap, expand)


def _softmax_step_km(q4, k, v, bias, carry):
    m, l, acc = carry
    s = _nt_dot(k, q4) + bias
    m_new = jnp.maximum(m, jnp.max(s, axis=0, keepdims=True))
    alpha = jnp.exp(m - m_new)
    pr = jnp.exp(s - m_new)
    l = alpha * l + jnp.sum(pr, axis=0, keepdims=True)
    acc = alpha * acc + jnp.dot(v.T, pr.astype(BF16), preferred_element_type=F32)
    return m_new, l, acc


def _nsa_prompt_kernel(q_ref, ks_ref, vs_ref, kw_ref, vw_ref, bias_ref, mask_ref, ocmp_ref, gate_ref, o_ref):
    qi = pl.program_id(2)
    tq = q_ref.shape[0]
    ck = 2 * tq
    rows = NSA_GROUP * tq
    q4 = _heads_to_rows(q_ref)
    n_win = WINDOW // tq

    def bias_pair(kc, far):
        tiles = []
        for half in range(2):
            delta = qi - 2 * kc - half
            if far is not None:
                delta = jnp.minimum(delta, far)
            off = pl.multiple_of((delta + 1) * tq, tq)
            tile = bias_ref[:, pl.ds(off, tq), :]
            tiles.append(jnp.concatenate([tile[g] for g in range(NSA_GROUP)], axis=1))
        return jnp.concatenate(tiles, axis=0)

    def kv_chunk(k_ref, v_ref, kc):
        off = pl.multiple_of(kc * ck, ck)
        return k_ref[pl.ds(off, ck), :], v_ref[pl.ds(off, ck), :]

    init = (jnp.full((1, rows), NEG_INF, F32), jnp.zeros((1, rows), F32), jnp.zeros((HEAD_DIM, rows), F32))

    def sel_body(kc, carry):
        off = pl.multiple_of(kc * ck, ck)
        mb = mask_ref[pl.ds(off, ck), :].astype(F32)
        bias = bias_pair(kc, 2) + jnp.concatenate([mb] * NSA_GROUP, axis=1)
        k, v = kv_chunk(ks_ref, vs_ref, kc)
        return _softmax_step_km(q4, k, v, bias, carry)

    def win_body(kc, carry):
        k, v = kv_chunk(kw_ref, vw_ref, kc)
        return _softmax_step_km(q4, k, v, bias_pair(kc, None), carry)

    win_first = jnp.maximum(qi - n_win, 0) // 2
    sel_carry = lax.fori_loop(0, win_first, sel_body, init)
    (_, l_s, acc_s), (_, l_w, acc_w) = lax.fori_loop(
        win_first, qi // 2 + 1, lambda kc, c: (sel_body(kc, c[0]), win_body(kc, c[1])), (sel_carry, init))
    o_sel = acc_s / l_s
    o_win = acc_w / l_w

    gates = gate_ref[...]
    for g in range(NSA_GROUP):
        sl = slice(g * HEAD_DIM, (g + 1) * HEAD_DIM)
        rs = slice(g * tq, (g + 1) * tq)
        o = (gates[:, 3 * g:3 * g + 1] * ocmp_ref[:, sl] + gates[:, 3 * g + 1:3 * g + 2] * o_sel[:, rs].T
             + gates[:, 3 * g + 2:3 * g + 3] * o_win[:, rs].T)
        o_ref[:, sl] = o.astype(o_ref.dtype)


def nsa_prompt_attention(z16, z32, bias5, mask, o_cmp, bsz, seq):
    n = bsz * seq
    tq = PAGE
    nq = seq // tq
    grp = NSA_GROUP * HEAD_DIM
    seq_col = lambda col: pl.BlockSpec((seq, HEAD_DIM), lambda b, k, i: (b, col // HEAD_DIM + k))
    return pl.pallas_call(
        _nsa_prompt_kernel,
        grid=(bsz, NSA_KV_HEADS, nq),
        in_specs=[
            pl.BlockSpec((tq, grp), lambda b, k, i: (b * nq + i, COL_Q // grp + k)),
            seq_col(COL_KV + 2 * KV_LANES), seq_col(COL_KV + 3 * KV_LANES),
            seq_col(COL_WIN), seq_col(COL_WIN + KV_LANES),
            pl.BlockSpec((NSA_GROUP, bias5.shape[1], tq), lambda b, k, i: (k, 0, 0)),
            pl.BlockSpec((None, None, seq, tq), lambda b, k, i: (b, k, 0, i)),
            pl.BlockSpec((tq, grp), lambda b, k, i: (b * nq + i, k)),
            pl.BlockSpec((tq, LANE), lambda b, k, i: (b * nq + i, COL_GATE // LANE + k)),
        ],
        out_specs=pl.BlockSpec((tq, grp), lambda b, k, i: (b * nq + i, k)),
        out_shape=jax.ShapeDtypeStruct((n, NSA_HEADS * HEAD_DIM), BF16),
        compiler_params=_cparams("parallel", "parallel", "arbitrary"),
        name="nsa_prompt_attention",
    )(z16, z16, z16, z16, z16, bias5, mask, o_cmp, z32)


ROWS_PER_TOKEN = 4 * NSA_KV_HEADS


def _softmax_attend(q, k, v, bias):
    s = _nt_dot(q, k) + bias
    e = jnp.exp(s - jnp.max(s, axis=-1, keepdims=True))
    o = jnp.dot(e.astype(BF16), v, preferred_element_type=F32)
    return o / jnp.sum(e, axis=-1, keepdims=True), e


PAGE_BUFFERS = 16


def _low_half_mask():
    return lax.broadcasted_iota(jnp.int32, (SUBLANE, HEAD_DIM), 0) < NSA_KV_HEADS


def _split_token_pairs(toks):
    lo = _low_half_mask()
    a = [jnp.where(lo, toks[i], pltpu.roll(toks[i + 1], NSA_KV_HEADS, 0)) for i in (0, 2)]
    b = [jnp.where(lo, pltpu.roll(toks[i], NSA_KV_HEADS, 0), toks[i + 1]) for i in (0, 2)]
    return jnp.concatenate(a, axis=0).astype(BF16), jnp.concatenate(b, axis=0).astype(BF16)


def _nsa_sample_kernel(n_pages, n_sel, pt_ref, q_ref, pages_hbm, knew_ref, wst_ref, wnew_ref, bcmp_ref, bsel_ref,
                       bwin_ref, kpos_ref, wck_ref, wcv_ref, posk_ref, posv_ref, w2k_ref, w2v_ref, kn_ref, ovl_ref,
                       exp_ref, gate_ref, o_ref, wout_ref, xs_ref, ks_ref, vs_ref, kw_ref, vw_ref, pbuf_ref, sem_ref):
    b = pl.program_id(0)
    n_total = pl.num_programs(0) * n_pages
    n_past = n_pages * PAGE
    t = q_ref.shape[0]
    rpt = ROWS_PER_TOKEN
    kvh_n = NSA_KV_HEADS

    def page_copy(g):
        slot = g % PAGE_BUFFERS
        return pltpu.make_async_copy(pages_hbm.at[pt_ref[g // n_pages, g % n_pages]], pbuf_ref.at[slot],
                                     sem_ref.at[slot])

    @pl.when(b == 0)
    def _():
        for g in range(PAGE_BUFFERS):
            page_copy(g).start()

    def page_body(pg, carry):
        g = b * n_pages + pg
        page_copy(g).wait()
        page_ref = pbuf_ref.at[g % PAGE_BUFFERS]
        row0 = pl.multiple_of(pg * (CMP_PER_PAGE * kvh_n), CMP_PER_PAGE * kvh_n)
        for r in range(CMP_STRIDE):
            tl = [page_ref[pl.ds((CMP_STRIDE * hb + r) * rpt, SUBLANE), :] for hb in range(CMP_PER_PAGE)]
            for w in range(CMP_PER_PAGE // 4):
                xk, xv = _split_token_pairs(tl[4 * w:4 * w + 4])
                xs_ref[0, pl.ds(row0 + 16 * w, 16), r * HEAD_DIM:(r + 1) * HEAD_DIM] = xk
                xs_ref[1, pl.ds(row0 + 16 * w, 16), r * HEAD_DIM:(r + 1) * HEAD_DIM] = xv
        k0 = pl.multiple_of(pg * (PAGE * kvh_n), PAGE * kvh_n)
        for w in range(PAGE // 4):
            tl = [page_ref[pl.ds((4 * w + i) * rpt + SUBLANE, SUBLANE), :] for i in range(4)]
            kk, vv = _split_token_pairs(tl)
            ks_ref[pl.ds(k0 + 16 * w, 16), :] = kk
            vs_ref[pl.ds(k0 + 16 * w, 16), :] = vv

        @pl.when(g + PAGE_BUFFERS < n_total)
        def _():
            page_copy(g + PAGE_BUFFERS).start()

        return carry

    lax.fori_loop(0, n_pages, page_body, 0)
    _nsa_sample_finish(n_past, n_sel, t, q_ref, knew_ref, wst_ref, wnew_ref, bcmp_ref, bsel_ref, bwin_ref, kpos_ref,
                       wck_ref, wcv_ref, posk_ref, posv_ref, w2k_ref, w2v_ref, kn_ref, ovl_ref, exp_ref, gate_ref,
                       o_ref, xs_ref, ks_ref, vs_ref, kw_ref, vw_ref)

    rows_tok = 2 * kvh_n
    keep_rows = wst_ref.shape[0] - t * rows_tok
    wout_ref[0:keep_rows, :] = wst_ref[t * rows_tok:, :]
    wnew = wnew_ref[...]
    for j in range(rows_tok):
        wout_ref[pl.ds(keep_rows + j, t, stride=rows_tok), :] = wnew[:, j * HEAD_DIM:(j + 1) * HEAD_DIM]


def _nsa_sample_finish(n_past, n_sel, t, q_ref, knew_ref, wst_ref, wnew_ref, bcmp_ref, bsel_ref, bwin_ref, kpos_ref,
                       wck_ref, wcv_ref, posk_ref, posv_ref, w2k_ref, w2v_ref, kn_ref, ovl_ref, exp_ref, gate_ref,
                       o_ref, xs_ref, ks_ref, vs_ref, kw_ref, vw_ref):
    kvh_n = NSA_KV_HEADS

    def new_rows(x):
        rows = jnp.concatenate([x[:, k * HEAD_DIM:(k + 1) * HEAD_DIM] for k in range(kvh_n)], axis=0)
        return jnp.concatenate([rows, jnp.zeros((PAGE * kvh_n - rows.shape[0], HEAD_DIM), F32)], axis=0).astype(BF16)

    knew = knew_ref[...]
    ks_ref[kvh_n * n_past:kvh_n * (n_past + PAGE), :] = new_rows(knew[:, :KV_LANES])
    vs_ref[kvh_n * n_past:kvh_n * (n_past + PAGE), :] = new_rows(knew[:, KV_LANES:])

    q = q_ref[...]
    qall = jnp.concatenate([q[:, h * HEAD_DIM:(h + 1) * HEAD_DIM] for h in range(NSA_HEADS)], axis=0).astype(BF16)

    kc = _compress_finish(xs_ref[0], wck_ref[...], posk_ref[...], w2k_ref[...], kn_ref[...], kvh_minor=True)
    vc = _compress_finish(xs_ref[1], wcv_ref[...], posv_ref[...], w2v_ref[...], None, kvh_minor=True)
    o_cmp, e_c = _softmax_attend(qall, kc.astype(BF16), vc.astype(BF16), bcmp_ref[...])
    p_c = e_c / jnp.sum(e_c, axis=-1, keepdims=True)
    imp_h = _dot3_lhs(p_c, ovl_ref[...])
    imp = jnp.concatenate(
        [sum(imp_h[(k * NSA_GROUP + g) * t:(k * NSA_GROUP + g + 1) * t] for g in range(NSA_GROUP))
         for k in range(kvh_n)], axis=0)
    qpos = n_past + lax.broadcasted_iota(jnp.int32, (kvh_n * t, 1), 0) % t
    mb = _selected_key_mask(imp, exp_ref[...], qpos, kpos_ref[...], n_sel)
    mb = jnp.concatenate([mb[k * t:(k + 1) * t] for k in range(kvh_n) for _ in range(NSA_GROUP)], axis=0)

    o_sel, _ = _softmax_attend(qall, ks_ref[...], vs_ref[...], bsel_ref[...] + mb)

    n_st = wst_ref.shape[0] // (2 * kvh_n)

    def win_body(w, carry):
        base = pl.multiple_of(w * 32, 32)
        tl = [wst_ref[pl.ds(base + SUBLANE * i, SUBLANE), :] for i in range(4)]
        kk, vv = _split_token_pairs(tl)
        dst = pl.multiple_of(w * 16, 16)
        kw_ref[pl.ds(dst, 16), :] = kk
        vw_ref[pl.ds(dst, 16), :] = vv
        return carry

    lax.fori_loop(0, n_st // 4, win_body, 0, unroll=4)
    wnew = wnew_ref[...]
    kw_ref[kvh_n * n_st:kvh_n * (n_st + PAGE), :] = new_rows(wnew[:, :KV_LANES])
    vw_ref[kvh_n * n_st:kvh_n * (n_st + PAGE), :] = new_rows(wnew[:, KV_LANES:])
    o_win, _ = _softmax_attend(qall, kw_ref[...], vw_ref[...], bwin_ref[...])

    gates = gate_ref[...]
    for h in range(NSA_HEADS):
        kvh, g = divmod(h, NSA_GROUP)
        rs = slice(h * t, (h + 1) * t)
        gc = kvh * LANE + 3 * g
        o_ref[:, h * HEAD_DIM:(h + 1) * HEAD_DIM] = (
            gates[:, gc:gc + 1] * o_cmp[rs] + gates[:, gc + 1:gc + 2] * o_sel[rs] + gates[:, gc + 2:gc + 3] * o_win[rs])


def _sample_column_tables(bias_cmp, bias_sel, bias_win, seq, n_past, n_win):
    kvh_n = NSA_KV_HEADS
    row_kvh = np.arange(NSA_HEADS * seq) // (NSA_GROUP * seq)

    def columns(n_old):
        c = np.arange((n_old + PAGE) * kvh_n)
        old = c < n_old * kvh_n
        cn = c - n_old * kvh_n
        valid = old | (cn < kvh_n * seq)
        pos = np.where(old, c // kvh_n, n_old + cn % seq)
        kvh = np.where(old, c % kvh_n, cn // seq)
        return np.where(valid, pos, 0), kvh, valid

    def widen(bias, pos, kvh, valid):
        ok = valid[None, :] & (kvh[None, :] == row_kvh[:, None])
        return jnp.where(jnp.asarray(ok), jnp.take(bias, jnp.asarray(pos), axis=1), NEG_INF)

    cc = np.arange(N_CMP_PAD * kvh_n)
    b_cmp = widen(bias_cmp, cc // kvh_n, cc % kvh_n, np.ones_like(cc, bool))
    pos_s, kvh_s, valid_s = columns(n_past)
    b_sel = widen(bias_sel, pos_s, kvh_s, valid_s)
    pos_w, kvh_w, valid_w = columns(n_win)
    b_win = widen(bias_win, pos_w, kvh_w, valid_w)
    kpos = jnp.asarray(np.where(valid_s, pos_s, np.iinfo(np.int32).max)[None, :].astype(np.int32))
    ci = np.arange(N_CMP_PAD)[:, None] * CMP_STRIDE
    sj = np.arange(LANE)[None, :] * SEL_BLOCK
    overlap = np.repeat(((ci < sj + SEL_BLOCK) & (ci + CMP_BLOCK > sj)).astype(np.float32), kvh_n, axis=0)
    expand = ((pos_s[None, :] // SEL_BLOCK == np.arange(LANE)[:, None]) & valid_s[None, :]).astype(np.float32)
    return b_cmp, b_sel, b_win, kpos, jnp.asarray(overlap, BF16), jnp.asarray(expand, BF16)


def nsa_sample_attention(z32, pages, page_table, win_state, bias_cmp, bias_sel, bias_win, cmp_w, seq):
    bsz, n_pages = page_table.shape
    n = bsz * seq
    width = NSA_HEADS * HEAD_DIM
    kvh_n = NSA_KV_HEADS
    n_past = n_pages * PAGE
    n_win = win_state.shape[1] // (2 * kvh_n)
    n_sel = -(-(n_past + seq) // SEL_BLOCK)
    assert bsz * n_pages >= PAGE_BUFFERS and bias_sel.shape[1] == n_past + PAGE
    b_cmp, b_sel, b_win, kpos, overlap, expand = _sample_column_tables(bias_cmp, bias_sel, bias_win, seq, n_past, n_win)
    cst = lambda a: pl.BlockSpec(a.shape, lambda b, pt: (0,) * a.ndim, pipeline_mode=pl.Buffered(1))
    per_seq = lambda w, col: pl.BlockSpec((seq, w), lambda b, pt: (b, col // w))
    consts = [b_cmp, b_sel, b_win, kpos] + list(cmp_w) + [overlap, expand]
    return pl.pallas_call(
        functools.partial(_nsa_sample_kernel, n_pages, n_sel),
        grid_spec=pltpu.PrefetchScalarGridSpec(
            num_scalar_prefetch=1,
            grid=(bsz,),
            in_specs=[per_seq(width, COL_Q),
                      pl.BlockSpec(memory_space=pl.ANY),
                      per_seq(2 * KV_LANES, COL_KV + 2 * KV_LANES),
                      pl.BlockSpec((None, win_state.shape[1], HEAD_DIM), lambda b, pt: (b, 0, 0)),
                      per_seq(2 * KV_LANES, COL_WIN)]
                     + [cst(a) for a in consts] + [per_seq(kvh_n * LANE, COL_GATE)],
            out_specs=[per_seq(width, 0),
                       pl.BlockSpec((None, win_state.shape[1], HEAD_DIM), lambda b, pt: (b, 0, 0))],
            scratch_shapes=[pltpu.VMEM((2, kvh_n * N_CMP_PAD, CMP_STRIDE * HEAD_DIM), BF16),
                            pltpu.VMEM((kvh_n * (n_past + PAGE), HEAD_DIM), BF16),
                            pltpu.VMEM((kvh_n * (n_past + PAGE), HEAD_DIM), BF16),
                            pltpu.VMEM((kvh_n * (n_win + PAGE), HEAD_DIM), BF16),
                            pltpu.VMEM((kvh_n * (n_win + PAGE), HEAD_DIM), BF16),
                            pltpu.VMEM((PAGE_BUFFERS, PAGE * ROWS_PER_TOKEN, HEAD_DIM), F32),
                            pltpu.SemaphoreType.DMA((PAGE_BUFFERS,))],
        ),
        out_shape=[jax.ShapeDtypeStruct((n, width), F32), jax.ShapeDtypeStruct(win_state.shape, win_state.dtype)],
        compiler_params=_cparams("arbitrary"),
        name="nsa_sample_attention",
    )(page_table, z32, pages, z32, win_state, z32, *consts, z32)


SCALE = HEAD_DIM ** -0.5


def _layer_weights(i, ffn1_w_gate, ffn1_w_up, ffn1_w_down, w_in, w_out, nsa_q_norm, nsa_k_norm,
                   ffn2_w_gate, ffn2_w_up, ffn2_w_down, ple_w_gate, ple_w_proj):
    wi = w_in[i]
    wgate = wi[:, COL_GATE:].reshape(-1, NSA_KV_HEADS, 3 * NSA_GROUP)
    wgate = jnp.pad(wgate, ((0, 0), (0, 0), (0, LANE - 3 * NSA_GROUP))).reshape(-1, NSA_KV_HEADS * LANE)
    w_mix = jnp.concatenate([wi[:, :COL_GATE], wgate], axis=1).astype(BF16)
    t_q, t_ks, t_kw = COL_Q // PROJ_TILE, (COL_KV + 2 * KV_LANES) // PROJ_TILE, COL_WIN // PROJ_TILE
    n_q = NSA_WIDTH // PROJ_TILE
    gains = jnp.ones((Z_COLS // PROJ_TILE, 1, HEAD_DIM), F32)
    gains = gains.at[t_q:t_q + n_q].set(nsa_q_norm[i].astype(F32) * SCALE)
    gains = gains.at[t_ks].set(nsa_k_norm[i, 1].astype(F32)).at[t_kw].set(nsa_k_norm[i, 2].astype(F32))
    return dict(
        ffn1=(ffn1_w_gate[i].astype(BF16), ffn1_w_up[i].astype(BF16), ffn1_w_down[i].astype(BF16)),
        ffn2=(ffn2_w_gate[i].astype(BF16), ffn2_w_up[i].astype(BF16), ffn2_w_down[i].astype(BF16)),
        w_mix=w_mix, mix_gains=gains, norm_tiles=tuple(range(t_q, t_q + n_q)) + (t_ks, t_kw),
        gate_tile=COL_GATE // PROJ_TILE,
        wo_h=w_out[i, :HGRN_WIDTH].astype(BF16), wo_n=w_out[i, HGRN_WIDTH:].astype(BF16),
        ple_gate=ple_w_gate[i].astype(BF16), ple_proj=ple_w_proj[i].astype(BF16),
    )


def _run_layer(i, w, x, pemb, bsz, seq, s0, nsa_fn, norms, hgrn_lb):
    ffn1_norm, mix_norm, hgrn_out_norm, ffn2_norm, ple_norm, ple_post_norm = norms
    x1 = ffn_residual(x, ffn1_norm[i], *w["ffn1"])
    z32, z16 = mixer_project(x1, mix_norm[i], w["w_mix"], w["mix_gains"], w["norm_tiles"], w["gate_tile"])
    o_h, s_fin = hgrn2_mix(z32, hgrn_lb, hgrn_out_norm[i], i, bsz, seq, HGRN_HEADS, s0=s0,
                           heads_per_step=HGRN_HEADS if seq % 16 else 4)
    o_n, win_state_new = nsa_fn(z32, z16)
    x2 = out_project_residual(x1, o_h, o_n, w["wo_h"], w["wo_n"])
    x3 = ffn_residual(x2, ffn2_norm[i], *w["ffn2"])
    y = ple_residual(x3, ple_norm[i], w["ple_gate"], pemb, w["ple_proj"], ple_post_norm[i])
    win_rows = z32[:, COL_WIN:COL_GATE] if win_state_new is None else win_state_new
    return y, z32[:, COL_KV:COL_WIN], win_rows, s_fin


def kernel(x_prompt, x_sample, cache_kv, state_win_kv, state_hgrn, page_table, p_prompt, p_sample, ffn1_norm, ffn1_w_gate, ffn1_w_up, ffn1_w_down, mix_norm, w_in, w_out, hgrn_lb, hgrn_out_norm, nsa_q_norm, nsa_k_norm, cmp_pos, cmp_w1, cmp_w2, rel_bias_table, ffn2_norm, ffn2_w_gate, ffn2_w_up, ffn2_w_down, ple_norm, ple_w_gate, ple_w_proj, ple_post_norm):
    depth = cache_kv.shape[0]
    bp, tp, d = x_prompt.shape
    bs, ts, _ = x_sample.shape
    n_pool = cache_kv.shape[1]
    n_pages = page_table.shape[1]
    past = n_pages * PAGE
    win_keep = state_win_kv.shape[2]
    assert tp % PAGE == 0 and tp >= WINDOW and win_keep == WINDOW
    norms = (ffn1_norm, mix_norm, hgrn_out_norm, ffn2_norm, ple_norm, ple_post_norm)
    table = rel_bias_table.astype(F32)

    n_cmp = (tp - CMP_BLOCK) // CMP_STRIDE + 1
    bias_pc = rel_bias(table, tp, N_CMP_PAD, 0, CMP_BLOCK - 1, CMP_STRIDE, n_cmp)
    bias_p5 = rel_bias(table, WINDOW + 3 * PAGE, PAGE, -PAGE, 0, 1, PAGE, window=WINDOW,
                       key_major_tiles=True)
    n_cmp_s = (past + ts - CMP_BLOCK) // CMP_STRIDE + 1
    assert n_cmp_s <= N_CMP_PAD - 1 and (n_cmp_s - 1) * CMP_STRIDE + CMP_BLOCK <= past
    sel_cols = past + PAGE
    bias_sc = rel_bias(table, ts, N_CMP_PAD, past, CMP_BLOCK - 1, CMP_STRIDE, n_cmp_s).reshape(NSA_HEADS * ts, N_CMP_PAD)
    bias_ss = rel_bias(table, ts, sel_cols, past, 0, 1, past + ts).reshape(NSA_HEADS * ts, sel_cols)
    bias_sw = rel_bias(table, ts, win_keep + PAGE, win_keep, 0, 1, win_keep + ts, window=WINDOW)
    bias_sw = bias_sw.reshape(NSA_HEADS * ts, win_keep + PAGE)

    xp = x_prompt.reshape(bp * tp, d)
    xs = x_sample.reshape(bs * ts, d)
    outs = [[] for _ in range(6)]
    for i in range(depth):
        w = _layer_weights(i, ffn1_w_gate, ffn1_w_up, ffn1_w_down, w_in, w_out, nsa_q_norm, nsa_k_norm,
                           ffn2_w_gate, ffn2_w_up, ffn2_w_down, ple_w_gate, ple_w_proj)
        cmp_w = compress_weights(cmp_w1[i], cmp_w2[i], cmp_pos[i], nsa_k_norm[i, 0])

        def nsa_prompt(z32, z16):
            pages = z32.reshape(bp * tp // PAGE, PAGE, Z_COLS)
            pt = jnp.arange(bp * tp // PAGE, dtype=jnp.int32).reshape(bp, tp // PAGE)
            kc, vc = compress_cache(pages, pt, COL_KV // HEAD_DIM, cmp_w)
            o_cmp, mask = cmp_attention_topk(z16, kc, vc, bias_pc, bp, tp, 0, tp)
            return nsa_prompt_attention(z16, z32, bias_p5, mask, o_cmp, bp, tp), None

        def nsa_sample(z32, z16):
            pages = cache_kv[i].reshape(n_pool, PAGE * ROWS_PER_TOKEN, HEAD_DIM)
            wst = state_win_kv[i].reshape(bs, win_keep * 2 * NSA_KV_HEADS, HEAD_DIM)
            return nsa_sample_attention(z32, pages, page_table, wst, bias_sc, bias_ss, bias_sw, cmp_w, ts)

        xp, kv_p, win_p, h_p = _run_layer(i, w, xp, p_prompt[i].reshape(bp * tp, -1), bp, tp, None, nsa_prompt, norms, hgrn_lb)
        xs, kv_s, win_s, h_s = _run_layer(i, w, xs, p_sample[i].reshape(bs * ts, -1), bs, ts, state_hgrn[i], nsa_sample, norms, hgrn_lb)
        outs[0].append(kv_p.reshape(bp, tp, 4, NSA_KV_HEADS, HEAD_DIM))
        outs[1].append(win_p.reshape(bp, tp, 2, NSA_KV_HEADS, HEAD_DIM)[:, -WINDOW:])
        outs[2].append(h_p)
        outs[3].append(kv_s.reshape(bs, ts, 4, NSA_KV_HEADS, HEAD_DIM))
        outs[4].append(win_s.reshape(bs, win_keep, 2, NSA_KV_HEADS, HEAD_DIM))
        outs[5].append(h_s.astype(state_hgrn.dtype))
    return (xp.reshape(bp, tp, d), xs.reshape(bs, ts, d)) + tuple(jnp.stack(o) for o in outs)
```

```python
import functools
import math

import numpy as np
import jax
import jax.numpy as jnp
from jax import lax
from jax.experimental import pallas as pl
from jax.experimental.pallas import tpu as pltpu

F32 = jnp.float32
BF16 = jnp.bfloat16

LANE = 128
SUBLANE = 8
VMEM_LIMIT_BYTES = 56 * 1024 * 1024

HEAD_DIM = 128
RMS_EPS = 1e-6
NEG_INF = -1e30


def _cparams(*sem):
    return pltpu.CompilerParams(dimension_semantics=sem, vmem_limit_bytes=VMEM_LIMIT_BYTES)


def _tile(n, pref):
    if n <= pref:
        return n
    t = pref
    while t >= SUBLANE:
        if n % t == 0:
            return t
        t -= SUBLANE
    return n


def _rms(x, gain):
    ms = jnp.mean(x * x, axis=-1, keepdims=True)
    return x * lax.rsqrt(ms + RMS_EPS) * gain


def _norm_rows_to(x_ref, gain_ref, dst_ref):
    rows = x_ref.shape[0]
    rc = 32 if rows % 32 == 0 else rows

    def body(i, carry):
        r = pl.multiple_of(i * rc, rc)
        dst_ref[pl.ds(r, rc), :] = _rms(x_ref[pl.ds(r, rc), :], gain_ref[...]).astype(dst_ref.dtype)
        return carry

    lax.fori_loop(0, rows // rc, body, 0)


def _ffn_kernel(x_ref, gain_ref, wg_ref, wu_ref, wd_ref, o_ref, hn_ref):
    @pl.when(pl.program_id(1) == 0)
    def _():
        _norm_rows_to(x_ref, gain_ref, hn_ref)
        o_ref[...] = x_ref[...]

    h = hn_ref[...]
    g = jnp.dot(h, wg_ref[...], preferred_element_type=F32)
    u = jnp.dot(h, wu_ref[...], preferred_element_type=F32)
    a = (g * jax.nn.sigmoid(g) * (0.5 * u)).astype(BF16)
    o_ref[...] += jnp.dot(a, wd_ref[...], preferred_element_type=F32)


def ffn_residual(x, gain, wg, wu, wd, tm_pref=512, tf_pref=256):
    n, d = x.shape
    f = wg.shape[1]
    tm, tf = _tile(n, tm_pref), _tile(f, tf_pref)
    return pl.pallas_call(
        _ffn_kernel,
        grid=(n // tm, f // tf),
        in_specs=[
            pl.BlockSpec((tm, d), lambda i, j: (i, 0)),
            pl.BlockSpec((1, d), lambda i, j: (0, 0)),
            pl.BlockSpec((d, tf), lambda i, j: (0, j)),
            pl.BlockSpec((d, tf), lambda i, j: (0, j)),
            pl.BlockSpec((tf, d), lambda i, j: (j, 0)),
        ],
        out_specs=pl.BlockSpec((tm, d), lambda i, j: (i, 0)),
        out_shape=jax.ShapeDtypeStruct((n, d), F32),
        scratch_shapes=[pltpu.VMEM((tm, d), BF16)],
        compiler_params=_cparams("parallel", "arbitrary"),
        name="ffn_residual",
    )(x, gain.reshape(1, d), wg, wu, wd)


PROJ_TILE = 512


def _mixproj_kernel(norm_tiles, gate_tile, kv_tile0, n_slots, x_ref, gain_ref, w_ref, wg_ref, hg_ref,
                    z32_ref, z16_ref, kvr_ref, hn_ref):
    j = pl.program_id(1)
    tm = x_ref.shape[0]

    @pl.when(j == 0)
    def _():
        _norm_rows_to(x_ref, gain_ref, hn_ref)

    def put(val):
        z32_ref[...] = val
        z16_ref[...] = val.astype(BF16)

    @pl.when(j != gate_tile)
    def _():
        acc = jnp.dot(hn_ref[...], w_ref[...], preferred_element_type=F32)
        is_norm = functools.reduce(jnp.logical_or, [j == t for t in norm_tiles])
        gain = hg_ref[0]
        heads = [acc[:, h * HEAD_DIM:(h + 1) * HEAD_DIM] for h in range(acc.shape[1] // HEAD_DIM)]
        heads = [jnp.where(is_norm, _rms(v, gain), v) for v in heads]
        put(jnp.concatenate(heads, axis=1))

        @pl.when(jnp.logical_and(j >= kv_tile0, j < kv_tile0 + n_slots))
        def _():
            row0 = (j - kv_tile0) * len(heads)
            for kvh, v in enumerate(heads):
                kvr_ref[pl.ds(row0 + kvh, tm, stride=n_slots * len(heads)), :] = v

    @pl.when(j == gate_tile)
    def _():
        put(jax.nn.sigmoid(jnp.dot(hn_ref[...], wg_ref[...], preferred_element_type=F32)))


def mixer_project(x, gain, w, w_gate, head_gain, norm_tiles, gate_tile, kv_tile0, n_slots, tm_pref=512):
    n, d = x.shape
    tm, tn = _tile(n, tm_pref), PROJ_TILE
    assert w.shape[1] == gate_tile * tn and w_gate.shape[1] == tn
    ncols = w.shape[1] + tn
    rows_tok = n_slots * (tn // HEAD_DIM)
    return pl.pallas_call(
        functools.partial(_mixproj_kernel, norm_tiles, gate_tile, kv_tile0, n_slots),
        grid=(n // tm, ncols // tn),
        in_specs=[
            pl.BlockSpec((tm, d), lambda i, j: (i, 0)),
            pl.BlockSpec((1, d), lambda i, j: (0, 0)),
            pl.BlockSpec((d, tn), lambda i, j: (0, jnp.minimum(j, gate_tile - 1))),
            pl.BlockSpec((d, tn), lambda i, j: (0, 0), pipeline_mode=pl.Buffered(1)),
            pl.BlockSpec((1, 1, HEAD_DIM), lambda i, j: (j, 0, 0)),
        ],
        out_specs=[pl.BlockSpec((tm, tn), lambda i, j: (i, j)), pl.BlockSpec((tm, tn), lambda i, j: (i, j)),
                   pl.BlockSpec((tm * rows_tok, HEAD_DIM), lambda i, j: (i, 0))],
        out_shape=[jax.ShapeDtypeStruct((n, ncols), F32), jax.ShapeDtypeStruct((n, ncols), BF16),
                   jax.ShapeDtypeStruct((n * rows_tok, HEAD_DIM), F32)],
        scratch_shapes=[pltpu.VMEM((tm, d), BF16)],
        compiler_params=_cparams("parallel", "arbitrary"),
        name="mixer_project",
    )(x, gain.reshape(1, d), w, w_gate, head_gain)


def _outproj_kernel(x_ref, a_ref, b_ref, wa_ref, wb_ref, o_ref):
    acc = jnp.dot(a_ref[...].astype(BF16), wa_ref[...], preferred_element_type=F32)
    acc += jnp.dot(b_ref[...].astype(BF16), wb_ref[...], preferred_element_type=F32)
    o_ref[...] = x_ref[...] + acc


def out_project_residual(x, a, b, wa, wb, tm_pref=512, tn_pref=1024):
    n, d = x.shape
    ka, kb = a.shape[1], b.shape[1]
    tm, tn = _tile(n, tm_pref), _tile(d, tn_pref)
    return pl.pallas_call(
        _outproj_kernel,
        grid=(n // tm, d // tn),
        in_specs=[
            pl.BlockSpec((tm, tn), lambda i, j: (i, j)),
            pl.BlockSpec((tm, ka), lambda i, j: (i, 0)),
            pl.BlockSpec((tm, kb), lambda i, j: (i, 0)),
            pl.BlockSpec((ka, tn), lambda i, j: (0, j)),
            pl.BlockSpec((kb, tn), lambda i, j: (0, j)),
        ],
        out_specs=pl.BlockSpec((tm, tn), lambda i, j: (i, j)),
        out_shape=jax.ShapeDtypeStruct((n, d), F32),
        compiler_params=_cparams("parallel", "arbitrary"),
        name="out_project_residual",
    )(x, a, b, wa, wb)


def _ple_kernel(tn, x_ref, gain_ref, wg_ref, p_ref, wp_ref, pg_ref, o_ref, hn_ref, pe_ref):
    j = pl.program_id(1)

    @pl.when(j == 0)
    def _():
        _norm_rows_to(x_ref, gain_ref, hn_ref)
        pe_ref[...] = jnp.dot(p_ref[...].astype(BF16), wp_ref[...], preferred_element_type=F32)
        _norm_rows_to(pe_ref, pg_ref, pe_ref)

    c = pl.multiple_of(j * tn, LANE)
    gate = jax.nn.sigmoid(jnp.dot(hn_ref[...], wg_ref[...], preferred_element_type=F32))
    o_ref[...] = x_ref[:, pl.ds(c, tn)] + gate * pe_ref[:, pl.ds(c, tn)]


def ple_residual(x, gain, wg, p, wp, post_gain, tm_pref=512, tn_pref=512):
    n, d = x.shape
    pd = p.shape[1]
    tm, tn = _tile(n, tm_pref), _tile(d, tn_pref)
    return pl.pallas_call(
        functools.partial(_ple_kernel, tn),
        grid=(n // tm, d // tn),
        in_specs=[
            pl.BlockSpec((tm, d), lambda i, j: (i, 0)),
            pl.BlockSpec((1, d), lambda i, j: (0, 0)),
            pl.BlockSpec((d, tn), lambda i, j: (0, j)),
            pl.BlockSpec((tm, pd), lambda i, j: (i, 0)),
            pl.BlockSpec((pd, d), lambda i, j: (0, 0)),
            pl.BlockSpec((1, d), lambda i, j: (0, 0)),
        ],
        out_specs=pl.BlockSpec((tm, tn), lambda i, j: (i, j)),
        out_shape=jax.ShapeDtypeStruct((n, d), F32),
        scratch_shapes=[pltpu.VMEM((tm, d), BF16), pltpu.VMEM((tm, d), F32)],
        compiler_params=_cparams("parallel", "arbitrary"),
        name="ple_residual",
    )(x, gain.reshape(1, d), wg, p, wp, post_gain.reshape(1, d))


HGRN_DIAG = SUBLANE


def _split3_dot(lhs_bf16, x):
    hi = x.astype(BF16)
    r1 = x - hi.astype(F32)
    mid = r1.astype(BF16)
    lo = (r1 - mid.astype(F32)).astype(BF16)
    acc = jnp.dot(lhs_bf16, hi, preferred_element_type=F32)
    acc += jnp.dot(lhs_bf16, mid, preferred_element_type=F32)
    acc += jnp.dot(lhs_bf16, lo, preferred_element_type=F32)
    return acc


def _bcast_row_in_blocks(a, bs, row):
    c, k = a.shape
    a3 = a.reshape(c // bs, bs, k)
    return jnp.broadcast_to(a3[:, row:row + 1, :], (c // bs, bs, k)).reshape(c, k)


def _nt_dot(a, b):
    return lax.dot_general(a, b, (((1,), (1,)), ((), ())), preferred_element_type=F32)


def _hgrn_kernel(layer, has_s0, hps, q_ref, f_ref, i_ref, g_ref, lb_ref, og_ref, *rest):
    if has_s0:
        s0_ref, o_ref, sfin_ref, st_ref = rest
    else:
        s0_ref = None
        o_ref, sfin_ref, st_ref = rest
    ci = pl.program_id(2)

    @pl.when(ci == 0)
    def _():
        if s0_ref is not None:
            st_ref[...] = jnp.stack([s0_ref[hh].T for hh in range(hps)])
        else:
            st_ref[...] = jnp.zeros(st_ref.shape, F32)

    outs, states = [], []
    for hh in range(hps):
        ls = slice(hh * HEAD_DIM, (hh + 1) * HEAD_DIM)
        o, st_new = _hgrn_head(layer, q_ref[:, ls], f_ref[:, ls], i_ref[:, ls], g_ref[:, ls], lb_ref[:, ls],
                               og_ref[...], st_ref[hh])
        outs.append(o)
        states.append(st_new)
    o_ref[...] = jnp.concatenate(outs, axis=1).astype(o_ref.dtype)
    st_ref[...] = jnp.stack(states)

    @pl.when(ci == pl.num_programs(2) - 1)
    def _():
        sfin_ref[...] = jnp.stack([s.T for s in states])


def _hgrn_head(layer, q, f_raw, v, gr, lbr, out_gain, st):
    c = q.shape[0]
    e = jnp.exp(lbr - jnp.max(lbr, axis=0, keepdims=True))
    lb = jnp.sum(e[:layer + 1], axis=0, keepdims=True) / jnp.sum(e, axis=0, keepdims=True)

    fg = lb + (1.0 - lb) * jax.nn.sigmoid(f_raw)
    logf = jnp.log(fg)
    kk = 1.0 - fg

    ti = lax.broadcasted_iota(jnp.int32, (c, c), 0)
    si = lax.broadcasted_iota(jnp.int32, (c, c), 1)
    tril = (si <= ti)
    a = _split3_dot(tril.astype(BF16), logf)
    a_last = a[c - 1:c, :]

    kk_b = kk.astype(BF16)
    d = HGRN_DIAG
    xs = [(q * jnp.exp(jnp.minimum(a - _bcast_row_in_blocks(a, d, j), 0.0))).astype(BF16) for j in range(d)]
    res = _nt_dot(jnp.concatenate(xs, axis=0), kk_b)
    attn = jnp.zeros((c, c), F32)
    for j in range(d):
        attn += jnp.where((si % d) == j, res[j * c:(j + 1) * c], 0.0)
    attn = jnp.where(((si // d) == (ti // d)) & tril, attn, 0.0)
    bs = 2 * d
    while bs <= c:
        half = bs // 2
        bnd = _bcast_row_in_blocks(a, bs, half - 1)
        qe = (q * jnp.exp(jnp.minimum(a - bnd, 0.0))).astype(BF16)
        ke = (kk * jnp.exp(jnp.minimum(bnd - a, 0.0))).astype(BF16)
        m = ((si // bs) == (ti // bs)) & ((ti % bs) >= half) & ((si % bs) < half)
        attn += jnp.where(m, _nt_dot(qe, ke), 0.0)
        bs *= 2

    v_b = v.astype(BF16)
    o = jnp.dot(attn.astype(BF16), v_b, preferred_element_type=F32)
    o += _nt_dot((q * jnp.exp(a)).astype(BF16), st.astype(BF16))
    kd = (kk * jnp.exp(a_last - a)).astype(BF16)
    st_new = st * jnp.exp(a_last) + jnp.dot(v_b.T, kd, preferred_element_type=F32)
    return _rms(o, out_gain) * (gr * jax.nn.sigmoid(gr)), st_new


def hgrn2_mix(zh, hgrn_lb, out_gain, layer, batch, seq, heads, s0=None, chunk_pref=128, heads_per_step=4):
    n = batch * seq
    dk = HEAD_DIM
    c = _tile(seq, chunk_pref)
    assert c % HGRN_DIAG == 0 and (c // HGRN_DIAG) & (c // HGRN_DIAG - 1) == 0
    nc = seq // c
    nl = hgrn_lb.shape[0]
    hps = heads_per_step
    hg = heads // hps

    def zspec(sec):
        return pl.BlockSpec((c, hps * dk), lambda b, h, ci: (b * nc + ci, sec * hg + h))

    state_spec = pl.BlockSpec((None, hps, dk, dk), lambda b, h, ci: (b, h, 0, 0))
    in_specs = [zspec(0), zspec(1), zspec(2), zspec(3),
                pl.BlockSpec((nl, hps * dk), lambda b, h, ci: (0, h)),
                pl.BlockSpec((1, dk), lambda b, h, ci: (0, 0))]
    args = [zh, zh, zh, zh, hgrn_lb, out_gain.reshape(1, dk)]
    if s0 is not None:
        in_specs.append(state_spec)
        args.append(s0)
    return pl.pallas_call(
        functools.partial(_hgrn_kernel, layer, s0 is not None, hps),
        grid=(batch, hg, nc),
        in_specs=in_specs,
        out_specs=[pl.BlockSpec((c, hps * dk), lambda b, h, ci: (b * nc + ci, h)), state_spec],
        out_shape=[jax.ShapeDtypeStruct((n, heads * dk), BF16 if c % 16 == 0 else F32),
                   jax.ShapeDtypeStruct((batch, heads, dk, dk), F32)],
        scratch_shapes=[pltpu.VMEM((hps, dk, dk), F32)],
        compiler_params=_cparams("parallel", "parallel", "arbitrary"),
        name="hgrn2_mix",
    )(*args)


NSA_KV_HEADS = 4
NSA_GROUP = 4
NSA_HEADS = NSA_KV_HEADS * NSA_GROUP
KV_LANES = NSA_KV_HEADS * HEAD_DIM
PAGE = 128
CMP_BLOCK = 32
CMP_STRIDE = 16
CMP_PER_PAGE = PAGE // CMP_STRIDE
N_CMP_PAD = 128
SEL_BLOCK = 64
SEL_TOPK = 16
SEL_LOCAL = 2
SEL_FORCE = 1e3
SEL_INVALID = -1e9
WINDOW = 512
N_BUCKETS = 32
MAX_DISTANCE = 128
MASKED = -1e29

HGRN_HEADS = 16
HGRN_WIDTH = HGRN_HEADS * HEAD_DIM
NSA_WIDTH = NSA_HEADS * HEAD_DIM
COL_Q = 4 * HGRN_WIDTH
COL_KV = COL_Q + NSA_WIDTH
COL_WIN = COL_KV + 4 * KV_LANES
COL_GATE = COL_WIN + 2 * KV_LANES
Z_COLS = COL_GATE + NSA_KV_HEADS * LANE


def _bias_kernel(q0, k0, kstride, ncols_valid, window, key_major_tiles, table_ref, o_ref):
    h = pl.program_id(0)
    rt, cols = o_ref.shape
    r = lax.broadcasted_iota(jnp.int32, (rt, cols), 0)
    c = lax.broadcasted_iota(jnp.int32, (rt, cols), 1)
    if key_major_tiles:
        dist = (q0 + pl.program_id(1) * rt + c) - (k0 + r * kstride)
    else:
        dist = (q0 + r + pl.program_id(1) * rt) - (k0 + c * kstride)
    dpos = jnp.maximum(dist, 0)
    max_exact = N_BUCKETS // 2
    log_ratio = jnp.log(jnp.maximum(dpos, 1).astype(F32) / max_exact) / math.log(MAX_DISTANCE / max_exact)
    large = jnp.minimum(max_exact + (log_ratio * (N_BUCKETS - max_exact)).astype(jnp.int32), N_BUCKETS - 1)
    bucket = jnp.where(dpos < max_exact, dpos, large)
    acc = jnp.zeros((rt, cols), F32)
    for b in range(N_BUCKETS):
        acc = jnp.where(bucket == b, table_ref[b, h], acc)
    valid = (dist >= 0) & (c < ncols_valid)
    if window is not None:
        valid = valid & (dist < window)
    o_ref[...] = jnp.where(valid, acc, NEG_INF)


def rel_bias(table, rows, cols, q0, k0, kstride, ncols_valid, window=None, key_major_tiles=False):
    rt = cols if key_major_tiles else _tile(rows, 256)
    return pl.pallas_call(
        functools.partial(_bias_kernel, q0, k0, kstride, ncols_valid, window, key_major_tiles),
        grid=(NSA_HEADS, rows // rt),
        in_specs=[pl.BlockSpec(memory_space=pltpu.SMEM)],
        out_specs=pl.BlockSpec((None, rt, cols), lambda h, i: (h, i, 0)),
        out_shape=jax.ShapeDtypeStruct((NSA_HEADS, rows, cols), F32),
        compiler_params=_cparams("parallel", "parallel"),
        name="rel_bias",
    )(table)


def _compress_kernel(pt_ref, *refs):
    n_in = 2 * NSA_KV_HEADS
    page_refs = refs[:n_in]
    wck_ref, wcv_ref, posk_ref, posv_ref, w2k_ref, w2v_ref, kn_ref, kc_ref, vc_ref, xs_ref = refs[n_in:]
    p = pl.program_id(1)
    for slot in range(2):
        for kvh in range(NSA_KV_HEADS):
            page_ref = page_refs[slot * NSA_KV_HEADS + kvh]
            row0 = pl.multiple_of(kvh * N_CMP_PAD + p * CMP_PER_PAGE, CMP_PER_PAGE)
            for r in range(CMP_STRIDE):
                xs_ref[slot, pl.ds(row0, CMP_PER_PAGE), r * HEAD_DIM:(r + 1) * HEAD_DIM] = (
                    page_ref[pl.ds(r, CMP_PER_PAGE, stride=CMP_STRIDE), :])

    @pl.when(p == pl.num_programs(1) - 1)
    def _():
        kc = _compress_finish(xs_ref[0].astype(BF16), wck_ref[...], posk_ref[...], w2k_ref[...], kn_ref[...])
        vc = _compress_finish(xs_ref[1].astype(BF16), wcv_ref[...], posv_ref[...], w2v_ref[...], None)
        kc_ref[...] = kc.reshape(NSA_KV_HEADS, N_CMP_PAD, HEAD_DIM).astype(kc_ref.dtype)
        vc_ref[...] = vc.reshape(NSA_KV_HEADS, N_CMP_PAD, HEAD_DIM).astype(vc_ref.dtype)


def _compress_finish(x, wcat, posb, w2, k_gain, kvh_minor=False):
    m_rows = x.shape[0]
    hid = w2.shape[0]
    step = NSA_KV_HEADS if kvh_minor else 1
    pq = jnp.dot(x, wcat, preferred_element_type=F32)
    nxt = pltpu.roll(pq[:, hid:], m_rows - step, 0)
    hcur = pq[:, :hid] + nxt + posb
    act = (hcur * jax.nn.sigmoid(hcur)).astype(BF16)
    out = jnp.dot(act, w2, preferred_element_type=F32)
    if k_gain is not None:
        out = _rms(out, k_gain)
    rowid = lax.broadcasted_iota(jnp.int32, out.shape, 0)
    blk = rowid // step if kvh_minor else rowid % N_CMP_PAD
    return jnp.where(blk == N_CMP_PAD - 1, 0.0, out)


def _posb_kernel(pos_ref, w1_ref, o_ref):
    o_ref[...] = jnp.dot(pos_ref[...].astype(BF16), w1_ref[...], preferred_element_type=F32)


def compress_weights(w1, w2, pos, k_norm_cmp):
    hid = w2.shape[1]
    out = []
    for s in range(2):
        wcat = w1[s].reshape(2, CMP_STRIDE, HEAD_DIM, hid).transpose(1, 2, 0, 3).reshape(CMP_STRIDE * HEAD_DIM, 2 * hid)
        posf = jnp.broadcast_to(pos[s].reshape(1, CMP_BLOCK * HEAD_DIM), (SUBLANE, CMP_BLOCK * HEAD_DIM))
        posb = pl.pallas_call(_posb_kernel, out_shape=jax.ShapeDtypeStruct((SUBLANE, hid), F32),
                              name="cmp_pos_bias")(posf, w1[s].astype(BF16))[0:1]
        out.append((wcat.astype(BF16), posb, w2[s].astype(BF16)))
    (wck, posk, w2k), (wcv, posv, w2v) = out
    return [wck, wcv, posk, posv, w2k, w2v, k_norm_cmp.reshape(1, HEAD_DIM)]


def compress_cache(pages, page_table, lane_block, cmp_w):
    bsz, n_pages = page_table.shape
    assert n_pages * CMP_PER_PAGE == N_CMP_PAD
    full = lambda a: pl.BlockSpec(a.shape, lambda b, p, pt: (0,) * a.ndim)
    n_in = 2 * NSA_KV_HEADS

    def page_map(lane_blk, b, p, pt):
        return (pt[b, p], 0, lane_blk)

    ins = cmp_w
    out_spec = pl.BlockSpec((None, NSA_KV_HEADS, N_CMP_PAD, HEAD_DIM), lambda b, p, pt: (b, 0, 0, 0))
    out_sds = jax.ShapeDtypeStruct((bsz, NSA_KV_HEADS, N_CMP_PAD, HEAD_DIM), BF16)
    return pl.pallas_call(
        _compress_kernel,
        grid_spec=pltpu.PrefetchScalarGridSpec(
            num_scalar_prefetch=1,
            grid=(bsz, n_pages),
            in_specs=[pl.BlockSpec((None, PAGE, HEAD_DIM), functools.partial(page_map, lane_block + j))
                      for j in range(n_in)] + [full(a) for a in ins],
            out_specs=[out_spec, out_spec],
            scratch_shapes=[pltpu.VMEM((2, NSA_KV_HEADS * N_CMP_PAD, CMP_STRIDE * HEAD_DIM), F32)],
        ),
        out_shape=[out_sds, out_sds],
        compiler_params=_cparams("parallel", "arbitrary"),
        name="compress_cache",
    )(page_table, *([pages] * n_in), *ins)


def _heads_to_rows(q_ref):
    return jnp.concatenate([q_ref[:, g * HEAD_DIM:(g + 1) * HEAD_DIM] for g in range(NSA_GROUP)], axis=0).astype(BF16)


def _dot3_lhs(x, rhs_bf16):
    hi = x.astype(BF16)
    r1 = x - hi.astype(F32)
    mid = r1.astype(BF16)
    lo = (r1 - mid.astype(F32)).astype(BF16)
    acc = jnp.dot(hi, rhs_bf16, preferred_element_type=F32)
    acc += jnp.dot(mid, rhs_bf16, preferred_element_type=F32)
    acc += jnp.dot(lo, rhs_bf16, preferred_element_type=F32)
    return acc


def _selected_key_mask(imp, expand, qpos, kpos, n_sel, key_major=False, qpos_row=None):
    r = imp.shape[0]
    lane = lax.broadcasted_iota(jnp.int32, (r, LANE), 1)
    lag = qpos // SEL_BLOCK - lane
    forced = (lane == 0) | ((lag >= 0) & (lag < SEL_LOCAL))
    score = jnp.where(lag >= 0, imp + jnp.where(forced, SEL_FORCE, 0.0), SEL_INVALID)
    cnt = jnp.zeros((r, LANE), jnp.int32)
    for j in range(n_sel):
        col = score[:, j:j + 1]
        cnt += jnp.where(col > score, 1, jnp.where(col == score, jnp.where(lane > j, 1, 0), 0))
    sel = jnp.where(cnt < min(SEL_TOPK, n_sel), jnp.where(lane < n_sel, 1.0, 0.0), 0.0)
    if key_major:
        ex = _nt_dot(expand, sel.astype(BF16))
        keep = jnp.where(kpos <= qpos_row, ex, 0.0) > 0.5
    else:
        ex = jnp.dot(sel.astype(BF16), expand, preferred_element_type=F32)
        keep = jnp.where(kpos <= qpos, ex, 0.0) > 0.5
    return jnp.where(keep, 0.0, NEG_INF)


def _cmp_topk_kernel(q0, n_sel, q_ref, kc_ref, vc_ref, bias_ref, ovl_ref, exp_ref, ocmp_ref, mask_ref):
    qi = pl.program_id(2)
    tq = q_ref.shape[0]
    q4 = _heads_to_rows(q_ref)
    bias = bias_ref[...].reshape(NSA_GROUP * tq, N_CMP_PAD)
    s = _nt_dot(q4, kc_ref[...]) + bias
    m = jnp.max(s, axis=-1, keepdims=True)
    e = jnp.where(bias > MASKED, jnp.exp(s - m), 0.0)
    p = e / jnp.maximum(jnp.sum(e, axis=-1, keepdims=True), 1e-30)
    o = jnp.dot(p.astype(BF16), vc_ref[...], preferred_element_type=F32)
    psum = jnp.zeros((tq, N_CMP_PAD), F32)
    for g in range(NSA_GROUP):
        ocmp_ref[:, g * HEAD_DIM:(g + 1) * HEAD_DIM] = o[g * tq:(g + 1) * tq]
        psum += p[g * tq:(g + 1) * tq]
    qpos = q0 + qi * tq + lax.broadcasted_iota(jnp.int32, (tq, 1), 0)
    qpos_row = q0 + qi * tq + lax.broadcasted_iota(jnp.int32, (1, tq), 1)
    kpos = lax.broadcasted_iota(jnp.int32, (mask_ref.shape[0], 1), 0)
    imp = _dot3_lhs(psum, ovl_ref[...])
    mask_ref[...] = _selected_key_mask(imp, exp_ref[...], qpos, kpos, n_sel, key_major=True,
                                       qpos_row=qpos_row).astype(mask_ref.dtype)


def cmp_attention_topk(z16, kc, vc, bias_c, bsz, seq, q0, key_len, tq_pref=256):
    n = bsz * seq
    tq = _tile(seq, tq_pref)
    nq = seq // tq
    q_blk = COL_Q // (NSA_GROUP * HEAD_DIM)
    n_sel = -(-key_len // SEL_BLOCK)
    lp = -(-(n_sel * SEL_BLOCK) // LANE) * LANE
    ci = np.arange(N_CMP_PAD)[:, None] * CMP_STRIDE
    sj = np.arange(LANE)[None, :] * SEL_BLOCK
    overlap = jnp.asarray(((ci < sj + SEL_BLOCK) & (ci + CMP_BLOCK > sj)).astype(np.float32), BF16)
    expand = jnp.asarray((np.arange(lp)[:, None] // SEL_BLOCK == np.arange(LANE)[None, :]).astype(np.float32), BF16)
    return pl.pallas_call(
        functools.partial(_cmp_topk_kernel, q0, n_sel),
        grid=(bsz, NSA_KV_HEADS, nq),
        in_specs=[
            pl.BlockSpec((tq, NSA_GROUP * HEAD_DIM), lambda b, k, i: (b * nq + i, q_blk + k)),
            pl.BlockSpec((None, None, N_CMP_PAD, HEAD_DIM), lambda b, k, i: (b, k, 0, 0)),
            pl.BlockSpec((None, None, N_CMP_PAD, HEAD_DIM), lambda b, k, i: (b, k, 0, 0)),
            pl.BlockSpec((NSA_GROUP, tq, N_CMP_PAD), lambda b, k, i: (k, i, 0)),
            pl.BlockSpec((N_CMP_PAD, LANE), lambda b, k, i: (0, 0)),
            pl.BlockSpec((lp, LANE), lambda b, k, i: (0, 0)),
        ],
        out_specs=[
            pl.BlockSpec((tq, NSA_GROUP * HEAD_DIM), lambda b, k, i: (b * nq + i, k)),
            pl.BlockSpec((None, None, lp, tq), lambda b, k, i: (b, k, 0, i)),
        ],
        out_shape=[jax.ShapeDtypeStruct((n, NSA_HEADS * HEAD_DIM), F32),
                   jax.ShapeDtypeStruct((bsz, NSA_KV_HEADS, lp, seq), BF16)],
        compiler_params=_cparams("parallel", "parallel", "parallel"),
        name="cmp_attention_topk",
    )(z16, kc, vc, bias_c, overlap, expand)


def _softmax_step_km(q4, k, v, bias, carry):
    m, l, acc = carry
    s = _nt_dot(k, q4) + bias
    m_new = jnp.maximum(m, jnp.max(s, axis=0, keepdims=True))
    alpha = jnp.exp(m - m_new)
    pr = jnp.exp(s - m_new)
    l = alpha * l + jnp.sum(pr, axis=0, keepdims=True)
    acc = alpha * acc + jnp.dot(v.T, pr.astype(BF16), preferred_element_type=F32)
    return m_new, l, acc


def _nsa_prompt_kernel(q_ref, ks_ref, vs_ref, kw_ref, vw_ref, bias_ref, mask_ref, ocmp_ref, gate_ref, o_ref):
    qi = pl.program_id(2)
    tq = q_ref.shape[0]
    ck = 2 * tq
    rows = NSA_GROUP * tq
    q4 = _heads_to_rows(q_ref)
    n_win = WINDOW // tq

    def bias_pair(kc, far):
        tiles = []
        for half in range(2):
            delta = qi - 2 * kc - half
            if far is not None:
                delta = jnp.minimum(delta, far)
            off = pl.multiple_of((delta + 1) * tq, tq)
            tile = bias_ref[:, pl.ds(off, tq), :]
            tiles.append(jnp.concatenate([tile[g] for g in range(NSA_GROUP)], axis=1))
        return jnp.concatenate(tiles, axis=0)

    def kv_chunk(k_ref, v_ref, kc):
        off = pl.multiple_of(kc * ck, ck)
        return k_ref[pl.ds(off, ck), :], v_ref[pl.ds(off, ck), :]

    init = (jnp.full((1, rows), NEG_INF, F32), jnp.zeros((1, rows), F32), jnp.zeros((HEAD_DIM, rows), F32))

    def sel_body(kc, carry):
        off = pl.multiple_of(kc * ck, ck)
        mb = mask_ref[pl.ds(off, ck), :].astype(F32)
        bias = bias_pair(kc, 2) + jnp.concatenate([mb] * NSA_GROUP, axis=1)
        k, v = kv_chunk(ks_ref, vs_ref, kc)
        return _softmax_step_km(q4, k, v, bias, carry)

    def win_body(kc, carry):
        k, v = kv_chunk(kw_ref, vw_ref, kc)
        return _softmax_step_km(q4, k, v, bias_pair(kc, None), carry)

    win_first = jnp.maximum(qi - n_win, 0) // 2
    sel_carry = lax.fori_loop(0, win_first, sel_body, init)
    (_, l_s, acc_s), (_, l_w, acc_w) = lax.fori_loop(
        win_first, qi // 2 + 1, lambda kc, c: (sel_body(kc, c[0]), win_body(kc, c[1])), (sel_carry, init))
    o_sel = acc_s / l_s
    o_win = acc_w / l_w

    gates = gate_ref[...]
    for g in range(NSA_GROUP):
        sl = slice(g * HEAD_DIM, (g + 1) * HEAD_DIM)
        rs = slice(g * tq, (g + 1) * tq)
        o = (gates[:, 3 * g:3 * g + 1] * ocmp_ref[:, sl] + gates[:, 3 * g + 1:3 * g + 2] * o_sel[:, rs].T
             + gates[:, 3 * g + 2:3 * g + 3] * o_win[:, rs].T)
        o_ref[:, sl] = o.astype(o_ref.dtype)


def nsa_prompt_attention(z16, z32, bias5, mask, o_cmp, bsz, seq):
    n = bsz * seq
    tq = PAGE
    nq = seq // tq
    grp = NSA_GROUP * HEAD_DIM
    seq_col = lambda col: pl.BlockSpec((seq, HEAD_DIM), lambda b, k, i: (b, col // HEAD_DIM + k))
    return pl.pallas_call(
        _nsa_prompt_kernel,
        grid=(bsz, NSA_KV_HEADS, nq),
        in_specs=[
            pl.BlockSpec((tq, grp), lambda b, k, i: (b * nq + i, COL_Q // grp + k)),
            seq_col(COL_KV + 2 * KV_LANES), seq_col(COL_KV + 3 * KV_LANES),
            seq_col(COL_WIN), seq_col(COL_WIN + KV_LANES),
            pl.BlockSpec((NSA_GROUP, bias5.shape[1], tq), lambda b, k, i: (k, 0, 0)),
            pl.BlockSpec((None, None, seq, tq), lambda b, k, i: (b, k, 0, i)),
            pl.BlockSpec((tq, grp), lambda b, k, i: (b * nq + i, k)),
            pl.BlockSpec((tq, LANE), lambda b, k, i: (b * nq + i, COL_GATE // LANE + k)),
        ],
        out_specs=pl.BlockSpec((tq, grp), lambda b, k, i: (b * nq + i, k)),
        out_shape=jax.ShapeDtypeStruct((n, NSA_HEADS * HEAD_DIM), BF16),
        compiler_params=_cparams("parallel", "parallel", "arbitrary"),
        name="nsa_prompt_attention",
    )(z16, z16, z16, z16, z16, bias5, mask, o_cmp, z32)


ROWS_PER_TOKEN = 4 * NSA_KV_HEADS


def _softmax_attend(q, k, v, bias):
    s = _nt_dot(q, k) + bias
    e = jnp.exp(s - jnp.max(s, axis=-1, keepdims=True))
    o = jnp.dot(e.astype(BF16), v, preferred_element_type=F32)
    return o / jnp.sum(e, axis=-1, keepdims=True), e


PAGE_BUFFERS = 16


def _low_half_mask():
    return lax.broadcasted_iota(jnp.int32, (SUBLANE, HEAD_DIM), 0) < NSA_KV_HEADS


def _split_token_pairs(toks):
    lo = _low_half_mask()
    a = [jnp.where(lo, toks[i], pltpu.roll(toks[i + 1], NSA_KV_HEADS, 0)) for i in (0, 2)]
    b = [jnp.where(lo, pltpu.roll(toks[i], NSA_KV_HEADS, 0), toks[i + 1]) for i in (0, 2)]
    return jnp.concatenate(a, axis=0).astype(BF16), jnp.concatenate(b, axis=0).astype(BF16)


def _nsa_sample_kernel(n_pages, n_sel, pt_ref, q_ref, pages_hbm, knew_ref, wst_ref, wnew_ref, bcmp_ref, bsel_ref,
                       bwin_ref, kpos_ref, wck_ref, wcv_ref, posk_ref, posv_ref, w2k_ref, w2v_ref, kn_ref, ovl_ref,
                       exp_ref, gate_ref, o_ref, wout_ref, xs_ref, ks_ref, vs_ref, kw_ref, vw_ref, pbuf_ref, sem_ref):
    b = pl.program_id(0)
    n_total = pl.num_programs(0) * n_pages
    n_past = n_pages * PAGE
    t = q_ref.shape[0]
    rpt = ROWS_PER_TOKEN
    kvh_n = NSA_KV_HEADS

    def page_copy(g):
        slot = g % PAGE_BUFFERS
        return pltpu.make_async_copy(pages_hbm.at[pt_ref[g // n_pages, g % n_pages]], pbuf_ref.at[slot],
                                     sem_ref.at[slot])

    @pl.when(b == 0)
    def _():
        for g in range(PAGE_BUFFERS):
            page_copy(g).start()

    def page_body(pg, carry):
        g = b * n_pages + pg
        page_copy(g).wait()
        page_ref = pbuf_ref.at[g % PAGE_BUFFERS]
        row0 = pl.multiple_of(pg * (CMP_PER_PAGE * kvh_n), CMP_PER_PAGE * kvh_n)
        for r in range(CMP_STRIDE):
            tl = [page_ref[pl.ds((CMP_STRIDE * hb + r) * rpt, SUBLANE), :] for hb in range(CMP_PER_PAGE)]
            for w in range(CMP_PER_PAGE // 4):
                xk, xv = _split_token_pairs(tl[4 * w:4 * w + 4])
                xs_ref[0, pl.ds(row0 + 16 * w, 16), r * HEAD_DIM:(r + 1) * HEAD_DIM] = xk
                xs_ref[1, pl.ds(row0 + 16 * w, 16), r * HEAD_DIM:(r + 1) * HEAD_DIM] = xv
        k0 = pl.multiple_of(pg * (PAGE * kvh_n), PAGE * kvh_n)
        for w in range(PAGE // 4):
            tl = [page_ref[pl.ds((4 * w + i) * rpt + SUBLANE, SUBLANE), :] for i in range(4)]
            kk, vv = _split_token_pairs(tl)
            ks_ref[pl.ds(k0 + 16 * w, 16), :] = kk
            vs_ref[pl.ds(k0 + 16 * w, 16), :] = vv

        @pl.when(g + PAGE_BUFFERS < n_total)
        def _():
            page_copy(g + PAGE_BUFFERS).start()

        return carry

    lax.fori_loop(0, n_pages, page_body, 0)
    _nsa_sample_finish(n_past, n_sel, t, q_ref, knew_ref, wst_ref, wnew_ref, bcmp_ref, bsel_ref, bwin_ref, kpos_ref,
                       wck_ref, wcv_ref, posk_ref, posv_ref, w2k_ref, w2v_ref, kn_ref, ovl_ref, exp_ref, gate_ref,
                       o_ref, xs_ref, ks_ref, vs_ref, kw_ref, vw_ref)

    rows_tok = 2 * kvh_n
    keep_rows = wst_ref.shape[0] - t * rows_tok
    wout_ref[0:keep_rows, :] = wst_ref[t * rows_tok:, :]
    wnew = wnew_ref[...]
    for j in range(rows_tok):
        wout_ref[pl.ds(keep_rows + j, t, stride=rows_tok), :] = wnew[:, j * HEAD_DIM:(j + 1) * HEAD_DIM]


def _nsa_sample_finish(n_past, n_sel, t, q_ref, knew_ref, wst_ref, wnew_ref, bcmp_ref, bsel_ref, bwin_ref, kpos_ref,
                       wck_ref, wcv_ref, posk_ref, posv_ref, w2k_ref, w2v_ref, kn_ref, ovl_ref, exp_ref, gate_ref,
                       o_ref, xs_ref, ks_ref, vs_ref, kw_ref, vw_ref):
    kvh_n = NSA_KV_HEADS

    def new_rows(x):
        rows = jnp.concatenate([x[:, k * HEAD_DIM:(k + 1) * HEAD_DIM] for k in range(kvh_n)], axis=0)
        return jnp.concatenate([rows, jnp.zeros((PAGE * kvh_n - rows.shape[0], HEAD_DIM), F32)], axis=0).astype(BF16)

    knew = knew_ref[...]
    ks_ref[kvh_n * n_past:kvh_n * (n_past + PAGE), :] = new_rows(knew[:, :KV_LANES])
    vs_ref[kvh_n * n_past:kvh_n * (n_past + PAGE), :] = new_rows(knew[:, KV_LANES:])

    q = q_ref[...]
    qall = jnp.concatenate([q[:, h * HEAD_DIM:(h + 1) * HEAD_DIM] for h in range(NSA_HEADS)], axis=0).astype(BF16)

    kc = _compress_finish(xs_ref[0], wck_ref[...], posk_ref[...], w2k_ref[...], kn_ref[...], kvh_minor=True)
    vc = _compress_finish(xs_ref[1], wcv_ref[...], posv_ref[...], w2v_ref[...], None, kvh_minor=True)
    o_cmp, e_c = _softmax_attend(qall, kc.astype(BF16), vc.astype(BF16), bcmp_ref[...])
    p_c = e_c / jnp.sum(e_c, axis=-1, keepdims=True)
    imp_h = _dot3_lhs(p_c, ovl_ref[...])
    imp = jnp.concatenate(
        [sum(imp_h[(k * NSA_GROUP + g) * t:(k * NSA_GROUP + g + 1) * t] for g in range(NSA_GROUP))
         for k in range(kvh_n)], axis=0)
    qpos = n_past + lax.broadcasted_iota(jnp.int32, (kvh_n * t, 1), 0) % t
    mb = _selected_key_mask(imp, exp_ref[...], qpos, kpos_ref[...], n_sel)
    mb = jnp.concatenate([mb[k * t:(k + 1) * t] for k in range(kvh_n) for _ in range(NSA_GROUP)], axis=0)

    o_sel, _ = _softmax_attend(qall, ks_ref[...], vs_ref[...], bsel_ref[...] + mb)

    n_st = wst_ref.shape[0] // (2 * kvh_n)

    def win_body(w, carry):
        base = pl.multiple_of(w * 32, 32)
        tl = [wst_ref[pl.ds(base + SUBLANE * i, SUBLANE), :] for i in range(4)]
        kk, vv = _split_token_pairs(tl)
        dst = pl.multiple_of(w * 16, 16)
        kw_ref[pl.ds(dst, 16), :] = kk
        vw_ref[pl.ds(dst, 16), :] = vv
        return carry

    lax.fori_loop(0, n_st // 4, win_body, 0, unroll=4)
    wnew = wnew_ref[...]
    kw_ref[kvh_n * n_st:kvh_n * (n_st + PAGE), :] = new_rows(wnew[:, :KV_LANES])
    vw_ref[kvh_n * n_st:kvh_n * (n_st + PAGE), :] = new_rows(wnew[:, KV_LANES:])
    o_win, _ = _softmax_attend(qall, kw_ref[...], vw_ref[...], bwin_ref[...])

    gates = gate_ref[...]
    for h in range(NSA_HEADS):
        kvh, g = divmod(h, NSA_GROUP)
        rs = slice(h * t, (h + 1) * t)
        gc = kvh * LANE + 3 * g
        o_ref[:, h * HEAD_DIM:(h + 1) * HEAD_DIM] = (
            gates[:, gc:gc + 1] * o_cmp[rs] + gates[:, gc + 1:gc + 2] * o_sel[rs] + gates[:, gc + 2:gc + 3] * o_win[rs])


def _sample_column_tables(bias_cmp, bias_sel, bias_win, seq, n_past, n_win):
    kvh_n = NSA_KV_HEADS
    row_kvh = np.arange(NSA_HEADS * seq) // (NSA_GROUP * seq)

    def columns(n_old):
        c = np.arange((n_old + PAGE) * kvh_n)
        old = c < n_old * kvh_n
        cn = c - n_old * kvh_n
        valid = old | (cn < kvh_n * seq)
        pos = np.where(old, c // kvh_n, n_old + cn % seq)
        kvh = np.where(old, c % kvh_n, cn // seq)
        return np.where(valid, pos, 0), kvh, valid

    def widen(bias, pos, kvh, valid):
        ok = valid[None, :] & (kvh[None, :] == row_kvh[:, None])
        return jnp.where(jnp.asarray(ok), jnp.take(bias, jnp.asarray(pos), axis=1), NEG_INF)

    cc = np.arange(N_CMP_PAD * kvh_n)
    b_cmp = widen(bias_cmp, cc // kvh_n, cc % kvh_n, np.ones_like(cc, bool))
    pos_s, kvh_s, valid_s = columns(n_past)
    b_sel = widen(bias_sel, pos_s, kvh_s, valid_s)
    pos_w, kvh_w, valid_w = columns(n_win)
    b_win = widen(bias_win, pos_w, kvh_w, valid_w)
    kpos = jnp.asarray(np.where(valid_s, pos_s, np.iinfo(np.int32).max)[None, :].astype(np.int32))
    ci = np.arange(N_CMP_PAD)[:, None] * CMP_STRIDE
    sj = np.arange(LANE)[None, :] * SEL_BLOCK
    overlap = np.repeat(((ci < sj + SEL_BLOCK) & (ci + CMP_BLOCK > sj)).astype(np.float32), kvh_n, axis=0)
    expand = ((pos_s[None, :] // SEL_BLOCK == np.arange(LANE)[:, None]) & valid_s[None, :]).astype(np.float32)
    return b_cmp, b_sel, b_win, kpos, jnp.asarray(overlap, BF16), jnp.asarray(expand, BF16)


def nsa_sample_attention(z32, pages, page_table, win_state, bias_cmp, bias_sel, bias_win, cmp_w, seq):
    bsz, n_pages = page_table.shape
    n = bsz * seq
    width = NSA_HEADS * HEAD_DIM
    kvh_n = NSA_KV_HEADS
    n_past = n_pages * PAGE
    n_win = win_state.shape[1] // (2 * kvh_n)
    n_sel = -(-(n_past + seq) // SEL_BLOCK)
    assert bsz * n_pages >= PAGE_BUFFERS and bias_sel.shape[1] == n_past + PAGE
    b_cmp, b_sel, b_win, kpos, overlap, expand = _sample_column_tables(bias_cmp, bias_sel, bias_win, seq, n_past, n_win)
    cst = lambda a: pl.BlockSpec(a.shape, lambda b, pt: (0,) * a.ndim, pipeline_mode=pl.Buffered(1))
    per_seq = lambda w, col: pl.BlockSpec((seq, w), lambda b, pt: (b, col // w))
    consts = [b_cmp, b_sel, b_win, kpos] + list(cmp_w) + [overlap, expand]
    return pl.pallas_call(
        functools.partial(_nsa_sample_kernel, n_pages, n_sel),
        grid_spec=pltpu.PrefetchScalarGridSpec(
            num_scalar_prefetch=1,
            grid=(bsz,),
            in_specs=[per_seq(width, COL_Q),
                      pl.BlockSpec(memory_space=pl.ANY),
                      per_seq(2 * KV_LANES, COL_KV + 2 * KV_LANES),
                      pl.BlockSpec((None, win_state.shape[1], HEAD_DIM), lambda b, pt: (b, 0, 0)),
                      per_seq(2 * KV_LANES, COL_WIN)]
                     + [cst(a) for a in consts] + [per_seq(kvh_n * LANE, COL_GATE)],
            out_specs=[per_seq(width, 0),
                       pl.BlockSpec((None, win_state.shape[1], HEAD_DIM), lambda b, pt: (b, 0, 0))],
            scratch_shapes=[pltpu.VMEM((2, kvh_n * N_CMP_PAD, CMP_STRIDE * HEAD_DIM), BF16),
                            pltpu.VMEM((kvh_n * (n_past + PAGE), HEAD_DIM), BF16),
                            pltpu.VMEM((kvh_n * (n_past + PAGE), HEAD_DIM), BF16),
                            pltpu.VMEM((kvh_n * (n_win + PAGE), HEAD_DIM), BF16),
                            pltpu.VMEM((kvh_n * (n_win + PAGE), HEAD_DIM), BF16),
                            pltpu.VMEM((PAGE_BUFFERS, PAGE * ROWS_PER_TOKEN, HEAD_DIM), F32),
                            pltpu.SemaphoreType.DMA((PAGE_BUFFERS,))],
        ),
        out_shape=[jax.ShapeDtypeStruct((n, width), F32), jax.ShapeDtypeStruct(win_state.shape, win_state.dtype)],
        compiler_params=_cparams("arbitrary"),
        name="nsa_sample_attention",
    )(page_table, z32, pages, z32, win_state, z32, *consts, z32)


SCALE = HEAD_DIM ** -0.5


def _layer_weights(i, ffn1_w_gate, ffn1_w_up, ffn1_w_down, w_in, w_out, nsa_q_norm, nsa_k_norm,
                   ffn2_w_gate, ffn2_w_up, ffn2_w_down, ple_w_gate, ple_w_proj):
    wi = w_in[i]
    wgate = wi[:, COL_GATE:].reshape(-1, NSA_KV_HEADS, 3 * NSA_GROUP)
    wgate = jnp.pad(wgate, ((0, 0), (0, 0), (0, LANE - 3 * NSA_GROUP))).reshape(-1, NSA_KV_HEADS * LANE)
    w_mix = wi[:, :COL_GATE].astype(BF16)
    t_q, t_ks, t_kw = COL_Q // PROJ_TILE, (COL_KV + 2 * KV_LANES) // PROJ_TILE, COL_WIN // PROJ_TILE
    n_q = NSA_WIDTH // PROJ_TILE
    gains = jnp.ones((Z_COLS // PROJ_TILE, 1, HEAD_DIM), F32)
    gains = gains.at[t_q:t_q + n_q].set(nsa_q_norm[i].astype(F32) * SCALE)
    gains = gains.at[t_ks].set(nsa_k_norm[i, 1].astype(F32)).at[t_kw].set(nsa_k_norm[i, 2].astype(F32))
    return dict(
        ffn1=(ffn1_w_gate[i].astype(BF16), ffn1_w_up[i].astype(BF16), ffn1_w_down[i].astype(BF16)),
        ffn2=(ffn2_w_gate[i].astype(BF16), ffn2_w_up[i].astype(BF16), ffn2_w_down[i].astype(BF16)),
        w_mix=w_mix, w_mix_gate=wgate.astype(BF16), mix_gains=gains,
        norm_tiles=tuple(range(t_q, t_q + n_q)) + (t_ks, t_kw),
        gate_tile=COL_GATE // PROJ_TILE,
        wo_h=w_out[i, :HGRN_WIDTH].astype(BF16), wo_n=w_out[i, HGRN_WIDTH:].astype(BF16),
        ple_gate=ple_w_gate[i].astype(BF16), ple_proj=ple_w_proj[i].astype(BF16),
    )


def _run_layer(i, w, x, pemb, bsz, seq, s0, nsa_fn, norms, hgrn_lb):
    ffn1_norm, mix_norm, hgrn_out_norm, ffn2_norm, ple_norm, ple_post_norm = norms
    x1 = ffn_residual(x, ffn1_norm[i], *w["ffn1"])
    z32, z16, kv_rows = mixer_project(x1, mix_norm[i], w["w_mix"], w["w_mix_gate"], w["mix_gains"], w["norm_tiles"],
                                      w["gate_tile"], COL_KV // PROJ_TILE, (COL_WIN - COL_KV) // PROJ_TILE)
    o_h, s_fin = hgrn2_mix(z32, hgrn_lb, hgrn_out_norm[i], i, bsz, seq, HGRN_HEADS, s0=s0,
                           heads_per_step=HGRN_HEADS if seq % 16 else 4)
    o_n, win_state_new = nsa_fn(z32, z16)
    x2 = out_project_residual(x1, o_h, o_n, w["wo_h"], w["wo_n"])
    x3 = ffn_residual(x2, ffn2_norm[i], *w["ffn2"])
    y = ple_residual(x3, ple_norm[i], w["ple_gate"], pemb, w["ple_proj"], ple_post_norm[i])
    win_rows = z32[:, COL_WIN:COL_GATE] if win_state_new is None else win_state_new
    return y, kv_rows, win_rows, s_fin


def kernel(x_prompt, x_sample, cache_kv, state_win_kv, state_hgrn, page_table, p_prompt, p_sample, ffn1_norm, ffn1_w_gate, ffn1_w_up, ffn1_w_down, mix_norm, w_in, w_out, hgrn_lb, hgrn_out_norm, nsa_q_norm, nsa_k_norm, cmp_pos, cmp_w1, cmp_w2, rel_bias_table, ffn2_norm, ffn2_w_gate, ffn2_w_up, ffn2_w_down, ple_norm, ple_w_gate, ple_w_proj, ple_post_norm):
    depth = cache_kv.shape[0]
    bp, tp, d = x_prompt.shape
    bs, ts, _ = x_sample.shape
    n_pool = cache_kv.shape[1]
    n_pages = page_table.shape[1]
    past = n_pages * PAGE
    win_keep = state_win_kv.shape[2]
    assert tp % PAGE == 0 and tp >= WINDOW and win_keep == WINDOW
    norms = (ffn1_norm, mix_norm, hgrn_out_norm, ffn2_norm, ple_norm, ple_post_norm)
    table = rel_bias_table.astype(F32)

    n_cmp = (tp - CMP_BLOCK) // CMP_STRIDE + 1
    bias_pc = rel_bias(table, tp, N_CMP_PAD, 0, CMP_BLOCK - 1, CMP_STRIDE, n_cmp)
    bias_p5 = rel_bias(table, WINDOW + 3 * PAGE, PAGE, -PAGE, 0, 1, PAGE, window=WINDOW,
                       key_major_tiles=True)
    n_cmp_s = (past + ts - CMP_BLOCK) // CMP_STRIDE + 1
    assert n_cmp_s <= N_CMP_PAD - 1 and (n_cmp_s - 1) * CMP_STRIDE + CMP_BLOCK <= past
    sel_cols = past + PAGE
    bias_sc = rel_bias(table, ts, N_CMP_PAD, past, CMP_BLOCK - 1, CMP_STRIDE, n_cmp_s).reshape(NSA_HEADS * ts, N_CMP_PAD)
    bias_ss = rel_bias(table, ts, sel_cols, past, 0, 1, past + ts).reshape(NSA_HEADS * ts, sel_cols)
    bias_sw = rel_bias(table, ts, win_keep + PAGE, win_keep, 0, 1, win_keep + ts, window=WINDOW)
    bias_sw = bias_sw.reshape(NSA_HEADS * ts, win_keep + PAGE)

    xp = x_prompt.reshape(bp * tp, d)
    xs = x_sample.reshape(bs * ts, d)
    outs = [[] for _ in range(6)]
    for i in range(depth):
        w = _layer_weights(i, ffn1_w_gate, ffn1_w_up, ffn1_w_down, w_in, w_out, nsa_q_norm, nsa_k_norm,
                           ffn2_w_gate, ffn2_w_up, ffn2_w_down, ple_w_gate, ple_w_proj)
        cmp_w = compress_weights(cmp_w1[i], cmp_w2[i], cmp_pos[i], nsa_k_norm[i, 0])

        def nsa_prompt(z32, z16):
            pages = z32.reshape(bp * tp // PAGE, PAGE, Z_COLS)
            pt = jnp.arange(bp * tp // PAGE, dtype=jnp.int32).reshape(bp, tp // PAGE)
            kc, vc = compress_cache(pages, pt, COL_KV // HEAD_DIM, cmp_w)
            o_cmp, mask = cmp_attention_topk(z16, kc, vc, bias_pc, bp, tp, 0, tp)
            return nsa_prompt_attention(z16, z32, bias_p5, mask, o_cmp, bp, tp), None

        def nsa_sample(z32, z16):
            pages = cache_kv[i].reshape(n_pool, PAGE * ROWS_PER_TOKEN, HEAD_DIM)
            wst = state_win_kv[i].reshape(bs, win_keep * 2 * NSA_KV_HEADS, HEAD_DIM)
            return nsa_sample_attention(z32, pages, page_table, wst, bias_sc, bias_ss, bias_sw, cmp_w, ts)

        xp, kv_p, win_p, h_p = _run_layer(i, w, xp, p_prompt[i].reshape(bp * tp, -1), bp, tp, None, nsa_prompt, norms, hgrn_lb)
        xs, kv_s, win_s, h_s = _run_layer(i, w, xs, p_sample[i].reshape(bs * ts, -1), bs, ts, state_hgrn[i], nsa_sample, norms, hgrn_lb)
        outs[0].append(kv_p.reshape(bp, tp, 4, NSA_KV_HEADS, HEAD_DIM))
        outs[1].append(win_p.reshape(bp, tp, 2, NSA_KV_HEADS, HEAD_DIM)[:, -WINDOW:])
        outs[2].append(h_p)
        outs[3].append(kv_s.reshape(bs, ts, 4, NSA_KV_HEADS, HEAD_DIM))
        outs[4].append(win_s.reshape(bs, win_keep, 2, NSA_KV_HEADS, HEAD_DIM))
        outs[5].append(h_s.astype(state_hgrn.dtype))
    return (xp.reshape(bp, tp, d), xs.reshape(bs, ts, d)) + tuple(jnp.stack(o) for o in outs)
```

```python
import functools
import math

import numpy as np
import jax
import jax.numpy as jnp
from jax import lax
from jax.experimental import pallas as pl
from jax.experimental.pallas import tpu as pltpu

F32 = jnp.float32
BF16 = jnp.bfloat16

LANE = 128
SUBLANE = 8
VMEM_LIMIT_BYTES = 56 * 1024 * 1024

HEAD_DIM = 128
RMS_EPS = 1e-6
NEG_INF = -1e30


def _cparams(*sem):
    return pltpu.CompilerParams(dimension_semantics=sem, vmem_limit_bytes=VMEM_LIMIT_BYTES)


def _tile(n, pref):
    if n <= pref:
        return n
    t = pref
    while t >= SUBLANE:
        if n % t == 0:
            return t
        t -= SUBLANE
    return n


def _rms(x, gain):
    ms = jnp.mean(x * x, axis=-1, keepdims=True)
    return x * lax.rsqrt(ms + RMS_EPS) * gain


def _norm_rows_to(x_ref, gain_ref, dst_ref):
    rows = x_ref.shape[0]
    rc = 32 if rows % 32 == 0 else rows

    def body(i, carry):
        r = pl.multiple_of(i * rc, rc)
        dst_ref[pl.ds(r, rc), :] = _rms(x_ref[pl.ds(r, rc), :], gain_ref[...]).astype(dst_ref.dtype)
        return carry

    lax.fori_loop(0, rows // rc, body, 0)


def _ffn_kernel(x_ref, gain_ref, wg_ref, wu_ref, wd_ref, o_ref, hn_ref):
    @pl.when(pl.program_id(1) == 0)
    def _():
        _norm_rows_to(x_ref, gain_ref, hn_ref)
        o_ref[...] = x_ref[...]

    h = hn_ref[...]
    g = jnp.dot(h, wg_ref[...], preferred_element_type=F32)
    u = jnp.dot(h, wu_ref[...], preferred_element_type=F32)
    a = (g * jax.nn.sigmoid(g) * (0.5 * u)).astype(BF16)
    o_ref[...] += jnp.dot(a, wd_ref[...], preferred_element_type=F32)


def ffn_residual(x, gain, wg, wu, wd, tm_pref=512, tf_pref=256):
    n, d = x.shape
    f = wg.shape[1]
    tm, tf = _tile(n, tm_pref), _tile(f, tf_pref)
    return pl.pallas_call(
        _ffn_kernel,
        grid=(n // tm, f // tf),
        in_specs=[
            pl.BlockSpec((tm, d), lambda i, j: (i, 0)),
            pl.BlockSpec((1, d), lambda i, j: (0, 0)),
            pl.BlockSpec((d, tf), lambda i, j: (0, j)),
            pl.BlockSpec((d, tf), lambda i, j: (0, j)),
            pl.BlockSpec((tf, d), lambda i, j: (j, 0)),
        ],
        out_specs=pl.BlockSpec((tm, d), lambda i, j: (i, 0)),
        out_shape=jax.ShapeDtypeStruct((n, d), F32),
        scratch_shapes=[pltpu.VMEM((tm, d), BF16)],
        compiler_params=_cparams("parallel", "arbitrary"),
        name="ffn_residual",
    )(x, gain.reshape(1, d), wg, wu, wd)


PROJ_TILE = 512


def _mixproj_kernel(norm_tiles, gate_tile, kv_tile0, n_slots, x_ref, gain_ref, w_ref, wg_ref, hg_ref,
                    z32_ref, z16_ref, kvr_ref, hn_ref):
    j = pl.program_id(1)
    tm = x_ref.shape[0]

    @pl.when(j == 0)
    def _():
        _norm_rows_to(x_ref, gain_ref, hn_ref)

    def put(val):
        z32_ref[...] = val
        z16_ref[...] = val.astype(BF16)

    @pl.when(j != gate_tile)
    def _():
        acc = jnp.dot(hn_ref[...], w_ref[...], preferred_element_type=F32)
        is_norm = functools.reduce(jnp.logical_or, [j == t for t in norm_tiles])
        gain = hg_ref[0]
        heads = [acc[:, h * HEAD_DIM:(h + 1) * HEAD_DIM] for h in range(acc.shape[1] // HEAD_DIM)]
        heads = [jnp.where(is_norm, _rms(v, gain), v) for v in heads]
        put(jnp.concatenate(heads, axis=1))

        @pl.when(jnp.logical_and(j >= kv_tile0, j < kv_tile0 + n_slots))
        def _():
            row0 = (j - kv_tile0) * len(heads)
            for kvh, v in enumerate(heads):
                kvr_ref[pl.ds(row0 + kvh, tm, stride=n_slots * len(heads)), :] = v

    @pl.when(j == gate_tile)
    def _():
        put(jax.nn.sigmoid(jnp.dot(hn_ref[...], wg_ref[...], preferred_element_type=F32)))


def mixer_project(x, gain, w, w_gate, head_gain, norm_tiles, gate_tile, kv_tile0, n_slots, tm_pref=512):
    n, d = x.shape
    tm, tn = _tile(n, tm_pref), PROJ_TILE
    assert w.shape[1] >= gate_tile * tn and w_gate.shape[1] == tn
    ncols = (gate_tile + 1) * tn
    rows_tok = n_slots * (tn // HEAD_DIM)
    return pl.pallas_call(
        functools.partial(_mixproj_kernel, norm_tiles, gate_tile, kv_tile0, n_slots),
        grid=(n // tm, ncols // tn),
        in_specs=[
            pl.BlockSpec((tm, d), lambda i, j: (i, 0)),
            pl.BlockSpec((1, d), lambda i, j: (0, 0)),
            pl.BlockSpec((d, tn), lambda i, j: (0, jnp.minimum(j, gate_tile - 1))),
            pl.BlockSpec((d, tn), lambda i, j: (0, 0), pipeline_mode=pl.Buffered(1)),
            pl.BlockSpec((1, 1, HEAD_DIM), lambda i, j: (j, 0, 0)),
        ],
        out_specs=[pl.BlockSpec((tm, tn), lambda i, j: (i, j)), pl.BlockSpec((tm, tn), lambda i, j: (i, j)),
                   pl.BlockSpec((tm * rows_tok, HEAD_DIM), lambda i, j: (i, 0))],
        out_shape=[jax.ShapeDtypeStruct((n, ncols), F32), jax.ShapeDtypeStruct((n, ncols), BF16),
                   jax.ShapeDtypeStruct((n * rows_tok, HEAD_DIM), F32)],
        scratch_shapes=[pltpu.VMEM((tm, d), BF16)],
        compiler_params=_cparams("parallel", "arbitrary"),
        name="mixer_project",
    )(x, gain.reshape(1, d), w, w_gate, head_gain)


def _outproj_kernel(x_ref, a_ref, b_ref, wa_ref, wb_ref, o_ref):
    acc = jnp.dot(a_ref[...].astype(BF16), wa_ref[...], preferred_element_type=F32)
    acc += jnp.dot(b_ref[...].astype(BF16), wb_ref[...], preferred_element_type=F32)
    o_ref[...] = x_ref[...] + acc


def out_project_residual(x, a, b, wa, wb, tm_pref=512, tn_pref=1024):
    n, d = x.shape
    ka, kb = a.shape[1], b.shape[1]
    tm, tn = _tile(n, tm_pref), _tile(d, tn_pref)
    return pl.pallas_call(
        _outproj_kernel,
        grid=(n // tm, d // tn),
        in_specs=[
            pl.BlockSpec((tm, tn), lambda i, j: (i, j)),
            pl.BlockSpec((tm, ka), lambda i, j: (i, 0)),
            pl.BlockSpec((tm, kb), lambda i, j: (i, 0)),
            pl.BlockSpec((ka, tn), lambda i, j: (0, j)),
            pl.BlockSpec((kb, tn), lambda i, j: (0, j)),
        ],
        out_specs=pl.BlockSpec((tm, tn), lambda i, j: (i, j)),
        out_shape=jax.ShapeDtypeStruct((n, d), F32),
        compiler_params=_cparams("parallel", "arbitrary"),
        name="out_project_residual",
    )(x, a, b, wa, wb)


def _ple_kernel(tn, x_ref, gain_ref, wg_ref, p_ref, wp_ref, pg_ref, o_ref, hn_ref, pe_ref):
    j = pl.program_id(1)

    @pl.when(j == 0)
    def _():
        _norm_rows_to(x_ref, gain_ref, hn_ref)
        pe_ref[...] = jnp.dot(p_ref[...].astype(BF16), wp_ref[...], preferred_element_type=F32)
        _norm_rows_to(pe_ref, pg_ref, pe_ref)

    c = pl.multiple_of(j * tn, LANE)
    gate = jax.nn.sigmoid(jnp.dot(hn_ref[...], wg_ref[...], preferred_element_type=F32))
    o_ref[...] = x_ref[:, pl.ds(c, tn)] + gate * pe_ref[:, pl.ds(c, tn)]


def ple_residual(x, gain, wg, p, wp, post_gain, tm_pref=512, tn_pref=512):
    n, d = x.shape
    pd = p.shape[1]
    tm, tn = _tile(n, tm_pref), _tile(d, tn_pref)
    return pl.pallas_call(
        functools.partial(_ple_kernel, tn),
        grid=(n // tm, d // tn),
        in_specs=[
            pl.BlockSpec((tm, d), lambda i, j: (i, 0)),
            pl.BlockSpec((1, d), lambda i, j: (0, 0)),
            pl.BlockSpec((d, tn), lambda i, j: (0, j)),
            pl.BlockSpec((tm, pd), lambda i, j: (i, 0)),
            pl.BlockSpec((pd, d), lambda i, j: (0, 0)),
            pl.BlockSpec((1, d), lambda i, j: (0, 0)),
        ],
        out_specs=pl.BlockSpec((tm, tn), lambda i, j: (i, j)),
        out_shape=jax.ShapeDtypeStruct((n, d), F32),
        scratch_shapes=[pltpu.VMEM((tm, d), BF16), pltpu.VMEM((tm, d), F32)],
        compiler_params=_cparams("parallel", "arbitrary"),
        name="ple_residual",
    )(x, gain.reshape(1, d), wg, p, wp, post_gain.reshape(1, d))


HGRN_DIAG = SUBLANE


def _split3_dot(lhs_bf16, x):
    hi = x.astype(BF16)
    r1 = x - hi.astype(F32)
    mid = r1.astype(BF16)
    lo = (r1 - mid.astype(F32)).astype(BF16)
    acc = jnp.dot(lhs_bf16, hi, preferred_element_type=F32)
    acc += jnp.dot(lhs_bf16, mid, preferred_element_type=F32)
    acc += jnp.dot(lhs_bf16, lo, preferred_element_type=F32)
    return acc


def _bcast_row_in_blocks(a, bs, row):
    c, k = a.shape
    a3 = a.reshape(c // bs, bs, k)
    return jnp.broadcast_to(a3[:, row:row + 1, :], (c // bs, bs, k)).reshape(c, k)


def _nt_dot(a, b):
    return lax.dot_general(a, b, (((1,), (1,)), ((), ())), preferred_element_type=F32)


def _hgrn_kernel(layer, has_s0, hps, q_ref, f_ref, i_ref, g_ref, lb_ref, og_ref, *rest):
    if has_s0:
        s0_ref, o_ref, sfin_ref, st_ref = rest
    else:
        s0_ref = None
        o_ref, sfin_ref, st_ref = rest
    ci = pl.program_id(2)

    @pl.when(ci == 0)
    def _():
        if s0_ref is not None:
            st_ref[...] = jnp.stack([s0_ref[hh].T for hh in range(hps)])
        else:
            st_ref[...] = jnp.zeros(st_ref.shape, F32)

    outs, states = [], []
    for hh in range(hps):
        ls = slice(hh * HEAD_DIM, (hh + 1) * HEAD_DIM)
        o, st_new = _hgrn_head(layer, q_ref[:, ls], f_ref[:, ls], i_ref[:, ls], g_ref[:, ls], lb_ref[:, ls],
                               og_ref[...], st_ref[hh])
        outs.append(o)
        states.append(st_new)
    o_ref[...] = jnp.concatenate(outs, axis=1).astype(o_ref.dtype)
    st_ref[...] = jnp.stack(states)

    @pl.when(ci == pl.num_programs(2) - 1)
    def _():
        sfin_ref[...] = jnp.stack([s.T for s in states])


def _hgrn_head(layer, q, f_raw, v, gr, lbr, out_gain, st):
    c = q.shape[0]
    e = jnp.exp(lbr - jnp.max(lbr, axis=0, keepdims=True))
    lb = jnp.sum(e[:layer + 1], axis=0, keepdims=True) / jnp.sum(e, axis=0, keepdims=True)

    fg = lb + (1.0 - lb) * jax.nn.sigmoid(f_raw)
    logf = jnp.log(fg)
    kk = 1.0 - fg

    ti = lax.broadcasted_iota(jnp.int32, (c, c), 0)
    si = lax.broadcasted_iota(jnp.int32, (c, c), 1)
    tril = (si <= ti)
    a = _split3_dot(tril.astype(BF16), logf)
    a_last = a[c - 1:c, :]

    kk_b = kk.astype(BF16)
    d = HGRN_DIAG
    xs = [(q * jnp.exp(jnp.minimum(a - _bcast_row_in_blocks(a, d, j), 0.0))).astype(BF16) for j in range(d)]
    res = _nt_dot(jnp.concatenate(xs, axis=0), kk_b)
    attn = jnp.zeros((c, c), F32)
    for j in range(d):
        attn += jnp.where((si % d) == j, res[j * c:(j + 1) * c], 0.0)
    attn = jnp.where(((si // d) == (ti // d)) & tril, attn, 0.0)
    bs = 2 * d
    while bs <= c:
        half = bs // 2
        bnd = _bcast_row_in_blocks(a, bs, half - 1)
        qe = (q * jnp.exp(jnp.minimum(a - bnd, 0.0))).astype(BF16)
        ke = (kk * jnp.exp(jnp.minimum(bnd - a, 0.0))).astype(BF16)
        m = ((si // bs) == (ti // bs)) & ((ti % bs) >= half) & ((si % bs) < half)
        attn += jnp.where(m, _nt_dot(qe, ke), 0.0)
        bs *= 2

    v_b = v.astype(BF16)
    o = jnp.dot(attn.astype(BF16), v_b, preferred_element_type=F32)
    o += _nt_dot((q * jnp.exp(a)).astype(BF16), st.astype(BF16))
    kd = (kk * jnp.exp(a_last - a)).astype(BF16)
    st_new = st * jnp.exp(a_last) + jnp.dot(v_b.T, kd, preferred_element_type=F32)
    return _rms(o, out_gain) * (gr * jax.nn.sigmoid(gr)), st_new


def hgrn2_mix(zh, hgrn_lb, out_gain, layer, batch, seq, heads, s0=None, chunk_pref=128, heads_per_step=4):
    n = batch * seq
    dk = HEAD_DIM
    c = _tile(seq, chunk_pref)
    assert c % HGRN_DIAG == 0 and (c // HGRN_DIAG) & (c // HGRN_DIAG - 1) == 0
    nc = seq // c
    nl = hgrn_lb.shape[0]
    hps = heads_per_step
    hg = heads // hps

    def zspec(sec):
        return pl.BlockSpec((c, hps * dk), lambda b, h, ci: (b * nc + ci, sec * hg + h))

    state_spec = pl.BlockSpec((None, hps, dk, dk), lambda b, h, ci: (b, h, 0, 0))
    in_specs = [zspec(0), zspec(1), zspec(2), zspec(3),
                pl.BlockSpec((nl, hps * dk), lambda b, h, ci: (0, h)),
                pl.BlockSpec((1, dk), lambda b, h, ci: (0, 0))]
    args = [zh, zh, zh, zh, hgrn_lb, out_gain.reshape(1, dk)]
    if s0 is not None:
        in_specs.append(state_spec)
        args.append(s0)
    return pl.pallas_call(
        functools.partial(_hgrn_kernel, layer, s0 is not None, hps),
        grid=(batch, hg, nc),
        in_specs=in_specs,
        out_specs=[pl.BlockSpec((c, hps * dk), lambda b, h, ci: (b * nc + ci, h)), state_spec],
        out_shape=[jax.ShapeDtypeStruct((n, heads * dk), BF16 if c % 16 == 0 else F32),
                   jax.ShapeDtypeStruct((batch, heads, dk, dk), F32)],
        scratch_shapes=[pltpu.VMEM((hps, dk, dk), F32)],
        compiler_params=_cparams("parallel", "parallel", "arbitrary"),
        name="hgrn2_mix",
    )(*args)


NSA_KV_HEADS = 4
NSA_GROUP = 4
NSA_HEADS = NSA_KV_HEADS * NSA_GROUP
KV_LANES = NSA_KV_HEADS * HEAD_DIM
PAGE = 128
CMP_BLOCK = 32
CMP_STRIDE = 16
CMP_PER_PAGE = PAGE // CMP_STRIDE
N_CMP_PAD = 128
SEL_BLOCK = 64
SEL_TOPK = 16
SEL_LOCAL = 2
SEL_FORCE = 1e3
SEL_INVALID = -1e9
WINDOW = 512
N_BUCKETS = 32
MAX_DISTANCE = 128
MASKED = -1e29

HGRN_HEADS = 16
HGRN_WIDTH = HGRN_HEADS * HEAD_DIM
NSA_WIDTH = NSA_HEADS * HEAD_DIM
COL_Q = 4 * HGRN_WIDTH
COL_KV = COL_Q + NSA_WIDTH
COL_WIN = COL_KV + 4 * KV_LANES
COL_GATE = COL_WIN + 2 * KV_LANES
Z_COLS = COL_GATE + NSA_KV_HEADS * LANE


def _bias_kernel(q0, k0, kstride, ncols_valid, window, key_major_tiles, table_ref, o_ref):
    h = pl.program_id(0)
    rt, cols = o_ref.shape
    r = lax.broadcasted_iota(jnp.int32, (rt, cols), 0)
    c = lax.broadcasted_iota(jnp.int32, (rt, cols), 1)
    if key_major_tiles:
        dist = (q0 + pl.program_id(1) * rt + c) - (k0 + r * kstride)
    else:
        dist = (q0 + r + pl.program_id(1) * rt) - (k0 + c * kstride)
    dpos = jnp.maximum(dist, 0)
    max_exact = N_BUCKETS // 2
    log_ratio = jnp.log(jnp.maximum(dpos, 1).astype(F32) / max_exact) / math.log(MAX_DISTANCE / max_exact)
    large = jnp.minimum(max_exact + (log_ratio * (N_BUCKETS - max_exact)).astype(jnp.int32), N_BUCKETS - 1)
    bucket = jnp.where(dpos < max_exact, dpos, large)
    acc = jnp.zeros((rt, cols), F32)
    for b in range(N_BUCKETS):
        acc = jnp.where(bucket == b, table_ref[b, h], acc)
    valid = (dist >= 0) & (c < ncols_valid)
    if window is not None:
        valid = valid & (dist < window)
    o_ref[...] = jnp.where(valid, acc, NEG_INF)


def rel_bias(table, rows, cols, q0, k0, kstride, ncols_valid, window=None, key_major_tiles=False):
    rt = cols if key_major_tiles else _tile(rows, 256)
    return pl.pallas_call(
        functools.partial(_bias_kernel, q0, k0, kstride, ncols_valid, window, key_major_tiles),
        grid=(NSA_HEADS, rows // rt),
        in_specs=[pl.BlockSpec(memory_space=pltpu.SMEM)],
        out_specs=pl.BlockSpec((None, rt, cols), lambda h, i: (h, i, 0)),
        out_shape=jax.ShapeDtypeStruct((NSA_HEADS, rows, cols), F32),
        compiler_params=_cparams("parallel", "parallel"),
        name="rel_bias",
    )(table)


def _compress_kernel(pt_ref, *refs):
    n_in = 2 * NSA_KV_HEADS
    page_refs = refs[:n_in]
    wck_ref, wcv_ref, posk_ref, posv_ref, w2k_ref, w2v_ref, kn_ref, kc_ref, vc_ref, xs_ref = refs[n_in:]
    p = pl.program_id(1)
    for slot in range(2):
        for kvh in range(NSA_KV_HEADS):
            page_ref = page_refs[slot * NSA_KV_HEADS + kvh]
            row0 = pl.multiple_of(kvh * N_CMP_PAD + p * CMP_PER_PAGE, CMP_PER_PAGE)
            for r in range(CMP_STRIDE):
                xs_ref[slot, pl.ds(row0, CMP_PER_PAGE), r * HEAD_DIM:(r + 1) * HEAD_DIM] = (
                    page_ref[pl.ds(r, CMP_PER_PAGE, stride=CMP_STRIDE), :])

    @pl.when(p == pl.num_programs(1) - 1)
    def _():
        kc = _compress_finish(xs_ref[0].astype(BF16), wck_ref[...], posk_ref[...], w2k_ref[...], kn_ref[...])
        vc = _compress_finish(xs_ref[1].astype(BF16), wcv_ref[...], posv_ref[...], w2v_ref[...], None)
        kc_ref[...] = kc.reshape(NSA_KV_HEADS, N_CMP_PAD, HEAD_DIM).astype(kc_ref.dtype)
        vc_ref[...] = vc.reshape(NSA_KV_HEADS, N_CMP_PAD, HEAD_DIM).astype(vc_ref.dtype)


def _compress_finish(x, wcat, posb, w2, k_gain, kvh_minor=False):
    m_rows = x.shape[0]
    hid = w2.shape[0]
    step = NSA_KV_HEADS if kvh_minor else 1
    pq = jnp.dot(x, wcat, preferred_element_type=F32)
    nxt = pltpu.roll(pq[:, hid:], m_rows - step, 0)
    hcur = pq[:, :hid] + nxt + posb
    act = (hcur * jax.nn.sigmoid(hcur)).astype(BF16)
    out = jnp.dot(act, w2, preferred_element_type=F32)
    if k_gain is not None:
        out = _rms(out, k_gain)
    rowid = lax.broadcasted_iota(jnp.int32, out.shape, 0)
    blk = rowid // step if kvh_minor else rowid % N_CMP_PAD
    return jnp.where(blk == N_CMP_PAD - 1, 0.0, out)


def _posb_kernel(pos_ref, w1_ref, o_ref):
    o_ref[...] = jnp.dot(pos_ref[...].astype(BF16), w1_ref[...], preferred_element_type=F32)


def compress_weights(w1, w2, pos, k_norm_cmp):
    hid = w2.shape[1]
    out = []
    for s in range(2):
        wcat = w1[s].reshape(2, CMP_STRIDE, HEAD_DIM, hid).transpose(1, 2, 0, 3).reshape(CMP_STRIDE * HEAD_DIM, 2 * hid)
        posf = jnp.broadcast_to(pos[s].reshape(1, CMP_BLOCK * HEAD_DIM), (SUBLANE, CMP_BLOCK * HEAD_DIM))
        posb = pl.pallas_call(_posb_kernel, out_shape=jax.ShapeDtypeStruct((SUBLANE, hid), F32),
                              name="cmp_pos_bias")(posf, w1[s].astype(BF16))[0:1]
        out.append((wcat.astype(BF16), posb, w2[s].astype(BF16)))
    (wck, posk, w2k), (wcv, posv, w2v) = out
    return [wck, wcv, posk, posv, w2k, w2v, k_norm_cmp.reshape(1, HEAD_DIM)]


def compress_cache(pages, page_table, lane_block, cmp_w):
    bsz, n_pages = page_table.shape
    assert n_pages * CMP_PER_PAGE == N_CMP_PAD
    full = lambda a: pl.BlockSpec(a.shape, lambda b, p, pt: (0,) * a.ndim)
    n_in = 2 * NSA_KV_HEADS

    def page_map(lane_blk, b, p, pt):
        return (pt[b, p], 0, lane_blk)

    ins = cmp_w
    out_spec = pl.BlockSpec((None, NSA_KV_HEADS, N_CMP_PAD, HEAD_DIM), lambda b, p, pt: (b, 0, 0, 0))
    out_sds = jax.ShapeDtypeStruct((bsz, NSA_KV_HEADS, N_CMP_PAD, HEAD_DIM), BF16)
    return pl.pallas_call(
        _compress_kernel,
        grid_spec=pltpu.PrefetchScalarGridSpec(
            num_scalar_prefetch=1,
            grid=(bsz, n_pages),
            in_specs=[pl.BlockSpec((None, PAGE, HEAD_DIM), functools.partial(page_map, lane_block + j))
                      for j in range(n_in)] + [full(a) for a in ins],
            out_specs=[out_spec, out_spec],
            scratch_shapes=[pltpu.VMEM((2, NSA_KV_HEADS * N_CMP_PAD, CMP_STRIDE * HEAD_DIM), F32)],
        ),
        out_shape=[out_sds, out_sds],
        compiler_params=_cparams("parallel", "arbitrary"),
        name="compress_cache",
    )(page_table, *([pages] * n_in), *ins)


def _heads_to_rows(q_ref):
    return jnp.concatenate([q_ref[:, g * HEAD_DIM:(g + 1) * HEAD_DIM] for g in range(NSA_GROUP)], axis=0).astype(BF16)


def _dot3_lhs(x, rhs_bf16):
    hi = x.astype(BF16)
    r1 = x - hi.astype(F32)
    mid = r1.astype(BF16)
    lo = (r1 - mid.astype(F32)).astype(BF16)
    acc = jnp.dot(hi, rhs_bf16, preferred_element_type=F32)
    acc += jnp.dot(mid, rhs_bf16, preferred_element_type=F32)
    acc += jnp.dot(lo, rhs_bf16, preferred_element_type=F32)
    return acc


def _selected_key_mask(imp, expand, qpos, kpos, n_sel):
    r = imp.shape[0]
    lane = lax.broadcasted_iota(jnp.int32, (r, LANE), 1)
    lag = qpos // SEL_BLOCK - lane
    forced = (lane == 0) | ((lag >= 0) & (lag < SEL_LOCAL))
    score = jnp.where(lag >= 0, imp + jnp.where(forced, SEL_FORCE, 0.0), SEL_INVALID)
    cnt = jnp.zeros((r, LANE), jnp.int32)
    for j in range(n_sel):
        col = score[:, j:j + 1]
        cnt += jnp.where(col > score, 1, jnp.where(col == score, jnp.where(lane > j, 1, 0), 0))
    sel = jnp.where(cnt < min(SEL_TOPK, n_sel), jnp.where(lane < n_sel, 1.0, 0.0), 0.0)
    ex = jnp.dot(sel.astype(BF16), expand, preferred_element_type=F32)
    keep = jnp.where(kpos <= qpos, ex, 0.0) > 0.5
    return jnp.where(keep, 0.0, NEG_INF)


def _selected_key_mask_km(imp, expand, qpos_row, kpos, n_sel):
    r = imp.shape[0]
    nb = -(-n_sel // SUBLANE) * SUBLANE
    blk = lax.broadcasted_iota(jnp.int32, (nb, r), 0)
    lag = qpos_row // SEL_BLOCK - blk
    forced = (blk == 0) | ((lag >= 0) & (lag < SEL_LOCAL))
    score = jnp.where(lag >= 0, imp.T[:nb] + jnp.where(forced, SEL_FORCE, 0.0), SEL_INVALID)
    cnt = jnp.zeros((nb, r), jnp.int32)
    for j in range(n_sel):
        row = score[j:j + 1, :]
        cnt += jnp.where(row > score, 1, jnp.where(row == score, jnp.where(blk > j, 1, 0), 0))
    sel = jnp.where(cnt < min(SEL_TOPK, n_sel), jnp.where(blk < n_sel, 1.0, 0.0), 0.0)
    sel = jnp.concatenate([sel, jnp.zeros((LANE - nb, r), F32)], axis=0).astype(BF16)
    ex = jnp.dot(expand, sel, preferred_element_type=F32)
    keep = jnp.where(kpos <= qpos_row, ex, 0.0) > 0.5
    return jnp.where(keep, 0.0, NEG_INF)


def _cmp_topk_kernel(q0, n_sel, q_ref, kc_ref, vc_ref, bias_ref, ovl_ref, exp_ref, ocmp_ref, mask_ref):
    qi = pl.program_id(2)
    tq = q_ref.shape[0]
    q4 = _heads_to_rows(q_ref)
    bias = bias_ref[...].reshape(NSA_GROUP * tq, N_CMP_PAD)
    s = _nt_dot(q4, kc_ref[...]) + bias
    m = jnp.max(s, axis=-1, keepdims=True)
    e = jnp.where(bias > MASKED, jnp.exp(s - m), 0.0)
    p = e / jnp.maximum(jnp.sum(e, axis=-1, keepdims=True), 1e-30)
    o = jnp.dot(p.astype(BF16), vc_ref[...], preferred_element_type=F32)
    psum = jnp.zeros((tq, N_CMP_PAD), F32)
    for g in range(NSA_GROUP):
        ocmp_ref[:, g * HEAD_DIM:(g + 1) * HEAD_DIM] = o[g * tq:(g + 1) * tq]
        psum += p[g * tq:(g + 1) * tq]
    qpos_row = q0 + qi * tq + lax.broadcasted_iota(jnp.int32, (1, tq), 1)
    kpos = lax.broadcasted_iota(jnp.int32, (mask_ref.shape[0], 1), 0)
    imp = _dot3_lhs(psum, ovl_ref[...])
    mask_ref[...] = _selected_key_mask_km(imp, exp_ref[...], qpos_row, kpos, n_sel).astype(mask_ref.dtype)


def cmp_attention_topk(z16, kc, vc, bias_c, bsz, seq, q0, key_len, tq_pref=256):
    n = bsz * seq
    tq = _tile(seq, tq_pref)
    nq = seq // tq
    q_blk = COL_Q // (NSA_GROUP * HEAD_DIM)
    n_sel = -(-key_len // SEL_BLOCK)
    lp = -(-(n_sel * SEL_BLOCK) // LANE) * LANE
    ci = np.arange(N_CMP_PAD)[:, None] * CMP_STRIDE
    sj = np.arange(LANE)[None, :] * SEL_BLOCK
    overlap = jnp.asarray(((ci < sj + SEL_BLOCK) & (ci + CMP_BLOCK > sj)).astype(np.float32), BF16)
    expand = jnp.asarray((np.arange(lp)[:, None] // SEL_BLOCK == np.arange(LANE)[None, :]).astype(np.float32), BF16)
    return pl.pallas_call(
        functools.partial(_cmp_topk_kernel, q0, n_sel),
        grid=(bsz, NSA_KV_HEADS, nq),
        in_specs=[
            pl.BlockSpec((tq, NSA_GROUP * HEAD_DIM), lambda b, k, i: (b * nq + i, q_blk + k)),
            pl.BlockSpec((None, None, N_CMP_PAD, HEAD_DIM), lambda b, k, i: (b, k, 0, 0)),
            pl.BlockSpec((None, None, N_CMP_PAD, HEAD_DIM), lambda b, k, i: (b, k, 0, 0)),
            pl.BlockSpec((NSA_GROUP, tq, N_CMP_PAD), lambda b, k, i: (k, i, 0)),
            pl.BlockSpec((N_CMP_PAD, LANE), lambda b, k, i: (0, 0)),
            pl.BlockSpec((lp, LANE), lambda b, k, i: (0, 0)),
        ],
        out_specs=[
            pl.BlockSpec((tq, NSA_GROUP * HEAD_DIM), lambda b, k, i: (b * nq + i, k)),
            pl.BlockSpec((None, None, lp, tq), lambda b, k, i: (b, k, 0, i)),
        ],
        out_shape=[jax.ShapeDtypeStruct((n, NSA_HEADS * HEAD_DIM), F32),
                   jax.ShapeDtypeStruct((bsz, NSA_KV_HEADS, lp, seq), BF16)],
        compiler_params=_cparams("parallel", "parallel", "parallel"),
        name="cmp_attention_topk",
    )(z16, kc, vc, bias_c, overlap, expand)


def _softmax_step_km(q4, k, v, bias, carry):
    m, l, acc = carry
    s = _nt_dot(k, q4) + bias
    m_new = jnp.maximum(m, jnp.max(s, axis=0, keepdims=True))
    alpha = jnp.exp(m - m_new)
    pr = jnp.exp(s - m_new)
    l = alpha * l + jnp.sum(pr, axis=0, keepdims=True)
    acc = alpha * acc + jnp.dot(v.T, pr.astype(BF16), preferred_element_type=F32)
    return m_new, l, acc


def _nsa_prompt_kernel(q_ref, ks_ref, vs_ref, kw_ref, vw_ref, bias_ref, mask_ref, ocmp_ref, gate_ref, o_ref):
    qi = pl.program_id(2)
    tq = q_ref.shape[0]
    ck = 2 * tq
    rows = NSA_GROUP * tq
    q4 = _heads_to_rows(q_ref)
    n_win = WINDOW // tq

    def bias_pair(kc, far):
        tiles = []
        for half in range(2):
            delta = qi - 2 * kc - half
            if far is not None:
                delta = jnp.minimum(delta, far)
            off = pl.multiple_of((delta + 1) * tq, tq)
            tile = bias_ref[:, pl.ds(off, tq), :]
            tiles.append(jnp.concatenate([tile[g] for g in range(NSA_GROUP)], axis=1))
        return jnp.concatenate(tiles, axis=0)

    def kv_chunk(k_ref, v_ref, kc):
        off = pl.multiple_of(kc * ck, ck)
        return k_ref[pl.ds(off, ck), :], v_ref[pl.ds(off, ck), :]

    init = (jnp.full((1, rows), NEG_INF, F32), jnp.zeros((1, rows), F32), jnp.zeros((HEAD_DIM, rows), F32))

    def sel_body(kc, carry):
        off = pl.multiple_of(kc * ck, ck)
        mb = mask_ref[pl.ds(off, ck), :].astype(F32)
        bias = bias_pair(kc, 2) + jnp.concatenate([mb] * NSA_GROUP, axis=1)
        k, v = kv_chunk(ks_ref, vs_ref, kc)
        return _softmax_step_km(q4, k, v, bias, carry)

    def win_body(kc, carry):
        k, v = kv_chunk(kw_ref, vw_ref, kc)
        return _softmax_step_km(q4, k, v, bias_pair(kc, None), carry)

    win_first = jnp.maximum(qi - n_win, 0) // 2
    sel_carry = lax.fori_loop(0, win_first, sel_body, init)
    (_, l_s, acc_s), (_, l_w, acc_w) = lax.fori_loop(
        win_first, qi // 2 + 1, lambda kc, c: (sel_body(kc, c[0]), win_body(kc, c[1])), (sel_carry, init))
    o_sel = acc_s / l_s
    o_win = acc_w / l_w

    gates = gate_ref[...]
    for g in range(NSA_GROUP):
        sl = slice(g * HEAD_DIM, (g + 1) * HEAD_DIM)
        rs = slice(g * tq, (g + 1) * tq)
        o = (gates[:, 3 * g:3 * g + 1] * ocmp_ref[:, sl] + gates[:, 3 * g + 1:3 * g + 2] * o_sel[:, rs].T
             + gates[:, 3 * g + 2:3 * g + 3] * o_win[:, rs].T)
        o_ref[:, sl] = o.astype(o_ref.dtype)


def nsa_prompt_attention(z16, z32, bias5, mask, o_cmp, bsz, seq):
    n = bsz * seq
    tq = PAGE
    nq = seq // tq
    grp = NSA_GROUP * HEAD_DIM
    seq_col = lambda col: pl.BlockSpec((seq, HEAD_DIM), lambda b, k, i: (b, col // HEAD_DIM + k))
    return pl.pallas_call(
        _nsa_prompt_kernel,
        grid=(bsz, NSA_KV_HEADS, nq),
        in_specs=[
            pl.BlockSpec((tq, grp), lambda b, k, i: (b * nq + i, COL_Q // grp + k)),
            seq_col(COL_KV + 2 * KV_LANES), seq_col(COL_KV + 3 * KV_LANES),
            seq_col(COL_WIN), seq_col(COL_WIN + KV_LANES),
            pl.BlockSpec((NSA_GROUP, bias5.shape[1], tq), lambda b, k, i: (k, 0, 0)),
            pl.BlockSpec((None, None, seq, tq), lambda b, k, i: (b, k, 0, i)),
            pl.BlockSpec((tq, grp), lambda b, k, i: (b * nq + i, k)),
            pl.BlockSpec((tq, LANE), lambda b, k, i: (b * nq + i, COL_GATE // LANE + k)),
        ],
        out_specs=pl.BlockSpec((tq, grp), lambda b, k, i: (b * nq + i, k)),
        out_shape=jax.ShapeDtypeStruct((n, NSA_HEADS * HEAD_DIM), BF16),
        compiler_params=_cparams("parallel", "parallel", "arbitrary"),
        name="nsa_prompt_attention",
    )(z16, z16, z16, z16, z16, bias5, mask, o_cmp, z32)


ROWS_PER_TOKEN = 4 * NSA_KV_HEADS


def _softmax_attend(q, k, v, bias):
    s = _nt_dot(q, k) + bias
    e = jnp.exp(s - jnp.max(s, axis=-1, keepdims=True))
    o = jnp.dot(e.astype(BF16), v, preferred_element_type=F32)
    return o / jnp.sum(e, axis=-1, keepdims=True), e


PAGE_BUFFERS = 16


def _low_half_mask():
    return lax.broadcasted_iota(jnp.int32, (SUBLANE, HEAD_DIM), 0) < NSA_KV_HEADS


def _split_token_pairs(toks):
    lo = _low_half_mask()
    a = [jnp.where(lo, toks[i], pltpu.roll(toks[i + 1], NSA_KV_HEADS, 0)) for i in (0, 2)]
    b = [jnp.where(lo, pltpu.roll(toks[i], NSA_KV_HEADS, 0), toks[i + 1]) for i in (0, 2)]
    return jnp.concatenate(a, axis=0).astype(BF16), jnp.concatenate(b, axis=0).astype(BF16)


def _nsa_sample_kernel(n_pages, n_sel, pt_ref, q_ref, pages_hbm, knew_ref, wst_ref, wnew_ref, bcmp_ref, bsel_ref,
                       bwin_ref, kpos_ref, wck_ref, wcv_ref, posk_ref, posv_ref, w2k_ref, w2v_ref, kn_ref, ovl_ref,
                       exp_ref, gate_ref, o_ref, wout_ref, xs_ref, ks_ref, vs_ref, kw_ref, vw_ref, pbuf_ref, sem_ref):
    b = pl.program_id(0)
    n_total = pl.num_programs(0) * n_pages
    n_past = n_pages * PAGE
    t = q_ref.shape[0]
    rpt = ROWS_PER_TOKEN
    kvh_n = NSA_KV_HEADS

    def page_copy(g):
        slot = g % PAGE_BUFFERS
        return pltpu.make_async_copy(pages_hbm.at[pt_ref[g // n_pages, g % n_pages]], pbuf_ref.at[slot],
                                     sem_ref.at[slot])

    @pl.when(b == 0)
    def _():
        for g in range(PAGE_BUFFERS):
            page_copy(g).start()

    def page_body(pg, carry):
        g = b * n_pages + pg
        page_copy(g).wait()
        page_ref = pbuf_ref.at[g % PAGE_BUFFERS]
        row0 = pl.multiple_of(pg * (CMP_PER_PAGE * kvh_n), CMP_PER_PAGE * kvh_n)
        for r in range(CMP_STRIDE):
            tl = [page_ref[pl.ds((CMP_STRIDE * hb + r) * rpt, SUBLANE), :] for hb in range(CMP_PER_PAGE)]
            for w in range(CMP_PER_PAGE // 4):
                xk, xv = _split_token_pairs(tl[4 * w:4 * w + 4])
                xs_ref[0, pl.ds(row0 + 16 * w, 16), r * HEAD_DIM:(r + 1) * HEAD_DIM] = xk
                xs_ref[1, pl.ds(row0 + 16 * w, 16), r * HEAD_DIM:(r + 1) * HEAD_DIM] = xv
        k0 = pl.multiple_of(pg * (PAGE * kvh_n), PAGE * kvh_n)
        for w in range(PAGE // 4):
            tl = [page_ref[pl.ds((4 * w + i) * rpt + SUBLANE, SUBLANE), :] for i in range(4)]
            kk, vv = _split_token_pairs(tl)
            ks_ref[pl.ds(k0 + 16 * w, 16), :] = kk
            vs_ref[pl.ds(k0 + 16 * w, 16), :] = vv

        @pl.when(g + PAGE_BUFFERS < n_total)
        def _():
            page_copy(g + PAGE_BUFFERS).start()

        return carry

    lax.fori_loop(0, n_pages, page_body, 0)
    _nsa_sample_finish(n_past, n_sel, t, q_ref, knew_ref, wst_ref, wnew_ref, bcmp_ref, bsel_ref, bwin_ref, kpos_ref,
                       wck_ref, wcv_ref, posk_ref, posv_ref, w2k_ref, w2v_ref, kn_ref, ovl_ref, exp_ref, gate_ref,
                       o_ref, xs_ref, ks_ref, vs_ref, kw_ref, vw_ref)

    rows_tok = 2 * kvh_n
    keep_rows = wst_ref.shape[0] - t * rows_tok
    wout_ref[0:keep_rows, :] = wst_ref[t * rows_tok:, :]
    wnew = wnew_ref[...]
    for j in range(rows_tok):
        wout_ref[pl.ds(keep_rows + j, t, stride=rows_tok), :] = wnew[:, j * HEAD_DIM:(j + 1) * HEAD_DIM]


def _nsa_sample_finish(n_past, n_sel, t, q_ref, knew_ref, wst_ref, wnew_ref, bcmp_ref, bsel_ref, bwin_ref, kpos_ref,
                       wck_ref, wcv_ref, posk_ref, posv_ref, w2k_ref, w2v_ref, kn_ref, ovl_ref, exp_ref, gate_ref,
                       o_ref, xs_ref, ks_ref, vs_ref, kw_ref, vw_ref):
    kvh_n = NSA_KV_HEADS

    def new_rows(x):
        rows = jnp.concatenate([x[:, k * HEAD_DIM:(k + 1) * HEAD_DIM] for k in range(kvh_n)], axis=0)
        return jnp.concatenate([rows, jnp.zeros((PAGE * kvh_n - rows.shape[0], HEAD_DIM), F32)], axis=0).astype(BF16)

    knew = knew_ref[...]
    ks_ref[kvh_n * n_past:kvh_n * (n_past + PAGE), :] = new_rows(knew[:, :KV_LANES])
    vs_ref[kvh_n * n_past:kvh_n * (n_past + PAGE), :] = new_rows(knew[:, KV_LANES:])

    q = q_ref[...]
    qall = jnp.concatenate([q[:, h * HEAD_DIM:(h + 1) * HEAD_DIM] for h in range(NSA_HEADS)], axis=0).astype(BF16)

    kc = _compress_finish(xs_ref[0], wck_ref[...], posk_ref[...], w2k_ref[...], kn_ref[...], kvh_minor=True)
    vc = _compress_finish(xs_ref[1], wcv_ref[...], posv_ref[...], w2v_ref[...], None, kvh_minor=True)
    o_cmp, e_c = _softmax_attend(qall, kc.astype(BF16), vc.astype(BF16), bcmp_ref[...])
    p_c = e_c / jnp.sum(e_c, axis=-1, keepdims=True)
    imp_h = _dot3_lhs(p_c, ovl_ref[...])
    imp = jnp.concatenate(
        [sum(imp_h[(k * NSA_GROUP + g) * t:(k * NSA_GROUP + g + 1) * t] for g in range(NSA_GROUP))
         for k in range(kvh_n)], axis=0)
    qpos = n_past + lax.broadcasted_iota(jnp.int32, (kvh_n * t, 1), 0) % t
    mb = _selected_key_mask(imp, exp_ref[...], qpos, kpos_ref[...], n_sel)
    mb = jnp.concatenate([mb[k * t:(k + 1) * t] for k in range(kvh_n) for _ in range(NSA_GROUP)], axis=0)

    o_sel, _ = _softmax_attend(qall, ks_ref[...], vs_ref[...], bsel_ref[...] + mb)

    n_st = wst_ref.shape[0] // (2 * kvh_n)

    def win_body(w, carry):
        base = pl.multiple_of(w * 32, 32)
        tl = [wst_ref[pl.ds(base + SUBLANE * i, SUBLANE), :] for i in range(4)]
        kk, vv = _split_token_pairs(tl)
        dst = pl.multiple_of(w * 16, 16)
        kw_ref[pl.ds(dst, 16), :] = kk
        vw_ref[pl.ds(dst, 16), :] = vv
        return carry

    lax.fori_loop(0, n_st // 4, win_body, 0, unroll=4)
    wnew = wnew_ref[...]
    kw_ref[kvh_n * n_st:kvh_n * (n_st + PAGE), :] = new_rows(wnew[:, :KV_LANES])
    vw_ref[kvh_n * n_st:kvh_n * (n_st + PAGE), :] = new_rows(wnew[:, KV_LANES:])
    o_win, _ = _softmax_attend(qall, kw_ref[...], vw_ref[...], bwin_ref[...])

    gates = gate_ref[...]
    for h in range(NSA_HEADS):
        kvh, g = divmod(h, NSA_GROUP)
        rs = slice(h * t, (h + 1) * t)
        gc = kvh * LANE + 3 * g
        o_ref[:, h * HEAD_DIM:(h + 1) * HEAD_DIM] = (
            gates[:, gc:gc + 1] * o_cmp[rs] + gates[:, gc + 1:gc + 2] * o_sel[rs] + gates[:, gc + 2:gc + 3] * o_win[rs])


def _sample_column_tables(bias_cmp, bias_sel, bias_win, seq, n_past, n_win):
    kvh_n = NSA_KV_HEADS
    row_kvh = np.arange(NSA_HEADS * seq) // (NSA_GROUP * seq)

    def columns(n_old):
        c = np.arange((n_old + PAGE) * kvh_n)
        old = c < n_old * kvh_n
        cn = c - n_old * kvh_n
        valid = old | (cn < kvh_n * seq)
        pos = np.where(old, c // kvh_n, n_old + cn % seq)
        kvh = np.where(old, c % kvh_n, cn // seq)
        return np.where(valid, pos, 0), kvh, valid

    def widen(bias, pos, kvh, valid):
        ok = valid[None, :] & (kvh[None, :] == row_kvh[:, None])
        return jnp.where(jnp.asarray(ok), jnp.take(bias, jnp.asarray(pos), axis=1), NEG_INF)

    cc = np.arange(N_CMP_PAD * kvh_n)
    b_cmp = widen(bias_cmp, cc // kvh_n, cc % kvh_n, np.ones_like(cc, bool))
    pos_s, kvh_s, valid_s = columns(n_past)
    b_sel = widen(bias_sel, pos_s, kvh_s, valid_s)
    pos_w, kvh_w, valid_w = columns(n_win)
    b_win = widen(bias_win, pos_w, kvh_w, valid_w)
    kpos = jnp.asarray(np.where(valid_s, pos_s, np.iinfo(np.int32).max)[None, :].astype(np.int32))
    ci = np.arange(N_CMP_PAD)[:, None] * CMP_STRIDE
    sj = np.arange(LANE)[None, :] * SEL_BLOCK
    overlap = np.repeat(((ci < sj + SEL_BLOCK) & (ci + CMP_BLOCK > sj)).astype(np.float32), kvh_n, axis=0)
    expand = ((pos_s[None, :] // SEL_BLOCK == np.arange(LANE)[:, None]) & valid_s[None, :]).astype(np.float32)
    return b_cmp, b_sel, b_win, kpos, jnp.asarray(overlap, BF16), jnp.asarray(expand, BF16)


def nsa_sample_attention(z32, pages, page_table, win_state, bias_cmp, bias_sel, bias_win, cmp_w, seq):
    bsz, n_pages = page_table.shape
    n = bsz * seq
    width = NSA_HEADS * HEAD_DIM
    kvh_n = NSA_KV_HEADS
    n_past = n_pages * PAGE
    n_win = win_state.shape[1] // (2 * kvh_n)
    n_sel = -(-(n_past + seq) // SEL_BLOCK)
    assert bsz * n_pages >= PAGE_BUFFERS and bias_sel.shape[1] == n_past + PAGE
    b_cmp, b_sel, b_win, kpos, overlap, expand = _sample_column_tables(bias_cmp, bias_sel, bias_win, seq, n_past, n_win)
    cst = lambda a: pl.BlockSpec(a.shape, lambda b, pt: (0,) * a.ndim, pipeline_mode=pl.Buffered(1))
    per_seq = lambda w, col: pl.BlockSpec((seq, w), lambda b, pt: (b, col // w))
    consts = [b_cmp, b_sel, b_win, kpos] + list(cmp_w) + [overlap, expand]
    return pl.pallas_call(
        functools.partial(_nsa_sample_kernel, n_pages, n_sel),
        grid_spec=pltpu.PrefetchScalarGridSpec(
            num_scalar_prefetch=1,
            grid=(bsz,),
            in_specs=[per_seq(width, COL_Q),
                      pl.BlockSpec(memory_space=pl.ANY),
                      per_seq(2 * KV_LANES, COL_KV + 2 * KV_LANES),
                      pl.BlockSpec((None, win_state.shape[1], HEAD_DIM), lambda b, pt: (b, 0, 0)),
                      per_seq(2 * KV_LANES, COL_WIN)]
                     + [cst(a) for a in consts] + [per_seq(kvh_n * LANE, COL_GATE)],
            out_specs=[per_seq(width, 0),
                       pl.BlockSpec((None, win_state.shape[1], HEAD_DIM), lambda b, pt: (b, 0, 0))],
            scratch_shapes=[pltpu.VMEM((2, kvh_n * N_CMP_PAD, CMP_STRIDE * HEAD_DIM), BF16),
                            pltpu.VMEM((kvh_n * (n_past + PAGE), HEAD_DIM), BF16),
                            pltpu.VMEM((kvh_n * (n_past + PAGE), HEAD_DIM), BF16),
                            pltpu.VMEM((kvh_n * (n_win + PAGE), HEAD_DIM), BF16),
                            pltpu.VMEM((kvh_n * (n_win + PAGE), HEAD_DIM), BF16),
                            pltpu.VMEM((PAGE_BUFFERS, PAGE * ROWS_PER_TOKEN, HEAD_DIM), F32),
                            pltpu.SemaphoreType.DMA((PAGE_BUFFERS,))],
        ),
        out_shape=[jax.ShapeDtypeStruct((n, width), F32), jax.ShapeDtypeStruct(win_state.shape, win_state.dtype)],
        compiler_params=_cparams("arbitrary"),
        name="nsa_sample_attention",
    )(page_table, z32, pages, z32, win_state, z32, *consts, z32)


SCALE = HEAD_DIM ** -0.5


def _layer_weights(i, ffn1_w_gate, ffn1_w_up, ffn1_w_down, w_in, w_out, nsa_q_norm, nsa_k_norm,
                   ffn2_w_gate, ffn2_w_up, ffn2_w_down, ple_w_gate, ple_w_proj):
    wi = w_in[i]
    wgate = wi[:, COL_GATE:].reshape(-1, NSA_KV_HEADS, 3 * NSA_GROUP)
    wgate = jnp.pad(wgate, ((0, 0), (0, 0), (0, LANE - 3 * NSA_GROUP))).reshape(-1, NSA_KV_HEADS * LANE)
    w_mix = wi.astype(BF16)
    t_q, t_ks, t_kw = COL_Q // PROJ_TILE, (COL_KV + 2 * KV_LANES) // PROJ_TILE, COL_WIN // PROJ_TILE
    n_q = NSA_WIDTH // PROJ_TILE
    gains = jnp.ones((Z_COLS // PROJ_TILE, 1, HEAD_DIM), F32)
    gains = gains.at[t_q:t_q + n_q].set(nsa_q_norm[i].astype(F32) * SCALE)
    gains = gains.at[t_ks].set(nsa_k_norm[i, 1].astype(F32)).at[t_kw].set(nsa_k_norm[i, 2].astype(F32))
    return dict(
        ffn1=(ffn1_w_gate[i].astype(BF16), ffn1_w_up[i].astype(BF16), ffn1_w_down[i].astype(BF16)),
        ffn2=(ffn2_w_gate[i].astype(BF16), ffn2_w_up[i].astype(BF16), ffn2_w_down[i].astype(BF16)),
        w_mix=w_mix, w_mix_gate=wgate.astype(BF16), mix_gains=gains,
        norm_tiles=tuple(range(t_q, t_q + n_q)) + (t_ks, t_kw),
        gate_tile=COL_GATE // PROJ_TILE,
        wo_h=w_out[i, :HGRN_WIDTH].astype(BF16), wo_n=w_out[i, HGRN_WIDTH:].astype(BF16),
        ple_gate=ple_w_gate[i].astype(BF16), ple_proj=ple_w_proj[i].astype(BF16),
    )


def _run_layer(i, w, x, pemb, bsz, seq, s0, nsa_fn, norms, hgrn_lb):
    ffn1_norm, mix_norm, hgrn_out_norm, ffn2_norm, ple_norm, ple_post_norm = norms
    x1 = ffn_residual(x, ffn1_norm[i], *w["ffn1"])
    z32, z16, kv_rows = mixer_project(x1, mix_norm[i], w["w_mix"], w["w_mix_gate"], w["mix_gains"], w["norm_tiles"],
                                      w["gate_tile"], COL_KV // PROJ_TILE, (COL_WIN - COL_KV) // PROJ_TILE)
    o_h, s_fin = hgrn2_mix(z32, hgrn_lb, hgrn_out_norm[i], i, bsz, seq, HGRN_HEADS, s0=s0,
                           heads_per_step=HGRN_HEADS if seq % 16 else 8)
    o_n, win_state_new = nsa_fn(z32, z16)
    x2 = out_project_residual(x1, o_h, o_n, w["wo_h"], w["wo_n"])
    x3 = ffn_residual(x2, ffn2_norm[i], *w["ffn2"])
    y = ple_residual(x3, ple_norm[i], w["ple_gate"], pemb, w["ple_proj"], ple_post_norm[i])
    win_rows = z32[:, COL_WIN:COL_GATE] if win_state_new is None else win_state_new
    return y, kv_rows, win_rows, s_fin


def kernel(x_prompt, x_sample, cache_kv, state_win_kv, state_hgrn, page_table, p_prompt, p_sample, ffn1_norm, ffn1_w_gate, ffn1_w_up, ffn1_w_down, mix_norm, w_in, w_out, hgrn_lb, hgrn_out_norm, nsa_q_norm, nsa_k_norm, cmp_pos, cmp_w1, cmp_w2, rel_bias_table, ffn2_norm, ffn2_w_gate, ffn2_w_up, ffn2_w_down, ple_norm, ple_w_gate, ple_w_proj, ple_post_norm):
    depth = cache_kv.shape[0]
    bp, tp, d = x_prompt.shape
    bs, ts, _ = x_sample.shape
    n_pool = cache_kv.shape[1]
    n_pages = page_table.shape[1]
    past = n_pages * PAGE
    win_keep = state_win_kv.shape[2]
    assert tp % PAGE == 0 and tp >= WINDOW and win_keep == WINDOW
    norms = (ffn1_norm, mix_norm, hgrn_out_norm, ffn2_norm, ple_norm, ple_post_norm)
    table = rel_bias_table.astype(F32)

    n_cmp = (tp - CMP_BLOCK) // CMP_STRIDE + 1
    bias_pc = rel_bias(table, tp, N_CMP_PAD, 0, CMP_BLOCK - 1, CMP_STRIDE, n_cmp)
    bias_p5 = rel_bias(table, WINDOW + 3 * PAGE, PAGE, -PAGE, 0, 1, PAGE, window=WINDOW,
                       key_major_tiles=True)
    n_cmp_s = (past + ts - CMP_BLOCK) // CMP_STRIDE + 1
    assert n_cmp_s <= N_CMP_PAD - 1 and (n_cmp_s - 1) * CMP_STRIDE + CMP_BLOCK <= past
    sel_cols = past + PAGE
    bias_sc = rel_bias(table, ts, N_CMP_PAD, past, CMP_BLOCK - 1, CMP_STRIDE, n_cmp_s).reshape(NSA_HEADS * ts, N_CMP_PAD)
    bias_ss = rel_bias(table, ts, sel_cols, past, 0, 1, past + ts).reshape(NSA_HEADS * ts, sel_cols)
    bias_sw = rel_bias(table, ts, win_keep + PAGE, win_keep, 0, 1, win_keep + ts, window=WINDOW)
    bias_sw = bias_sw.reshape(NSA_HEADS * ts, win_keep + PAGE)

    xp = x_prompt.reshape(bp * tp, d)
    xs = x_sample.reshape(bs * ts, d)
    outs = [[] for _ in range(6)]
    for i in range(depth):
        w = _layer_weights(i, ffn1_w_gate, ffn1_w_up, ffn1_w_down, w_in, w_out, nsa_q_norm, nsa_k_norm,
                           ffn2_w_gate, ffn2_w_up, ffn2_w_down, ple_w_gate, ple_w_proj)
        cmp_w = compress_weights(cmp_w1[i], cmp_w2[i], cmp_pos[i], nsa_k_norm[i, 0])

        def nsa_prompt(z32, z16):
            pages = z32.reshape(bp * tp // PAGE, PAGE, Z_COLS)
            pt = jnp.arange(bp * tp // PAGE, dtype=jnp.int32).reshape(bp, tp // PAGE)
            kc, vc = compress_cache(pages, pt, COL_KV // HEAD_DIM, cmp_w)
            o_cmp, mask = cmp_attention_topk(z16, kc, vc, bias_pc, bp, tp, 0, tp)
            return nsa_prompt_attention(z16, z32, bias_p5, mask, o_cmp, bp, tp), None

        def nsa_sample(z32, z16):
            pages = cache_kv[i].reshape(n_pool, PAGE * ROWS_PER_TOKEN, HEAD_DIM)
            wst = state_win_kv[i].reshape(bs, win_keep * 2 * NSA_KV_HEADS, HEAD_DIM)
            return nsa_sample_attention(z32, pages, page_table, wst, bias_sc, bias_ss, bias_sw, cmp_w, ts)

        xp, kv_p, win_p, h_p = _run_layer(i, w, xp, p_prompt[i].reshape(bp * tp, -1), bp, tp, None, nsa_prompt, norms, hgrn_lb)
        xs, kv_s, win_s, h_s = _run_layer(i, w, xs, p_sample[i].reshape(bs * ts, -1), bs, ts, state_hgrn[i], nsa_sample, norms, hgrn_lb)
        outs[0].append(kv_p.reshape(bp, tp, 4, NSA_KV_HEADS, HEAD_DIM))
        outs[1].append(win_p.reshape(bp, tp, 2, NSA_KV_HEADS, HEAD_DIM)[:, -WINDOW:])
        outs[2].append(h_p)
        outs[3].append(kv_s.reshape(bs, ts, 4, NSA_KV_HEADS, HEAD_DIM))
        outs[4].append(win_s.reshape(bs, win_keep, 2, NSA_KV_HEADS, HEAD_DIM))
        outs[5].append(h_s.astype(state_hgrn.dtype))
    return (xp.reshape(bp, tp, d), xs.reshape(bs, ts, d)) + tuple(jnp.stack(o) for o in outs)
```
